```python
import math
import jax
import jax.numpy as jnp
from jax import lax
import numpy as np

D_MODEL = 1024
BATCH = 4
SEQ = 4096
DEPTH = 2
DEC_BATCH = 32
DEC_SEQ = 4
PAST_LEN = 8192
PAGE_SIZE = 128

N_A_LAYERS = DEPTH // 2
N_B_LAYERS = DEPTH - N_A_LAYERS
D_RNN = D_MODEL
LRU_BLOCKS = 4
LRU_BLOCK_W = D_RNN // LRU_BLOCKS
CONV_W = 4
RGLRU_C = 8.0
N_HEADS = 16
HEAD_DIM = D_MODEL // N_HEADS
Q_BLOCK = 128
SB_BIAS_NEAR = -3.0
SB_BIAS_FAR = -9.0
N_EXPERTS = 64
TOP_K = 8
N_EXPERT_GROUPS = 8
TOPK_GROUPS = 4
D_EXPERT = 256
D_SHARED = 256
ROUTED_SCALE = 2.5
MOE_BLOCK = 128
EPS = 1e-6

kernel_name = 'yoco_rglru_stickbreaking_moe_step'


def rms_norm(x, g):
    xf = x.astype(jnp.float32)
    y = xf * lax.rsqrt(jnp.mean(xf * xf, axis=-1, keepdims=True) + EPS)
    return (y * g.astype(jnp.float32)).astype(x.dtype)


def modulation(c, w, b, n):
    m = jax.nn.silu(c) @ w + b
    return jnp.split(m[:, None, :], n, axis=-1)


def ada_norm(x, g, shift, scale):
    return rms_norm(x, g) * (1.0 + scale) + shift


def causal_conv(xb, hist, w, b):
    T = xb.shape[1]
    xpad = jnp.concatenate([hist.astype(xb.dtype), xb], axis=1)
    y = b + w[0] * xpad[:, 0:T]
    for j in range(1, CONV_W):
        y = y + w[j] * xpad[:, j:j + T]
    return y, xpad[:, T:]


def rg_lru(x, h0, w_r, b_r, w_i, b_i, lam):
    B, T, _ = x.shape
    xf = x.astype(jnp.float32)
    xblk = xf.reshape(B, T, LRU_BLOCKS, LRU_BLOCK_W)
    r = jax.nn.sigmoid(jnp.einsum('btnk,nkj->btnj', xblk, w_r.astype(jnp.float32)).reshape(B, T, D_RNN) + b_r.astype(jnp.float32))
    i = jax.nn.sigmoid(jnp.einsum('btnk,nkj->btnj', xblk, w_i.astype(jnp.float32)).reshape(B, T, D_RNN) + b_i.astype(jnp.float32))
    log_a = -RGLRU_C * r * jax.nn.softplus(-lam.astype(jnp.float32))
    a = jnp.exp(log_a)
    u = jnp.sqrt(-jnp.expm1(2.0 * log_a)) * (i * xf)
    u = u.at[:, 0].add(a[:, 0] * h0.astype(jnp.float32))

    def comb(left, right):
        return (left[0] * right[0], right[0] * left[1] + right[1])

    _, h = lax.associative_scan(comb, (a, u), axis=1)
    return h, h[:, -1]


def recurrent_block(hn, conv_hist, h0, w_in, conv_w, conv_b, w_r, b_r, w_i, b_i, lam, w_out):
    proj = hn @ w_in
    gate_br, xb = jnp.split(proj, 2, axis=-1)
    xc, new_hist = causal_conv(xb, conv_hist, conv_w, conv_b)
    h, h_last = rg_lru(xc, h0, w_r, b_r, w_i, b_i, lam)
    y = (jax.nn.gelu(gate_br, approximate=True) * h.astype(hn.dtype)) @ w_out
    return y, new_hist, h_last


def sb_block(q, k, v, q_pos, bias):
    z = jnp.einsum('bqhd,bshd->bhqs', q, k).astype(jnp.float32) * (HEAD_DIM ** -0.5)
    z = z + bias.astype(jnp.float32)[None, :, None, None]
    k_pos = jnp.arange(k.shape[1])
    causal = k_pos[None, :] < q_pos[:, None]
    log_1m = jnp.where(causal, jax.nn.log_sigmoid(-z), 0.0)
    tail = lax.cumsum(log_1m, axis=3, reverse=True) - log_1m
    w = jnp.where(causal, jnp.exp(jax.nn.log_sigmoid(z) + tail), 0.0)
    return jnp.einsum('bhqs,bshd->bqhd', w.astype(v.dtype), v)


def sb_attention(q, k, v, q_start, bias):
    B, Tq, H, Dh = q.shape
    qb_len = Q_BLOCK if Tq % Q_BLOCK == 0 else Tq
    nb = Tq // qb_len
    qb = q.reshape(B, nb, qb_len, H, Dh).swapaxes(0, 1)
    pos = (q_start + jnp.arange(Tq)).reshape(nb, qb_len)
    out = lax.map(lambda a: sb_block(a[0], k, v, a[1], bias), (qb, pos))
    return out.swapaxes(0, 1).reshape(B, Tq, H, Dh)


def swiglu(x, w_gu, w_down):
    g, u = jnp.split(x @ w_gu, 2, axis=-1)
    return (jax.nn.silu(g) * u) @ w_down


def routed_experts(xf, eidx, ew, w_gu, w_down):
    N, D = xf.shape
    M = N * TOP_K
    flat_e = eidx.reshape(-1).astype(jnp.int32)
    order = jnp.argsort(flat_e)
    sorted_e = flat_e[order]
    counts = jnp.zeros((N_EXPERTS,), jnp.int32).at[flat_e].add(1)
    padded = (counts + MOE_BLOCK - 1) // MOE_BLOCK * MOE_BLOCK
    pad_end = jnp.cumsum(padded)
    pad_start = pad_end - padded
    start = jnp.cumsum(counts) - counts
    dest_sorted = (pad_start[sorted_e] + jnp.arange(M, dtype=jnp.int32) - start[sorted_e]).astype(jnp.int32)
    dest = jnp.zeros((M,), jnp.int32).at[order].set(dest_sorted)
    n_blocks = (M + N_EXPERTS * (MOE_BLOCK - 1)) // MOE_BLOCK
    rows = n_blocks * MOE_BLOCK
    row_tok = jnp.full((rows,), N, jnp.int32).at[dest].set(jnp.arange(M, dtype=jnp.int32) // TOP_K)
    xpad = jnp.concatenate([xf, jnp.zeros((1, D), xf.dtype)], axis=0)
    xb = xpad[row_tok].reshape(n_blocks, MOE_BLOCK, D)
    blk_e = jnp.minimum(jnp.searchsorted(pad_end, jnp.arange(n_blocks) * MOE_BLOCK, side='right'), N_EXPERTS - 1)

    def one_block(args):
        xblk, e = args
        return swiglu(xblk, w_gu[e], w_down[e])

    yb = lax.map(one_block, (xb, blk_e)).reshape(rows, D)
    y_pairs = yb[dest].reshape(N, TOP_K, D)
    return jnp.einsum('nk,nkd->nd', ew.astype(yb.dtype), y_pairs)


def moe(h, w_router, r_bias, w_gu, w_down, ws_gu, ws_down):
    B, T, D = h.shape
    xf = h.reshape(B * T, D)
    scores = jax.nn.sigmoid((xf @ w_router).astype(jnp.float32))
    sel = scores + r_bias.astype(jnp.float32)
    grp = sel.reshape(-1, N_EXPERT_GROUPS, N_EXPERTS // N_EXPERT_GROUPS)
    grp_score = lax.top_k(grp, 2)[0].sum(-1)
    _, gidx = lax.top_k(grp_score, TOPK_GROUPS)
    gmask = jax.nn.one_hot(gidx, N_EXPERT_GROUPS, dtype=jnp.float32).sum(1) > 0
    emask = jnp.repeat(gmask, N_EXPERTS // N_EXPERT_GROUPS, axis=1)
    _, eidx = lax.top_k(jnp.where(emask, sel, -jnp.inf), TOP_K)
    ew = jnp.take_along_axis(scores, eidx, axis=1)
    ew = ew / jnp.sum(ew, axis=-1, keepdims=True) * ROUTED_SCALE
    y = routed_experts(xf, eidx, ew, w_gu, w_down) + swiglu(xf, ws_gu, ws_down)
    return y.reshape(B, T, D)


def setup_inputs(seed: int = 0) -> dict:
    key = jax.random.key(seed)
    ks = iter(jax.random.split(key, 64))
    f32 = jnp.float32
    NA, NB, D = N_A_LAYERS, N_B_LAYERS, D_MODEL

    def nrm(shape, scale):
        return jax.random.normal(next(ks), shape, f32) * scale

    def gain(shape):
        return 1.0 + nrm(shape, 0.05)

    n_pages = PAST_LEN // PAGE_SIZE
    n_pool = (DEC_BATCH * n_pages * 5) // 4
    page_table = jax.random.permutation(next(ks), n_pool)[:DEC_BATCH * n_pages].reshape(DEC_BATCH, n_pages).astype(jnp.int32)
    u = jax.random.uniform(next(ks), (NA, D_RNN), f32, 0.9, 0.999)
    s = u ** (1.0 / RGLRU_C)
    a_lambda = jnp.log(s) - jnp.log1p(-s)
    sb_bias = jnp.linspace(SB_BIAS_NEAR, SB_BIAS_FAR, N_HEADS, dtype=f32)[None, :] + nrm((NB, N_HEADS), 0.1)
    mod = 0.5 * D ** -0.5
    return {
        'x_prompt': nrm((BATCH, SEQ, D), 1.0),
        'x_sample': nrm((DEC_BATCH, DEC_SEQ, D), 1.0),
        'c_prompt': nrm((BATCH, D), 1.0),
        'c_sample': nrm((DEC_BATCH, D), 1.0),
        'state_conv': nrm((NA, DEC_BATCH, CONV_W - 1, D_RNN), 1.0),
        'state_h': nrm((NA, DEC_BATCH, D_RNN), 0.5),
        'cache_k': nrm((n_pool, PAGE_SIZE, N_HEADS, HEAD_DIM), 1.0),
        'cache_v': nrm((n_pool, PAGE_SIZE, N_HEADS, HEAD_DIM), 1.0),
        'page_table': page_table,
        'a_norm_g': gain((NA, D)),
        'a_mod_w': nrm((NA, D, 3 * D), mod),
        'a_mod_b': nrm((NA, 3 * D), 0.02),
        'a_w_in': nrm((NA, D, 2 * D_RNN), D ** -0.5),
        'a_conv_w': nrm((NA, CONV_W, D_RNN), CONV_W ** -0.5),
        'a_conv_b': nrm((NA, D_RNN), 0.02),
        'a_w_gate_r': nrm((NA, LRU_BLOCKS, LRU_BLOCK_W, LRU_BLOCK_W), LRU_BLOCK_W ** -0.5),
        'a_b_gate_r': nrm((NA, D_RNN), 0.02),
        'a_w_gate_i': nrm((NA, LRU_BLOCKS, LRU_BLOCK_W, LRU_BLOCK_W), LRU_BLOCK_W ** -0.5),
        'a_b_gate_i': nrm((NA, D_RNN), 0.02),
        'a_lambda': a_lambda,
        'a_w_out': nrm((NA, D_RNN, D), D_RNN ** -0.5),
        'kv_norm_g': gain((D,)),
        'kv_mod_w': nrm((D, 2 * D), mod),
        'kv_mod_b': nrm((2 * D,), 0.02),
        'w_kv': nrm((D, 2 * N_HEADS * HEAD_DIM), D ** -0.5),
        'b_norm_g': gain((NB, D)),
        'b_mod_w': nrm((NB, D, 3 * D), mod),
        'b_mod_b': nrm((NB, 3 * D), 0.02),
        'b_w_q': nrm((NB, D, N_HEADS * HEAD_DIM), D ** -0.5),
        'b_sb_bias': sb_bias,
        'b_w_o': nrm((NB, N_HEADS * HEAD_DIM, D), (N_HEADS * HEAD_DIM) ** -0.5),
        'm_norm_g': gain((DEPTH, D)),
        'm_mod_w': nrm((DEPTH, D, 3 * D), mod),
        'm_mod_b': nrm((DEPTH, 3 * D), 0.02),
        'm_w_router': nrm((DEPTH, D, N_EXPERTS), D ** -0.5),
        'm_router_bias': nrm((DEPTH, N_EXPERTS), 0.01),
        'm_w_gate_up': nrm((DEPTH, N_EXPERTS, D, 2 * D_EXPERT), D ** -0.5),
        'm_w_down': nrm((DEPTH, N_EXPERTS, D_EXPERT, D), D_EXPERT ** -0.5),
        'm_ws_gate_up': nrm((DEPTH, D, 2 * D_SHARED), D ** -0.5),
        'm_ws_down': nrm((DEPTH, D_SHARED, D), D_SHARED ** -0.5),
        'final_norm_g': gain((D,)),
    }


def reference(x_prompt, x_sample, c_prompt, c_sample, state_conv, state_h, cache_k, cache_v, page_table,
              a_norm_g, a_mod_w, a_mod_b, a_w_in, a_conv_w, a_conv_b, a_w_gate_r, a_b_gate_r, a_w_gate_i,
              a_b_gate_i, a_lambda, a_w_out, kv_norm_g, kv_mod_w, kv_mod_b, w_kv, b_norm_g, b_mod_w, b_mod_b,
              b_w_q, b_sb_bias, b_w_o, m_norm_g, m_mod_w, m_mod_b, m_w_router, m_router_bias, m_w_gate_up,
              m_w_down, m_ws_gate_up, m_ws_down, final_norm_g):

    def run_group(x, c, conv_hist, h0, past_k, past_v):
        B, T, _ = x.shape
        q_start = past_k.shape[1]
        new_conv = []
        new_h = []
        keys = vals = k_rows = v_rows = None
        for layer in range(DEPTH):
            if layer < N_A_LAYERS:
                i = layer
                sh, sc, gt = modulation(c, a_mod_w[i], a_mod_b[i], 3)
                hn = ada_norm(x, a_norm_g[i], sh, sc)
                y, cs, hs = recurrent_block(hn, conv_hist[i], h0[i], a_w_in[i], a_conv_w[i], a_conv_b[i],
                                            a_w_gate_r[i], a_b_gate_r[i], a_w_gate_i[i], a_b_gate_i[i],
                                            a_lambda[i], a_w_out[i])
                new_conv.append(cs)
                new_h.append(hs)
                x = x + gt * y
            else:
                if layer == N_A_LAYERS:
                    sh, sc = modulation(c, kv_mod_w, kv_mod_b, 2)
                    kvn = ada_norm(x, kv_norm_g, sh, sc)
                    kv = (kvn @ w_kv).reshape(B, T, 2, N_HEADS, HEAD_DIM)
                    k_rows = kv[:, :, 0]
                    v_rows = kv[:, :, 1]
                    keys = jnp.concatenate([past_k.astype(k_rows.dtype), k_rows], axis=1)
                    vals = jnp.concatenate([past_v.astype(v_rows.dtype), v_rows], axis=1)
                j = layer - N_A_LAYERS
                sh, sc, gt = modulation(c, b_mod_w[j], b_mod_b[j], 3)
                hn = ada_norm(x, b_norm_g[j], sh, sc)
                q = (hn @ b_w_q[j]).reshape(B, T, N_HEADS, HEAD_DIM)
                o = sb_attention(q, keys, vals, q_start, b_sb_bias[j])
                x = x + gt * (o.reshape(B, T, N_HEADS * HEAD_DIM) @ b_w_o[j])
            sh, sc, gt = modulation(c, m_mod_w[layer], m_mod_b[layer], 3)
            hn = ada_norm(x, m_norm_g[layer], sh, sc)
            x = x + gt * moe(hn, m_w_router[layer], m_router_bias[layer], m_w_gate_up[layer],
                             m_w_down[layer], m_ws_gate_up[layer], m_ws_down[layer])
        y = rms_norm(x, final_norm_g)
        return y, jnp.stack(new_conv), jnp.stack(new_h), k_rows, v_rows

    bp = x_prompt.shape[0]
    conv0 = jnp.zeros((N_A_LAYERS, bp, CONV_W - 1, D_RNN), x_prompt.dtype)
    h0 = jnp.zeros((N_A_LAYERS, bp, D_RNN), jnp.float32)
    empty = jnp.zeros((bp, 0, N_HEADS, HEAD_DIM), x_prompt.dtype)
    y_p, conv_p, h_p, k_p, v_p = run_group(x_prompt, c_prompt, conv0, h0, empty, empty)

    bs = x_sample.shape[0]
    n_pages = page_table.shape[1]
    past_k = cache_k[page_table].reshape(bs, n_pages * PAGE_SIZE, N_HEADS, HEAD_DIM)
    past_v = cache_v[page_table].reshape(bs, n_pages * PAGE_SIZE, N_HEADS, HEAD_DIM)
    y_s, conv_s, h_s, k_s, v_s = run_group(x_sample, c_sample, state_conv, state_h, past_k, past_v)

    return (y_p, y_s, conv_p, h_p, k_p, v_p, conv_s, h_s, k_s, v_s)
```

```python
import functools

import jax
import jax.numpy as jnp
from jax import lax
from jax.experimental import pallas as pl
from jax.experimental.pallas import tpu as pltpu

F32 = jnp.float32
BF16 = jnp.bfloat16
I32 = jnp.int32

CONV_W = 4
LRU_BLOCKS = 4
RGLRU_C = 8.0
N_HEADS = 16
N_EXPERTS = 64
TOP_K = 8
N_EXPERT_GROUPS = 8
TOPK_GROUPS = 4
ROUTED_SCALE = 2.5
EPS = 1e-6
NEG_INF = float("-inf")

LANES = 128
SUBLANES = 8
VMEM_LIMIT_BYTES = 56 * 1024 * 1024

REC_ROWS = 256
ROUTER_ROWS = 512
PROJ_ROWS = 512
DISPATCH_ROWS = 256
COMBINE_ROWS = 128
EXPERT_ROWS = 256
ATTN_Q_ROWS = 128
ATTN_K_ROWS = 128


def _params(*sem):
    return pltpu.CompilerParams(dimension_semantics=sem, vmem_limit_bytes=VMEM_LIMIT_BYTES)


def _dot(a, b):
    return jnp.dot(a, b, preferred_element_type=F32)


def _dot_nt(a, b, precision=None):
    return lax.dot_general(a, b, (((1,), (1,)), ((), ())), precision=precision,
                           preferred_element_type=F32)


def _sigmoid(x):
    return 1.0 / (1.0 + jnp.exp(-x))


def _silu(x):
    return x * _sigmoid(x)


def _gelu_tanh(x):
    return 0.5 * x * (1.0 + jnp.tanh(0.7978845608028654 * (x + 0.044715 * (x * x * x))))


def _rms(x, g):
    ms = jnp.mean(x * x, axis=-1, keepdims=True)
    return x * lax.rsqrt(ms + EPS) * g


def _ada(x, g, shift, scale):
    return _rms(x, g) * (1.0 + scale) + shift


def _swiglu(xb, w_gu, w_down):
    gu = _dot(xb, w_gu)
    half = gu.shape[-1] // 2
    act = _silu(gu[:, :half]) * gu[:, half:]
    return _dot(act.astype(BF16), w_down)


def _mod_kernel(c_ref, w_ref, b_ref, o_ref):
    c = c_ref[...]
    o_ref[...] = _dot(_silu(c).astype(BF16), w_ref[...].astype(BF16)) + b_ref[...]


def _modulation(c_all, w, b):
    rows, d = c_all.shape
    n_out = w.shape[1]
    tn = 1024
    return pl.pallas_call(
        _mod_kernel,
        grid=(n_out // tn,),
        in_specs=[pl.BlockSpec((rows, d), lambda j: (0, 0)),
                  pl.BlockSpec((d, tn), lambda j: (0, j)),
                  pl.BlockSpec((1, tn), lambda j: (0, j))],
        out_specs=pl.BlockSpec((rows, tn), lambda j: (0, j)),
        out_shape=jax.ShapeDtypeStruct((rows, n_out), F32),
        compiler_params=_params("arbitrary"),
        name="modulation",
    )(c_all, w, b.reshape(1, n_out))


class _Group:
    def __init__(self, n_rows, rows, blocks_per_seq, mod_rows):
        self.n_rows = n_rows
        self.rows = rows
        self.blocks_per_seq = blocks_per_seq
        self.mod_rows = mod_rows
        self.n_blocks = n_rows // rows

    def mod_spec(self, col, d):
        bps = self.blocks_per_seq
        return pl.BlockSpec((1, self.mod_rows, d), lambda i: (i // bps, 0, col))

    def row_spec(self, width):
        return pl.BlockSpec((self.rows, width), lambda i: (i, 0))


def _const_spec(shape):
    nd = len(shape)
    return pl.BlockSpec(shape, lambda *_: (0,) * nd)


def _scan_rows(a, u, stride, steps):
    row = lax.broadcasted_iota(I32, a.shape, 0)
    s = 1
    while s < steps:
        sh = s * stride
        a_p = pltpu.roll(a, sh, 0)
        u_p = pltpu.roll(u, sh, 0)
        m = row >= sh
        u = jnp.where(m, a * u_p + u, u)
        a = jnp.where(m, a * a_p, a)
        s *= 2
    return a, u


def _rec_kernel(x_ref, sh_ref, sc_ref, gt_ref, hist_ref, h0_ref, g_ref, w_in_ref, cw_ref, cb_ref,
                wr_ref, br_ref, wi_ref, bi_ref, lam_ref, w_out_ref,
                xo_ref, conv_ref, hl_ref, xpad_sc, h_sc, *, stride, steps, hist0):
    t = pl.program_id(1)
    rows = stride * steps
    hrows = (CONV_W - 1) * stride
    d_rnn = h_sc.shape[-1]

    @pl.when(t == 0)
    def _():
        xpad_sc[pl.ds(hist0 - hrows, hrows), :] = hist_ref[0]
        h_sc[...] = h0_ref[0]

    x = x_ref[0]
    hn = _ada(x, g_ref[...], sh_ref[0], sc_ref[0])
    proj = _dot(hn.astype(BF16), w_in_ref[...])
    gate_br = proj[:, :d_rnn]
    xb = proj[:, d_rnn:]
    xpad_sc[pl.ds(hist0, rows), :] = xb
    xc = cb_ref[...] + cw_ref[0:1, :] * xpad_sc[pl.ds(hist0 - hrows, rows), :]
    for j in range(1, CONV_W):
        xc = xc + cw_ref[j:j + 1, :] * xpad_sc[pl.ds(hist0 - hrows + j * stride, rows), :]
    new_hist = xpad_sc[pl.ds(hist0 + rows - hrows, hrows), :]
    xpad_sc[pl.ds(hist0 - hrows, hrows), :] = new_hist
    conv_ref[0] = new_hist

    bw = d_rnn // LRU_BLOCKS
    r_parts, i_parts = [], []
    for n in range(LRU_BLOCKS):
        xblk = xc[:, n * bw:(n + 1) * bw].astype(BF16)
        r_parts.append(_dot(xblk, wr_ref[n]))
        i_parts.append(_dot(xblk, wi_ref[n]))
    r = _sigmoid(jnp.concatenate(r_parts, axis=-1) + br_ref[...])
    ig = _sigmoid(jnp.concatenate(i_parts, axis=-1) + bi_ref[...])
    neg_lam = -lam_ref[...]
    softplus = jnp.maximum(neg_lam, 0.0) + jnp.log1p(jnp.exp(-jnp.abs(neg_lam)))
    log_a = (-RGLRU_C * r) * softplus
    a = jnp.exp(log_a)
    u = jnp.sqrt(-jnp.tanh(log_a) * (a * a + 1.0)) * (ig * xc)
    a_cum, u_cum = _scan_rows(a, u, stride, steps)
    h_prev = h_sc[...]
    if stride == 1:
        h = a_cum * h_prev + u_cum
    else:
        h = a_cum * jnp.concatenate([h_prev] * steps, axis=0) + u_cum
    h_last = h[rows - stride:, :]
    h_sc[...] = h_last
    hl_ref[0] = h_last
    y = _dot((_gelu_tanh(gate_br) * h).astype(BF16), w_out_ref[...])
    xo_ref[0] = x + gt_ref[0] * y


def _recurrent(x3, mod, hist, h0, g, w_in, conv_w, conv_b, w_r, b_r, w_i, b_i, lam, w_out,
               *, stride, steps, mod_rows):
    nb, t_rows, d = x3.shape
    d_rnn = w_out.shape[0]
    rows = stride * steps
    nt = t_rows // rows
    hrows = (CONV_W - 1) * stride
    hist0 = -(-hrows // SUBLANES) * SUBLANES
    assert rows >= hrows and rows % SUBLANES == 0
    x_spec = pl.BlockSpec((1, rows, d), lambda b, t: (b, t, 0))

    def mspec(col):
        if mod_rows == 1:
            return pl.BlockSpec((1, 1, d), lambda b, t: (b, 0, col))
        return pl.BlockSpec((1, rows, d), lambda b, t: (b, 0, col))

    def cspec(shape):
        nd = len(shape)
        return pl.BlockSpec(shape, lambda b, t: (0,) * nd)

    kern = functools.partial(_rec_kernel, stride=stride, steps=steps, hist0=hist0)
    return pl.pallas_call(
        kern,
        grid=(nb, nt),
        in_specs=[x_spec, mspec(0), mspec(1), mspec(2),
                  pl.BlockSpec((1, hrows, d_rnn), lambda b, t: (b, 0, 0)),
                  pl.BlockSpec((1, stride, d_rnn), lambda b, t: (b, 0, 0)),
                  cspec((1, d)), cspec(w_in.shape), cspec(conv_w.shape), cspec((1, d_rnn)),
                  cspec(w_r.shape), cspec((1, d_rnn)), cspec(w_i.shape), cspec((1, d_rnn)),
                  cspec((1, d_rnn)), cspec(w_out.shape)],
        out_specs=[x_spec,
                   pl.BlockSpec((1, hrows, d_rnn), lambda b, t: (b, 0, 0)),
                   pl.BlockSpec((1, stride, d_rnn), lambda b, t: (b, 0, 0))],
        out_shape=[jax.ShapeDtypeStruct(x3.shape, F32),
                   jax.ShapeDtypeStruct((nb, hrows, d_rnn), F32),
                   jax.ShapeDtypeStruct((nb, stride, d_rnn), F32)],
        scratch_shapes=[pltpu.VMEM((hist0 + rows, d_rnn), F32), pltpu.VMEM((stride, d_rnn), F32)],
        compiler_params=_params("arbitrary", "arbitrary"),
        name="recurrent_block",
    )(x3, mod, mod, mod, hist, h0, g.reshape(1, d), w_in, conv_w, conv_b.reshape(1, d_rnn),
      w_r, b_r.reshape(1, d_rnn), w_i, b_i.reshape(1, d_rnn), lam.reshape(1, d_rnn), w_out)


def _tree(op, xs):
    xs = list(xs)
    while len(xs) > 1:
        nxt = [op(xs[i], xs[i + 1]) for i in range(0, len(xs) - 1, 2)]
        if len(xs) % 2:
            nxt.append(xs[-1])
        xs = nxt
    return xs[0]


def _router_kernel(*refs, pre):
    if pre:
        (x_ref, o_ref, wo_ref, agt_ref, sh_ref, sc_ref, g_ref, wrt_ref, rb_ref, cnt_in_ref,
         xo_ref, hn_ref, e_ref, p_ref, w_ref, cnt_ref, run_sc) = refs
    else:
        (x_ref, sh_ref, sc_ref, g_ref, wrt_ref, rb_ref, cnt_in_ref,
         hn_ref, e_ref, p_ref, w_ref, cnt_ref, run_sc) = refs
    i = pl.program_id(0)
    per = N_EXPERTS // N_EXPERT_GROUPS

    @pl.when(i == 0)
    def _():
        run_sc[...] = cnt_in_ref[...]

    x = x_ref[...]
    if pre:
        x = x + agt_ref[0] * _dot(o_ref[...], wo_ref[...])
        xo_ref[...] = x
    hn = _ada(x, g_ref[...], sh_ref[0], sc_ref[0])
    hn_ref[...] = hn
    tb = hn.shape[0]
    logits = _dot_nt(wrt_ref[...], hn, precision=lax.Precision.HIGHEST)
    scores = _sigmoid(logits)
    sel = scores + rb_ref[...][:, 0:1]
    sel_j = [sel[j * SUBLANES:(j + 1) * SUBLANES, :] for j in range(per)]
    sc_j = [scores[j * SUBLANES:(j + 1) * SUBLANES, :] for j in range(per)]
    m1 = _tree(jnp.maximum, sel_j)
    j1 = _tree(jnp.minimum, [jnp.where(sel_j[j] == m1, j, per) for j in range(per)])
    m2 = _tree(jnp.maximum, [jnp.where(j1 == j, NEG_INF, sel_j[j]) for j in range(per)])
    gs = m1 + m2
    g_iota = lax.broadcasted_iota(I32, gs.shape, 0)
    rank = jnp.zeros(gs.shape, I32)
    for dlt in range(1, N_EXPERT_GROUPS):
        other = pltpu.roll(gs, dlt, 0)
        tie = jnp.where(g_iota >= dlt, 1, 0)
        rank = rank + jnp.where(other > gs, 1, jnp.where(other == gs, tie, 0))
    gmask = rank < TOPK_GROUPS
    masked = [jnp.where(gmask, s, NEG_INF) for s in sel_j]
    eid = [g_iota * per + j for j in range(per)]
    chosen = [jnp.zeros(gs.shape, F32) for _ in range(per)]
    e_rows, s_rows = [], []
    for k in range(TOP_K):
        m = jnp.max(_tree(jnp.maximum, masked), axis=0, keepdims=True)
        cand = _tree(jnp.minimum, [jnp.where(masked[j] == m, eid[j], N_EXPERTS) for j in range(per)])
        ek = jnp.min(cand, axis=0, keepdims=True)
        sk = jnp.zeros(gs.shape, F32)
        for j in range(per):
            oh = eid[j] == ek
            masked[j] = jnp.where(oh, NEG_INF, masked[j])
            chosen[j] = jnp.where(oh, 1.0, chosen[j])
            sk = sk + jnp.where(oh, sc_j[j], 0.0)
        e_rows.append(ek)
        s_rows.append(jnp.sum(sk, axis=0, keepdims=True))
    denom = _tree(lambda p, q: p + q, s_rows)
    sel_mask = jnp.concatenate(chosen, axis=0)
    upper = (lax.broadcasted_iota(I32, (tb, tb), 0) < lax.broadcasted_iota(I32, (tb, tb), 1))
    prefix = _dot(sel_mask.astype(BF16), jnp.where(upper, 1.0, 0.0).astype(BF16)) + run_sc[...][:, 0:1]
    pre_j = [prefix[j * SUBLANES:(j + 1) * SUBLANES, :] for j in range(per)]
    for k in range(TOP_K):
        pk = jnp.zeros(gs.shape, F32)
        for j in range(per):
            pk = pk + jnp.where(eid[j] == e_rows[k], pre_j[j], 0.0)
        e_ref[pl.ds(k, 1), :] = e_rows[k]
        p_ref[pl.ds(k, 1), :] = jnp.sum(pk, axis=0, keepdims=True).astype(I32)
        w_ref[pl.ds(k, 1), :] = s_rows[k] / denom * ROUTED_SCALE
    run = run_sc[...] + jnp.sum(sel_mask, axis=1, keepdims=True)
    run_sc[...] = run
    cnt_ref[...] = run


def _router(grp, x, mod, g, w_rt, rbias, cnt_in, pre=None):
    d = x.shape[-1]
    n, tb = grp.n_rows, grp.rows
    ins = [x]
    in_specs = [grp.row_spec(d)]
    if pre is not None:
        o, w_o, amod = pre
        ins += [o, w_o, amod]
        in_specs += [grp.row_spec(o.shape[-1]), _const_spec(w_o.shape), grp.mod_spec(2, d)]
    ins += [mod, mod, g.reshape(1, d), w_rt, rbias, cnt_in]
    in_specs += [grp.mod_spec(0, d), grp.mod_spec(1, d), _const_spec((1, d)), _const_spec(w_rt.shape),
                 _const_spec(rbias.shape), _const_spec(cnt_in.shape)]
    tok_spec = pl.BlockSpec((TOP_K, tb), lambda i: (0, i))
    out_specs = [grp.row_spec(d), tok_spec, tok_spec, tok_spec, _const_spec(cnt_in.shape)]
    out_shape = [jax.ShapeDtypeStruct((n, d), F32), jax.ShapeDtypeStruct((TOP_K, n), I32),
                 jax.ShapeDtypeStruct((TOP_K, n), I32), jax.ShapeDtypeStruct((TOP_K, n), F32),
                 jax.ShapeDtypeStruct(cnt_in.shape, F32)]
    if pre is not None:
        out_specs = [grp.row_spec(d)] + out_specs
        out_shape = [jax.ShapeDtypeStruct((n, d), F32)] + out_shape
    return pl.pallas_call(
        functools.partial(_router_kernel, pre=pre is not None),
        grid=(grp.n_blocks,),
        in_specs=in_specs, out_specs=out_specs, out_shape=out_shape,
        scratch_shapes=[pltpu.VMEM(cnt_in.shape, F32)],
        compiler_params=_params("arbitrary"),
        name="router",
    )(*ins)


def _dispatch_kernel(dest_ref, hn_ref, xs_in_ref, xs_ref, sem, *, rows):
    del xs_in_ref

    def row_copy(r, k):
        d = dest_ref[r * TOP_K + k]
        return pltpu.make_async_copy(hn_ref.at[pl.ds(r, 1)], xs_ref.at[pl.ds(d, 1)], sem)

    def issue(r, c):
        for k in range(TOP_K):
            row_copy(r, k).start()
        return c

    def drain(r, c):
        for k in range(TOP_K):
            row_copy(r, k).wait()
        return c

    lax.fori_loop(0, rows, issue, 0)
    lax.fori_loop(0, rows, drain, 0)


def _dispatch(dest, hn, xs, rows):
    n, d = hn.shape
    return pl.pallas_call(
        functools.partial(_dispatch_kernel, rows=rows),
        grid=(n // rows,),
        in_specs=[pl.BlockSpec((rows * TOP_K,), lambda i: (i,), memory_space=pltpu.SMEM),
                  pl.BlockSpec((rows, d), lambda i: (i, 0)),
                  pl.BlockSpec(memory_space=pl.ANY)],
        out_specs=pl.BlockSpec(memory_space=pl.ANY),
        out_shape=jax.ShapeDtypeStruct(xs.shape, xs.dtype),
        scratch_shapes=[pltpu.SemaphoreType.DMA(())],
        input_output_aliases={2: 0},
        compiler_params=_params("arbitrary"),
        name="dispatch",
    )(dest, hn, xs)


def _experts_kernel(blk_ref, exp_ref, lo_ref, hi_ref, first_ref, newe_ref,
                    xs_ref, wgu_ref, wd_ref, y_ref, wgu_sc, wd_sc):
    del blk_ref, exp_ref
    s = pl.program_id(0)
    lo = lo_ref[s]
    hi = hi_ref[s]

    @pl.when(newe_ref[s] == 1)
    def _():
        wgu_sc[...] = wgu_ref[0].astype(BF16)
        wd_sc[...] = wd_ref[0].astype(BF16)

    @pl.when(first_ref[s] == 1)
    def _():
        y_ref[...] = jnp.zeros(y_ref.shape, F32)

    @pl.when(hi > lo)
    def _():
        gu = _dot(xs_ref[...].astype(BF16), wgu_sc[...])
        half = gu.shape[-1] // 2
        act = _silu(gu[:, :half]) * gu[:, half:]
        row = lax.broadcasted_iota(I32, act.shape, 0)
        act = jnp.where((row >= lo) & (row < hi), act, 0.0)
        y_ref[...] += _dot(act.astype(BF16), wd_sc[...])


def _experts(steps, xs, w_gu, w_down):
    m, d = xs.shape
    rb = EXPERT_ROWS
    n_steps = steps[0].shape[0]
    de2 = w_gu.shape[-1]
    grid_spec = pltpu.PrefetchScalarGridSpec(
        num_scalar_prefetch=6,
        grid=(n_steps,),
        in_specs=[pl.BlockSpec((rb, d), lambda s, blk, ex, *_: (blk[s], 0)),
                  pl.BlockSpec((1, d, de2), lambda s, blk, ex, *_: (ex[s], 0, 0)),
                  pl.BlockSpec((1, de2 // 2, d), lambda s, blk, ex, *_: (ex[s], 0, 0))],
        out_specs=pl.BlockSpec((rb, d), lambda s, blk, ex, *_: (blk[s], 0)),
        scratch_shapes=[pltpu.VMEM((d, de2), BF16), pltpu.VMEM((de2 // 2, d), BF16)],
    )
    return pl.pallas_call(
        _experts_kernel,
        grid_spec=grid_spec,
        out_shape=jax.ShapeDtypeStruct((m, d), F32),
        compiler_params=_params("arbitrary"),
        name="routed_experts",
    )(*steps, xs, w_gu, w_down)


def _expert_steps(counts, m):
    rb = EXPERT_ROWS
    nb = m // rb
    start = jnp.cumsum(counts) - counts
    end = start + counts
    cuts = jnp.sort(jnp.concatenate([jnp.arange(nb, dtype=I32) * rb, start.astype(I32)]))
    nxt = jnp.concatenate([cuts[1:], jnp.array([m], I32)])
    blk = jnp.minimum(cuts // rb, nb - 1)
    ex = jnp.minimum(jnp.searchsorted(end, cuts, side="right"), N_EXPERTS - 1).astype(I32)
    lo = cuts - blk * rb
    hi = nxt - blk * rb
    first = jnp.concatenate([jnp.ones((1,), I32), (blk[1:] != blk[:-1]).astype(I32)])
    newe = jnp.concatenate([jnp.ones((1,), I32), (ex[1:] != ex[:-1]).astype(I32)])
    return blk, ex, lo, hi, first, newe


def _combine_kernel(dest_ref, ew_ref, x_ref, hn_ref, gt_ref, wsgu_ref, wsd_ref, fg_ref, yb_ref,
                    o_ref, buf, sem, *, rows, final):
    def row_copy(r, k):
        d = dest_ref[r * TOP_K + k]
        return pltpu.make_async_copy(yb_ref.at[pl.ds(d, 1)], buf.at[k, pl.ds(r, 1)], sem)

    def issue(r, c):
        for k in range(TOP_K):
            row_copy(r, k).start()
        return c

    def drain(r, c):
        for k in range(TOP_K):
            row_copy(r, k).wait()
        return c

    lax.fori_loop(0, rows, issue, 0)
    shared = _swiglu(hn_ref[...].astype(BF16), wsgu_ref[...], wsd_ref[...])
    lax.fori_loop(0, rows, drain, 0)
    ew = ew_ref[...]
    routed = ew[:, 0:1] * buf[0]
    for k in range(1, TOP_K):
        routed = routed + ew[:, k:k + 1] * buf[k]
    x = x_ref[...] + gt_ref[0] * (routed + shared)
    if final:
        x = _rms(x, fg_ref[...])
    o_ref[...] = x


def _combine(grp, dest, ew, x, hn, mod, ws_gu, ws_down, fg, yb, final):
    d = x.shape[-1]
    rows = grp.rows
    return pl.pallas_call(
        functools.partial(_combine_kernel, rows=rows, final=final),
        grid=(grp.n_blocks,),
        in_specs=[pl.BlockSpec((rows * TOP_K,), lambda i: (i,), memory_space=pltpu.SMEM),
                  pl.BlockSpec((rows, TOP_K), lambda i: (i, 0)),
                  grp.row_spec(d), grp.row_spec(d), grp.mod_spec(2, d),
                  _const_spec(ws_gu.shape), _const_spec(ws_down.shape), _const_spec((1, d)),
                  pl.BlockSpec(memory_space=pl.ANY)],
        out_specs=grp.row_spec(d),
        out_shape=jax.ShapeDtypeStruct(x.shape, F32),
        scratch_shapes=[pltpu.VMEM((TOP_K, rows, d), F32), pltpu.SemaphoreType.DMA(())],
        compiler_params=_params("arbitrary"),
        name="combine",
    )(dest, ew, x, hn, mod, ws_gu, ws_down, fg.reshape(1, d), yb)


def _proj_kernel(x_ref, ksh_ref, ksc_ref, kg_ref, wkv_ref, bsh_ref, bsc_ref, bg_ref, wq_ref,
                 k_ref, v_ref, kb_ref, vb_ref, q_ref, *, q_scale):
    x = x_ref[...]
    kvn = _ada(x, kg_ref[...], ksh_ref[0], ksc_ref[0])
    kv = _dot(kvn.astype(BF16), wkv_ref[...])
    half = kv.shape[-1] // 2
    k = kv[:, :half]
    v = kv[:, half:]
    k_ref[...] = k
    v_ref[...] = v
    kb_ref[...] = k.astype(BF16)
    vb_ref[...] = v.astype(BF16)
    hn = _ada(x, bg_ref[...], bsh_ref[0], bsc_ref[0])
    q_ref[...] = (_dot(hn.astype(BF16), wq_ref[...]) * q_scale).astype(BF16)


def _proj(grp, x, kv_mod, kv_g, w_kv, b_mod, b_g, w_q, q_scale):
    d = x.shape[-1]
    n = grp.n_rows
    hd = w_q.shape[-1]
    return pl.pallas_call(
        functools.partial(_proj_kernel, q_scale=q_scale),
        grid=(grp.n_blocks,),
        in_specs=[grp.row_spec(d), grp.mod_spec(0, d), grp.mod_spec(1, d), _const_spec((1, d)),
                  _const_spec(w_kv.shape), grp.mod_spec(0, d), grp.mod_spec(1, d), _const_spec((1, d)),
                  _const_spec(w_q.shape)],
        out_specs=[grp.row_spec(hd)] * 5,
        out_shape=[jax.ShapeDtypeStruct((n, hd), F32), jax.ShapeDtypeStruct((n, hd), F32),
                   jax.ShapeDtypeStruct((n, hd), BF16), jax.ShapeDtypeStruct((n, hd), BF16),
                   jax.ShapeDtypeStruct((n, hd), BF16)],
        compiler_params=_params("arbitrary"),
        name="kv_q_projection",
    )(x, kv_mod, kv_mod, kv_g.reshape(1, d), w_kv, b_mod, b_mod, b_g.reshape(1, d), w_q)


def _later_key_matrix(tk):
    return jnp.where(lax.broadcasted_iota(I32, (tk, tk), 0) > lax.broadcasted_iota(I32, (tk, tk), 1),
                     1.0, 0.0).astype(BF16)


def _sb_block(z, tri, carry, mask):
    e = jnp.exp(-jnp.abs(z))
    soft = jnp.log1p(e)
    log_1m = jnp.minimum(-z, 0.0) - soft
    log_b = log_1m + z
    if mask is not None:
        log_1m = jnp.where(mask, log_1m, 0.0)
    hi = log_1m.astype(BF16)
    lo = (log_1m - hi.astype(F32)).astype(BF16)
    tail = _dot(hi, tri) + _dot(lo, tri)
    w = jnp.exp(log_b + tail + carry)
    if mask is not None:
        w = jnp.where(mask, w, 0.0)
    return w, carry + jnp.sum(log_1m, axis=1, keepdims=True)


def _attn_prompt_kernel(bias_ref, q_ref, k_ref, v_ref, o_ref, *, tq, tk):
    hp = pl.program_id(1)
    qi = pl.program_id(2)
    q = q_ref[0]
    lane = lax.broadcasted_iota(I32, q.shape, 1)
    half = q.shape[-1] // 2
    tri = _later_key_matrix(tk)
    q_pos = qi * tq + lax.broadcasted_iota(I32, (tq, tk), 0)
    k_off = lax.broadcasted_iota(I32, (tq, tk), 1)
    n_diag = tq // tk
    outs = []
    for hh in range(2):
        qh = jnp.where((lane >= half) if hh else (lane < half), q, jnp.zeros_like(q))
        bias = bias_ref[hp * 2 + hh]

        def block(j, carry, acc, masked):
            k0 = pl.multiple_of(j * tk, tk)
            kb = k_ref[0, pl.ds(k0, tk), :]
            vb = v_ref[0, pl.ds(k0, tk), :]
            z = _dot_nt(qh, kb) + bias
            mask = (k0 + k_off < q_pos) if masked else None
            w, carry = _sb_block(z, tri, carry, mask)
            return carry, acc + _dot(w.astype(BF16), vb)

        carry = jnp.zeros((tq, 1), F32)
        acc = jnp.zeros(q.shape, F32)
        for dj in reversed(range(n_diag)):
            carry, acc = block(qi * n_diag + dj, carry, acc, True)
        n_full = qi * n_diag

        def body(it, c):
            return block(n_full - 1 - it, c[0], c[1], False)

        carry, acc = lax.fori_loop(0, n_full, body, (carry, acc))
        outs.append(acc)
    o_ref[0] = jnp.where(lane < half, outs[0], outs[1]).astype(o_ref.dtype)


def _attn_prompt(q, k, v, bias):
    b, t, hd = q.shape
    tq, tk = ATTN_Q_ROWS, ATTN_K_ROWS
    pair = 2 * (hd // N_HEADS)
    return pl.pallas_call(
        functools.partial(_attn_prompt_kernel, tq=tq, tk=tk),
        grid=(b, hd // pair, t // tq),
        in_specs=[pl.BlockSpec(memory_space=pltpu.SMEM),
                  pl.BlockSpec((1, tq, pair), lambda bi, hp, qi: (bi, qi, hp)),
                  pl.BlockSpec((1, t, pair), lambda bi, hp, qi: (bi, 0, hp)),
                  pl.BlockSpec((1, t, pair), lambda bi, hp, qi: (bi, 0, hp))],
        out_specs=pl.BlockSpec((1, tq, pair), lambda bi, hp, qi: (bi, qi, hp)),
        out_shape=jax.ShapeDtypeStruct((b, t, hd), BF16),
        compiler_params=_params("arbitrary", "arbitrary", "arbitrary"),
        name="sb_attention_prompt",
    )(bias, q, k, v)


def _attn_sample_kernel(pt_ref, bias_ref, q_ref, kn_ref, vn_ref, kc_ref, vc_ref, o_ref, acc_sc, carry_sc,
                        *, page, q_pad):
    del pt_ref
    j = pl.program_id(1)
    heads = q_ref.shape[1]
    rows = heads * q_pad
    tri = _later_key_matrix(page)

    @pl.when(j == 0)
    def _():
        acc_sc[...] = jnp.zeros(acc_sc.shape, F32)
        carry_sc[...] = jnp.zeros(carry_sc.shape, F32)

    def process(k_ref, v_ref, masked):
        k3 = jnp.stack([k_ref[0, pl.ds(h, page, stride=heads), :] for h in range(heads)]).astype(BF16)
        v3 = jnp.stack([v_ref[0, pl.ds(h, page, stride=heads), :] for h in range(heads)]).astype(BF16)
        z3 = lax.dot_general(q_ref[0], k3, (((2,), (2,)), ((0,), (0,))), preferred_element_type=F32)
        z = z3.reshape(rows, page) + bias_ref[...]
        mask = None
        if masked:
            qidx = lax.broadcasted_iota(I32, (rows, page), 0) % q_pad
            mask = lax.broadcasted_iota(I32, (rows, page), 1) < qidx
        w, carry = _sb_block(z, tri, carry_sc[...][:, 0:1], mask)
        carry_sc[...] = jnp.broadcast_to(carry, carry_sc.shape)
        w3 = w.reshape(heads, q_pad, page).astype(BF16)
        acc_sc[...] += lax.dot_general(w3, v3, (((2,), (1,)), ((0,), (0,))), preferred_element_type=F32)

    @pl.when(j == 0)
    def _():
        process(kn_ref, vn_ref, True)

    @pl.when(j > 0)
    def _():
        process(kc_ref, vc_ref, False)

    @pl.when(j == pl.num_programs(1) - 1)
    def _():
        o_ref[0] = acc_sc[...]


def _attn_sample(page_table, bias_rows, q3, k_new, v_new, cache_k, cache_v, page):
    bs, heads, q_pad, dh = q3.shape
    n_pages = page_table.shape[1]
    rows = heads * q_pad
    prow = page * heads

    def cache_map(b, j, pt):
        return (pt[b, n_pages - jnp.maximum(j, 1)], 0, 0)

    grid_spec = pltpu.PrefetchScalarGridSpec(
        num_scalar_prefetch=1,
        grid=(bs, n_pages + 1),
        in_specs=[pl.BlockSpec((rows, page), lambda b, j, pt: (0, 0)),
                  pl.BlockSpec((1, heads, q_pad, dh), lambda b, j, pt: (b, 0, 0, 0)),
                  pl.BlockSpec((1, prow, dh), lambda b, j, pt: (b, 0, 0)),
                  pl.BlockSpec((1, prow, dh), lambda b, j, pt: (b, 0, 0)),
                  pl.BlockSpec((1, prow, dh), cache_map),
                  pl.BlockSpec((1, prow, dh), cache_map)],
        out_specs=pl.BlockSpec((1, heads, q_pad, dh), lambda b, j, pt: (b, 0, 0, 0)),
        scratch_shapes=[pltpu.VMEM((heads, q_pad, dh), F32), pltpu.VMEM((rows, LANES), F32)],
    )
    return pl.pallas_call(
        functools.partial(_attn_sample_kernel, page=page, q_pad=q_pad),
        grid_spec=grid_spec,
        out_shape=jax.ShapeDtypeStruct((bs, heads, q_pad, dh), F32),
        compiler_params=_params("arbitrary", "arbitrary"),
        name="sb_attention_sample",
    )(page_table, bias_rows, q3, k_new, v_new, cache_k, cache_v)


def _permute_experts(a):
    per = N_EXPERTS // N_EXPERT_GROUPS
    return a.reshape(N_EXPERT_GROUPS, per, *a.shape[1:]).swapaxes(0, 1).reshape(a.shape)


def _moe_layer(groups, cgroups, xs_rows, mods, pres, g, w_router, r_bias, w_gu, w_down, ws_gu, ws_down,
               fg, final):
    d = xs_rows[0].shape[-1]
    w_rt = _permute_experts(w_router.T)
    rbias = jnp.broadcast_to(_permute_experts(r_bias.reshape(N_EXPERTS, 1)), (N_EXPERTS, LANES))
    cnt = jnp.zeros((N_EXPERTS, LANES), F32)
    routed = []
    x_new = []
    for grp, x, mod, pre in zip(groups, xs_rows, mods, pres):
        outs = _router(grp, x, mod, g, w_rt, rbias, cnt, pre)
        if pre is not None:
            x, outs = outs[0], outs[1:]
        hn, e_t, p_t, w_t, cnt = outs
        routed.append((hn, e_t, p_t, w_t))
        x_new.append(x)
    per = N_EXPERTS // N_EXPERT_GROUPS
    counts = cnt[:, 0].astype(I32).reshape(per, N_EXPERT_GROUPS).T.reshape(N_EXPERTS)
    start = jnp.cumsum(counts) - counts
    m = sum(grp.n_rows for grp in groups) * TOP_K
    steps = _expert_steps(counts, m)
    xs = jnp.zeros((m, d), F32)
    dests = []
    for grp, (hn, e_t, p_t, w_t) in zip(groups, routed):
        dest = (start[e_t] + p_t).T.reshape(-1)
        dests.append(dest)
        xs = _dispatch(dest, hn, xs, min(DISPATCH_ROWS, grp.n_rows))
    yb = _experts(steps, xs, w_gu, w_down)
    outs = []
    for cgrp, x, mod, dest, (hn, e_t, p_t, w_t) in zip(cgroups, x_new, mods, dests, routed):
        outs.append(_combine(cgrp, dest, w_t.T, x, hn, mod, ws_gu, ws_down, fg, yb, final))
    return outs


def kernel(x_prompt, x_sample, c_prompt, c_sample, state_conv, state_h, cache_k, cache_v, page_table, a_norm_g, a_mod_w, a_mod_b, a_w_in, a_conv_w, a_conv_b, a_w_gate_r, a_b_gate_r, a_w_gate_i, a_b_gate_i, a_lambda, a_w_out, kv_norm_g, kv_mod_w, kv_mod_b, w_kv, b_norm_g, b_mod_w, b_mod_b, b_w_q, b_sb_bias, b_w_o, m_norm_g, m_mod_w, m_mod_b, m_w_router, m_router_bias, m_w_gate_up, m_w_down, m_ws_gate_up, m_ws_down, final_norm_g):
    bp, t, d = x_prompt.shape
    bs, ts, _ = x_sample.shape
    depth = m_norm_g.shape[0]
    n_a = a_norm_g.shape[0]
    d_rnn = a_w_out.shape[1]
    heads = N_HEADS
    dh = b_w_q.shape[-1] // heads
    page = cache_k.shape[1]
    np_rows = bp * t
    ns_rows = bs * ts
    assert n_a == 1 and depth == 2, "layer pattern of this step: one self-decoder, one cross-decoder layer"

    n_c = bp + bs
    c_rows = -(-n_c // SUBLANES) * SUBLANES
    c_all = jnp.zeros((c_rows, d), F32).at[:bp].set(c_prompt).at[bp:n_c].set(c_sample)

    def mods(w, b):
        m = _modulation(c_all, w, b)
        mp = m[:bp].reshape(bp, 1, -1)
        ms = jnp.tile(m[bp:n_c], (ts, 1)).reshape(1, ns_rows, -1)
        return mp, ms

    a_mod = mods(a_mod_w[0], a_mod_b[0])
    kv_mod = mods(kv_mod_w, kv_mod_b)
    b_mod = mods(b_mod_w[0], b_mod_b[0])
    m_mod = [mods(m_mod_w[l], m_mod_b[l]) for l in range(depth)]

    bf = lambda w: w.astype(BF16)

    rec_w = (a_norm_g[0], bf(a_w_in[0]), a_conv_w[0], a_conv_b[0], bf(a_w_gate_r[0]), a_b_gate_r[0],
             bf(a_w_gate_i[0]), a_b_gate_i[0], a_lambda[0], bf(a_w_out[0]))
    hist_p = jnp.zeros((bp, CONV_W - 1, d_rnn), F32)
    h0_p = jnp.zeros((bp, 1, d_rnn), F32)
    x1_p, conv_p, h_p = _recurrent(x_prompt, a_mod[0], hist_p, h0_p, *rec_w,
                                   stride=1, steps=REC_ROWS, mod_rows=1)
    xs_tm = x_sample.swapaxes(0, 1).reshape(1, ns_rows, d)
    hist_s = state_conv[0].swapaxes(0, 1).reshape(1, (CONV_W - 1) * bs, d_rnn)
    h0_s = state_h[0].reshape(1, bs, d_rnn)
    x1_s, conv_s, h_s = _recurrent(xs_tm, a_mod[1], hist_s, h0_s, *rec_w,
                                   stride=bs, steps=ts, mod_rows=ns_rows)

    grp_p = _Group(np_rows, ROUTER_ROWS, t // ROUTER_ROWS, 1)
    grp_s = _Group(ns_rows, ns_rows, 1, ns_rows)
    groups = [grp_p, grp_s]
    cgroups = [_Group(np_rows, COMBINE_ROWS, t // COMBINE_ROWS, 1), grp_s]

    x2_p, x2_s = _moe_layer(groups, cgroups, [x1_p.reshape(np_rows, d), x1_s.reshape(ns_rows, d)],
                            [m_mod[0][0], m_mod[0][1]], [None, None], m_norm_g[0], m_w_router[0],
                            m_router_bias[0], m_w_gate_up[0], m_w_down[0], bf(m_ws_gate_up[0]),
                            bf(m_ws_down[0]), final_norm_g, False)

    q_scale = float(dh) ** -0.5
    w_kv_b, w_q_b = bf(w_kv), bf(b_w_q[0])
    pgrp_p = _Group(np_rows, PROJ_ROWS, t // PROJ_ROWS, 1)
    k_p, v_p, kb_p, vb_p, q_p = _proj(pgrp_p, x2_p, kv_mod[0], kv_norm_g, w_kv_b, b_mod[0], b_norm_g[0],
                                      w_q_b, q_scale)
    k_s, v_s, _, _, q_s = _proj(grp_s, x2_s, kv_mod[1], kv_norm_g, w_kv_b, b_mod[1], b_norm_g[0],
                                w_q_b, q_scale)

    hd = heads * dh
    o_p = _attn_prompt(q_p.reshape(bp, t, hd), kb_p.reshape(bp, t, hd), vb_p.reshape(bp, t, hd),
                       b_sb_bias[0]).reshape(np_rows, hd)

    q_pad = SUBLANES
    q3 = jnp.zeros((bs, heads, q_pad, dh), BF16).at[:, :, :ts].set(
        q_s.reshape(ts, bs, heads, dh).transpose(1, 2, 0, 3))

    def new_keys(a):
        a = a.reshape(ts, bs, heads, dh).transpose(1, 0, 2, 3)
        return jnp.zeros((bs, page, heads, dh), F32).at[:, :ts].set(a).reshape(bs, page * heads, dh)

    bias_rows = jnp.broadcast_to(jnp.repeat(b_sb_bias[0], q_pad)[:, None], (heads * q_pad, page))
    o3 = _attn_sample(page_table, bias_rows, q3, new_keys(k_s), new_keys(v_s),
                      cache_k.reshape(cache_k.shape[0], page * heads, dh),
                      cache_v.reshape(cache_v.shape[0], page * heads, dh), page)
    o_s = o3[:, :, :ts].transpose(2, 0, 1, 3).reshape(ns_rows, hd).astype(BF16)

    w_o_b = bf(b_w_o[0])
    y_p, y_s = _moe_layer(groups, cgroups, [x2_p, x2_s], [m_mod[1][0], m_mod[1][1]],
                          [(o_p, w_o_b, b_mod[0]), (o_s, w_o_b, b_mod[1])], m_norm_g[1], m_w_router[1],
                          m_router_bias[1], m_w_gate_up[1], m_w_down[1], bf(m_ws_gate_up[1]),
                          bf(m_ws_down[1]), final_norm_g, True)

    def from_tm(a, *tail):
        return a.reshape(ts, bs, *tail).swapaxes(0, 1)

    return (y_p.reshape(bp, t, d),
            from_tm(y_s, d),
            conv_p[None],
            h_p.reshape(1, bp, d_rnn),
            k_p.reshape(bp, t, heads, dh),
            v_p.reshape(bp, t, heads, dh),
            conv_s.reshape(CONV_W - 1, bs, d_rnn).swapaxes(0, 1)[None],
            h_s.reshape(1, bs, d_rnn),
            from_tm(k_s, heads, dh),
            from_tm(v_s, heads, dh))
```

```python
import functools

import jax
import jax.numpy as jnp
from jax import lax
from jax.experimental import pallas as pl
from jax.experimental.pallas import tpu as pltpu

F32 = jnp.float32
BF16 = jnp.bfloat16
I32 = jnp.int32

CONV_W = 4
LRU_BLOCKS = 4
RGLRU_C = 8.0
N_HEADS = 16
N_EXPERTS = 64
TOP_K = 8
N_EXPERT_GROUPS = 8
TOPK_GROUPS = 4
ROUTED_SCALE = 2.5
EPS = 1e-6
NEG_INF = float("-inf")

LANES = 128
SUBLANES = 8
VMEM_LIMIT_BYTES = 56 * 1024 * 1024

REC_ROWS = 256
ROUTER_ROWS = 512
PROJ_ROWS = 512
DISPATCH_ROWS = 256
COMBINE_ROWS = 128
EXPERT_ROWS = 256
ATTN_Q_ROWS = 256
ATTN_K_ROWS = 128
SAMPLE_PAGES_PER_STEP = 4


def _params(*sem):
    return pltpu.CompilerParams(dimension_semantics=sem, vmem_limit_bytes=VMEM_LIMIT_BYTES)


def _dot(a, b):
    return jnp.dot(a, b, preferred_element_type=F32)


def _dot_nt(a, b, precision=None):
    return lax.dot_general(a, b, (((1,), (1,)), ((), ())), precision=precision,
                           preferred_element_type=F32)


def _sigmoid(x):
    return 1.0 / (1.0 + jnp.exp(-x))


def _silu(x):
    return x * _sigmoid(x)


def _gelu_tanh(x):
    return 0.5 * x * (1.0 + jnp.tanh(0.7978845608028654 * (x + 0.044715 * (x * x * x))))


def _rms(x, g):
    ms = jnp.mean(x * x, axis=-1, keepdims=True)
    return x * lax.rsqrt(ms + EPS) * g


def _ada(x, g, shift, scale):
    return _rms(x, g) * (1.0 + scale) + shift


def _swiglu(xb, w_gu, w_down):
    gu = _dot(xb, w_gu)
    half = gu.shape[-1] // 2
    act = _silu(gu[:, :half]) * gu[:, half:]
    return _dot(act.astype(BF16), w_down)


def _mod_kernel(c_ref, w_ref, b_ref, o_ref):
    c = c_ref[...]
    o_ref[...] = _dot(_silu(c).astype(BF16), w_ref[...].astype(BF16)) + b_ref[...]


def _modulation(c_all, w, b):
    rows, d = c_all.shape
    n_out = w.shape[1]
    tn = 1024
    return pl.pallas_call(
        _mod_kernel,
        grid=(n_out // tn,),
        in_specs=[pl.BlockSpec((rows, d), lambda j: (0, 0)),
                  pl.BlockSpec((d, tn), lambda j: (0, j)),
                  pl.BlockSpec((1, tn), lambda j: (0, j))],
        out_specs=pl.BlockSpec((rows, tn), lambda j: (0, j)),
        out_shape=jax.ShapeDtypeStruct((rows, n_out), F32),
        compiler_params=_params("arbitrary"),
        name="modulation",
    )(c_all, w, b.reshape(1, n_out))


class _Group:
    def __init__(self, n_rows, rows, blocks_per_seq, mod_rows):
        self.n_rows = n_rows
        self.rows = rows
        self.blocks_per_seq = blocks_per_seq
        self.mod_rows = mod_rows
        self.n_blocks = n_rows // rows

    def mod_spec(self, col, d):
        bps = self.blocks_per_seq
        return pl.BlockSpec((1, self.mod_rows, d), lambda i: (i // bps, 0, col))

    def row_spec(self, width):
        return pl.BlockSpec((self.rows, width), lambda i: (i, 0))


def _const_spec(shape):
    nd = len(shape)
    return pl.BlockSpec(shape, lambda *_: (0,) * nd)


def _scan_rows(a, u, stride, steps):
    row = lax.broadcasted_iota(I32, a.shape, 0)
    s = 1
    while s < steps:
        sh = s * stride
        a_p = pltpu.roll(a, sh, 0)
        u_p = pltpu.roll(u, sh, 0)
        m = row >= sh
        u = jnp.where(m, a * u_p + u, u)
        a = jnp.where(m, a * a_p, a)
        s *= 2
    return a, u


def _rec_kernel(x_ref, sh_ref, sc_ref, gt_ref, hist_ref, h0_ref, g_ref, w_in_ref, cw_ref, cb_ref,
                wr_ref, br_ref, wi_ref, bi_ref, lam_ref, w_out_ref,
                xo_ref, conv_ref, hl_ref, xpad_sc, h_sc, *, stride, steps, hist0):
    t = pl.program_id(1)
    rows = stride * steps
    hrows = (CONV_W - 1) * stride
    d_rnn = h_sc.shape[-1]

    @pl.when(t == 0)
    def _():
        xpad_sc[pl.ds(hist0 - hrows, hrows), :] = hist_ref[0]
        h_sc[...] = h0_ref[0]

    x = x_ref[0]
    hn = _ada(x, g_ref[...], sh_ref[0], sc_ref[0])
    proj = _dot(hn.astype(BF16), w_in_ref[...])
    gate_br = proj[:, :d_rnn]
    xb = proj[:, d_rnn:]
    xpad_sc[pl.ds(hist0, rows), :] = xb
    xc = cb_ref[...] + cw_ref[0:1, :] * xpad_sc[pl.ds(hist0 - hrows, rows), :]
    for j in range(1, CONV_W):
        xc = xc + cw_ref[j:j + 1, :] * xpad_sc[pl.ds(hist0 - hrows + j * stride, rows), :]
    new_hist = xpad_sc[pl.ds(hist0 + rows - hrows, hrows), :]
    xpad_sc[pl.ds(hist0 - hrows, hrows), :] = new_hist
    conv_ref[0] = new_hist

    bw = d_rnn // LRU_BLOCKS
    r_parts, i_parts = [], []
    for n in range(LRU_BLOCKS):
        xblk = xc[:, n * bw:(n + 1) * bw].astype(BF16)
        r_parts.append(_dot(xblk, wr_ref[n]))
        i_parts.append(_dot(xblk, wi_ref[n]))
    r = _sigmoid(jnp.concatenate(r_parts, axis=-1) + br_ref[...])
    ig = _sigmoid(jnp.concatenate(i_parts, axis=-1) + bi_ref[...])
    neg_lam = -lam_ref[...]
    softplus = jnp.maximum(neg_lam, 0.0) + jnp.log1p(jnp.exp(-jnp.abs(neg_lam)))
    log_a = (-RGLRU_C * r) * softplus
    a = jnp.exp(log_a)
    u = jnp.sqrt(-jnp.tanh(log_a) * (a * a + 1.0)) * (ig * xc)
    a_cum, u_cum = _scan_rows(a, u, stride, steps)
    h_prev = h_sc[...]
    if stride == 1:
        h = a_cum * h_prev + u_cum
    else:
        h = a_cum * jnp.concatenate([h_prev] * steps, axis=0) + u_cum
    h_last = h[rows - stride:, :]
    h_sc[...] = h_last
    hl_ref[0] = h_last
    y = _dot((_gelu_tanh(gate_br) * h).astype(BF16), w_out_ref[...])
    xo_ref[0] = x + gt_ref[0] * y


def _recurrent(x3, mod, hist, h0, g, w_in, conv_w, conv_b, w_r, b_r, w_i, b_i, lam, w_out,
               *, stride, steps, mod_rows):
    nb, t_rows, d = x3.shape
    d_rnn = w_out.shape[0]
    rows = stride * steps
    nt = t_rows // rows
    hrows = (CONV_W - 1) * stride
    hist0 = -(-hrows // SUBLANES) * SUBLANES
    assert rows >= hrows and rows % SUBLANES == 0
    x_spec = pl.BlockSpec((1, rows, d), lambda b, t: (b, t, 0))

    def mspec(col):
        if mod_rows == 1:
            return pl.BlockSpec((1, 1, d), lambda b, t: (b, 0, col))
        return pl.BlockSpec((1, rows, d), lambda b, t: (b, 0, col))

    def cspec(shape):
        nd = len(shape)
        return pl.BlockSpec(shape, lambda b, t: (0,) * nd)

    kern = functools.partial(_rec_kernel, stride=stride, steps=steps, hist0=hist0)
    return pl.pallas_call(
        kern,
        grid=(nb, nt),
        in_specs=[x_spec, mspec(0), mspec(1), mspec(2),
                  pl.BlockSpec((1, hrows, d_rnn), lambda b, t: (b, 0, 0)),
                  pl.BlockSpec((1, stride, d_rnn), lambda b, t: (b, 0, 0)),
                  cspec((1, d)), cspec(w_in.shape), cspec(conv_w.shape), cspec((1, d_rnn)),
                  cspec(w_r.shape), cspec((1, d_rnn)), cspec(w_i.shape), cspec((1, d_rnn)),
                  cspec((1, d_rnn)), cspec(w_out.shape)],
        out_specs=[x_spec,
                   pl.BlockSpec((1, hrows, d_rnn), lambda b, t: (b, 0, 0)),
                   pl.BlockSpec((1, stride, d_rnn), lambda b, t: (b, 0, 0))],
        out_shape=[jax.ShapeDtypeStruct(x3.shape, F32),
                   jax.ShapeDtypeStruct((nb, hrows, d_rnn), F32),
                   jax.ShapeDtypeStruct((nb, stride, d_rnn), F32)],
        scratch_shapes=[pltpu.VMEM((hist0 + rows, d_rnn), F32), pltpu.VMEM((stride, d_rnn), F32)],
        compiler_params=_params("arbitrary", "arbitrary"),
        name="recurrent_block",
    )(x3, mod, mod, mod, hist, h0, g.reshape(1, d), w_in, conv_w, conv_b.reshape(1, d_rnn),
      w_r, b_r.reshape(1, d_rnn), w_i, b_i.reshape(1, d_rnn), lam.reshape(1, d_rnn), w_out)


def _tree(op, xs):
    xs = list(xs)
    while len(xs) > 1:
        nxt = [op(xs[i], xs[i + 1]) for i in range(0, len(xs) - 1, 2)]
        if len(xs) % 2:
            nxt.append(xs[-1])
        xs = nxt
    return xs[0]


def _router_kernel(*refs, pre):
    if pre:
        (x_ref, o_ref, wo_ref, agt_ref, sh_ref, sc_ref, g_ref, wrt_ref, rb_ref, cnt_in_ref,
         xo_ref, hn_ref, e_ref, p_ref, w_ref, cnt_ref, run_sc) = refs
    else:
        (x_ref, sh_ref, sc_ref, g_ref, wrt_ref, rb_ref, cnt_in_ref,
         hn_ref, e_ref, p_ref, w_ref, cnt_ref, run_sc) = refs
    i = pl.program_id(0)
    per = N_EXPERTS // N_EXPERT_GROUPS

    @pl.when(i == 0)
    def _():
        run_sc[...] = cnt_in_ref[...]

    x = x_ref[...]
    if pre:
        x = x + agt_ref[0] * _dot(o_ref[...], wo_ref[...])
        xo_ref[...] = x
    hn = _ada(x, g_ref[...], sh_ref[0], sc_ref[0])
    hn_ref[...] = hn
    tb = hn.shape[0]
    logits = _dot_nt(wrt_ref[...], hn, precision=lax.Precision.HIGHEST)
    scores = _sigmoid(logits)
    sel = scores + rb_ref[...][:, 0:1]
    sel_j = [sel[j * SUBLANES:(j + 1) * SUBLANES, :] for j in range(per)]
    sc_j = [scores[j * SUBLANES:(j + 1) * SUBLANES, :] for j in range(per)]
    m1 = _tree(jnp.maximum, sel_j)
    j1 = _tree(jnp.minimum, [jnp.where(sel_j[j] == m1, j, per) for j in range(per)])
    m2 = _tree(jnp.maximum, [jnp.where(j1 == j, NEG_INF, sel_j[j]) for j in range(per)])
    gs = m1 + m2
    g_iota = lax.broadcasted_iota(I32, gs.shape, 0)
    rank = jnp.zeros(gs.shape, I32)
    for dlt in range(1, N_EXPERT_GROUPS):
        other = pltpu.roll(gs, dlt, 0)
        tie = jnp.where(g_iota >= dlt, 1, 0)
        rank = rank + jnp.where(other > gs, 1, jnp.where(other == gs, tie, 0))
    gmask = rank < TOPK_GROUPS
    masked = [jnp.where(gmask, s, NEG_INF) for s in sel_j]
    eid = [g_iota * per + j for j in range(per)]
    chosen = [jnp.zeros(gs.shape, F32) for _ in range(per)]
    e_rows, s_rows = [], []
    for k in range(TOP_K):
        m = jnp.max(_tree(jnp.maximum, masked), axis=0, keepdims=True)
        cand = _tree(jnp.minimum, [jnp.where(masked[j] == m, eid[j], N_EXPERTS) for j in range(per)])
        ek = jnp.min(cand, axis=0, keepdims=True)
        sk = jnp.zeros(gs.shape, F32)
        for j in range(per):
            oh = eid[j] == ek
            masked[j] = jnp.where(oh, NEG_INF, masked[j])
            chosen[j] = jnp.where(oh, 1.0, chosen[j])
            sk = sk + jnp.where(oh, sc_j[j], 0.0)
        e_rows.append(ek)
        s_rows.append(jnp.sum(sk, axis=0, keepdims=True))
    denom = _tree(lambda p, q: p + q, s_rows)
    sel_mask = jnp.concatenate(chosen, axis=0)
    upper = (lax.broadcasted_iota(I32, (tb, tb), 0) < lax.broadcasted_iota(I32, (tb, tb), 1))
    prefix = _dot(sel_mask.astype(BF16), jnp.where(upper, 1.0, 0.0).astype(BF16)) + run_sc[...][:, 0:1]
    pre_j = [prefix[j * SUBLANES:(j + 1) * SUBLANES, :] for j in range(per)]
    for k in range(TOP_K):
        pk = jnp.zeros(gs.shape, F32)
        for j in range(per):
            pk = pk + jnp.where(eid[j] == e_rows[k], pre_j[j], 0.0)
        e_ref[pl.ds(k, 1), :] = e_rows[k]
        p_ref[pl.ds(k, 1), :] = jnp.sum(pk, axis=0, keepdims=True).astype(I32)
        w_ref[pl.ds(k, 1), :] = s_rows[k] / denom * ROUTED_SCALE
    run = run_sc[...] + jnp.sum(sel_mask, axis=1, keepdims=True)
    run_sc[...] = run
    cnt_ref[...] = run


def _router(grp, x, mod, g, w_rt, rbias, cnt_in, pre=None):
    d = x.shape[-1]
    n, tb = grp.n_rows, grp.rows
    ins = [x]
    in_specs = [grp.row_spec(d)]
    if pre is not None:
        o, w_o, amod = pre
        ins += [o, w_o, amod]
        in_specs += [grp.row_spec(o.shape[-1]), _const_spec(w_o.shape), grp.mod_spec(2, d)]
    ins += [mod, mod, g.reshape(1, d), w_rt, rbias, cnt_in]
    in_specs += [grp.mod_spec(0, d), grp.mod_spec(1, d), _const_spec((1, d)), _const_spec(w_rt.shape),
                 _const_spec(rbias.shape), _const_spec(cnt_in.shape)]
    tok_spec = pl.BlockSpec((TOP_K, tb), lambda i: (0, i))
    out_specs = [grp.row_spec(d), tok_spec, tok_spec, tok_spec, _const_spec(cnt_in.shape)]
    out_shape = [jax.ShapeDtypeStruct((n, d), F32), jax.ShapeDtypeStruct((TOP_K, n), I32),
                 jax.ShapeDtypeStruct((TOP_K, n), I32), jax.ShapeDtypeStruct((TOP_K, n), F32),
                 jax.ShapeDtypeStruct(cnt_in.shape, F32)]
    if pre is not None:
        out_specs = [grp.row_spec(d)] + out_specs
        out_shape = [jax.ShapeDtypeStruct((n, d), F32)] + out_shape
    return pl.pallas_call(
        functools.partial(_router_kernel, pre=pre is not None),
        grid=(grp.n_blocks,),
        in_specs=in_specs, out_specs=out_specs, out_shape=out_shape,
        scratch_shapes=[pltpu.VMEM(cnt_in.shape, F32)],
        compiler_params=_params("arbitrary"),
        name="router",
    )(*ins)


def _dispatch_kernel(dest_ref, hn_ref, *rest, rows):
    xs_ref, sem = rest[-2:]

    def row_copy(r, k):
        d = dest_ref[r * TOP_K + k]
        return pltpu.make_async_copy(hn_ref.at[pl.ds(r, 1)], xs_ref.at[pl.ds(d, 1)], sem)

    def issue(r, c):
        for k in range(TOP_K):
            row_copy(r, k).start()
        return c

    def drain(r, c):
        for k in range(TOP_K):
            row_copy(r, k).wait()
        return c

    lax.fori_loop(0, rows, issue, 0)
    lax.fori_loop(0, rows, drain, 0)


def _dispatch(dest, hn, xs, rows, m):
    n, d = hn.shape
    ins = [dest, hn]
    in_specs = [pl.BlockSpec((rows * TOP_K,), lambda i: (i,), memory_space=pltpu.SMEM),
                pl.BlockSpec((rows, d), lambda i: (i, 0))]
    aliases = {}
    if xs is not None:
        ins.append(xs)
        in_specs.append(pl.BlockSpec(memory_space=pl.ANY))
        aliases = {2: 0}
    return pl.pallas_call(
        functools.partial(_dispatch_kernel, rows=rows),
        grid=(n // rows,),
        in_specs=in_specs,
        out_specs=pl.BlockSpec(memory_space=pl.ANY),
        out_shape=jax.ShapeDtypeStruct((m, d), F32),
        scratch_shapes=[pltpu.SemaphoreType.DMA(())],
        input_output_aliases=aliases,
        compiler_params=_params("arbitrary"),
        name="dispatch",
    )(*ins)


def _experts_kernel(blk_ref, exp_ref, lo_ref, hi_ref, first_ref, newe_ref,
                    xs_ref, wgu_ref, wd_ref, y_ref, wgu_sc, wd_sc):
    del blk_ref, exp_ref
    s = pl.program_id(0)
    lo = lo_ref[s]
    hi = hi_ref[s]

    @pl.when(newe_ref[s] == 1)
    def _():
        wgu_sc[...] = wgu_ref[0].astype(BF16)
        wd_sc[...] = wd_ref[0].astype(BF16)

    @pl.when(first_ref[s] == 1)
    def _():
        y_ref[...] = jnp.zeros(y_ref.shape, F32)

    @pl.when(hi > lo)
    def _():
        gu = _dot(xs_ref[...].astype(BF16), wgu_sc[...])
        half = gu.shape[-1] // 2
        act = _silu(gu[:, :half]) * gu[:, half:]
        row = lax.broadcasted_iota(I32, act.shape, 0)
        act = jnp.where((row >= lo) & (row < hi), act, 0.0)
        y_ref[...] += _dot(act.astype(BF16), wd_sc[...])


def _experts(steps, xs, w_gu, w_down):
    m, d = xs.shape
    rb = EXPERT_ROWS
    n_steps = steps[0].shape[0]
    de2 = w_gu.shape[-1]
    grid_spec = pltpu.PrefetchScalarGridSpec(
        num_scalar_prefetch=6,
        grid=(n_steps,),
        in_specs=[pl.BlockSpec((rb, d), lambda s, blk, ex, *_: (blk[s], 0)),
                  pl.BlockSpec((1, d, de2), lambda s, blk, ex, *_: (ex[s], 0, 0)),
                  pl.BlockSpec((1, de2 // 2, d), lambda s, blk, ex, *_: (ex[s], 0, 0))],
        out_specs=pl.BlockSpec((rb, d), lambda s, blk, ex, *_: (blk[s], 0)),
        scratch_shapes=[pltpu.VMEM((d, de2), BF16), pltpu.VMEM((de2 // 2, d), BF16)],
    )
    return pl.pallas_call(
        _experts_kernel,
        grid_spec=grid_spec,
        out_shape=jax.ShapeDtypeStruct((m, d), F32),
        compiler_params=_params("arbitrary"),
        name="routed_experts",
    )(*steps, xs, w_gu, w_down)


def _expert_steps(counts, m):
    rb = EXPERT_ROWS
    nb = m // rb
    start = jnp.cumsum(counts) - counts
    end = start + counts
    cuts = jnp.sort(jnp.concatenate([jnp.arange(nb, dtype=I32) * rb, start.astype(I32)]))
    nxt = jnp.concatenate([cuts[1:], jnp.array([m], I32)])
    blk = jnp.minimum(cuts // rb, nb - 1)
    ex = jnp.minimum(jnp.sum((end[None, :] <= cuts[:, None]).astype(I32), axis=1), N_EXPERTS - 1)
    lo = cuts - blk * rb
    hi = nxt - blk * rb
    first = jnp.concatenate([jnp.ones((1,), I32), (blk[1:] != blk[:-1]).astype(I32)])
    newe = jnp.concatenate([jnp.ones((1,), I32), (ex[1:] != ex[:-1]).astype(I32)])
    return blk, ex, lo, hi, first, newe


def _combine_kernel(dest_ref, ew_ref, x_ref, hn_ref, gt_ref, wsgu_ref, wsd_ref, fg_ref, yb_ref,
                    o_ref, buf, sem, *, rows, final):
    def row_copy(r, k):
        d = dest_ref[r * TOP_K + k]
        return pltpu.make_async_copy(yb_ref.at[pl.ds(d, 1)], buf.at[k, pl.ds(r, 1)], sem)

    def issue(r, c):
        for k in range(TOP_K):
            row_copy(r, k).start()
        return c

    def drain(r, c):
        for k in range(TOP_K):
            row_copy(r, k).wait()
        return c

    lax.fori_loop(0, rows, issue, 0)
    shared = _swiglu(hn_ref[...].astype(BF16), wsgu_ref[...], wsd_ref[...])
    lax.fori_loop(0, rows, drain, 0)
    ew = ew_ref[...]
    routed = ew[:, 0:1] * buf[0]
    for k in range(1, TOP_K):
        routed = routed + ew[:, k:k + 1] * buf[k]
    x = x_ref[...] + gt_ref[0] * (routed + shared)
    if final:
        x = _rms(x, fg_ref[...])
    o_ref[...] = x


def _combine(grp, dest, ew, x, hn, mod, ws_gu, ws_down, fg, yb, final):
    d = x.shape[-1]
    rows = grp.rows
    return pl.pallas_call(
        functools.partial(_combine_kernel, rows=rows, final=final),
        grid=(grp.n_blocks,),
        in_specs=[pl.BlockSpec((rows * TOP_K,), lambda i: (i,), memory_space=pltpu.SMEM),
                  pl.BlockSpec((rows, TOP_K), lambda i: (i, 0)),
                  grp.row_spec(d), grp.row_spec(d), grp.mod_spec(2, d),
                  _const_spec(ws_gu.shape), _const_spec(ws_down.shape), _const_spec((1, d)),
                  pl.BlockSpec(memory_space=pl.ANY)],
        out_specs=grp.row_spec(d),
        out_shape=jax.ShapeDtypeStruct(x.shape, F32),
        scratch_shapes=[pltpu.VMEM((TOP_K, rows, d), F32), pltpu.SemaphoreType.DMA(())],
        compiler_params=_params("arbitrary"),
        name="combine",
    )(dest, ew, x, hn, mod, ws_gu, ws_down, fg.reshape(1, d), yb)


def _proj_kernel(*refs, q_scale, transposed):
    if transposed:
        (x_ref, ksh_ref, ksc_ref, kg_ref, wkv_ref, wkvt_ref, bsh_ref, bsc_ref, bg_ref, wq_ref,
         k_ref, v_ref, kb_ref, vb_ref, q_ref) = refs
    else:
        (x_ref, ksh_ref, ksc_ref, kg_ref, wkv_ref, bsh_ref, bsc_ref, bg_ref, wq_ref,
         k_ref, v_ref, q_ref) = refs
    x = x_ref[...]
    kvn = _ada(x, kg_ref[...], ksh_ref[0], ksc_ref[0]).astype(BF16)
    kv = _dot(kvn, wkv_ref[...])
    half = kv.shape[-1] // 2
    if transposed:
        kvt = _dot_nt(wkvt_ref[...], kvn)
        k_ref[0] = kvt[:half]
        v_ref[0] = kvt[half:]
        kb_ref[...] = kv[:, :half].astype(BF16)
        vb_ref[...] = kv[:, half:].astype(BF16)
    else:
        k_ref[...] = kv[:, :half]
        v_ref[...] = kv[:, half:]
    hn = _ada(x, bg_ref[...], bsh_ref[0], bsc_ref[0])
    q_ref[...] = (_dot(hn.astype(BF16), wq_ref[...]) * q_scale).astype(BF16)


def _proj(grp, x, kv_mod, kv_g, w_kv, b_mod, b_g, w_q, q_scale, n_seq=None):
    d = x.shape[-1]
    n = grp.n_rows
    hd = w_q.shape[-1]
    transposed = n_seq is not None
    ins = [x, kv_mod, kv_mod, kv_g.reshape(1, d), w_kv]
    in_specs = [grp.row_spec(d), grp.mod_spec(0, d), grp.mod_spec(1, d), _const_spec((1, d)),
                _const_spec(w_kv.shape)]
    if transposed:
        ins.append(w_kv.T)
        in_specs.append(_const_spec((w_kv.shape[1], w_kv.shape[0])))
        bps = grp.blocks_per_seq
        t_spec = pl.BlockSpec((1, hd, grp.rows), lambda i: (i // bps, 0, i % bps))
        out_specs = [t_spec, t_spec, grp.row_spec(hd), grp.row_spec(hd), grp.row_spec(hd)]
        t_shape = jax.ShapeDtypeStruct((n_seq, hd, n // n_seq), F32)
        out_shape = [t_shape, t_shape] + [jax.ShapeDtypeStruct((n, hd), BF16)] * 3
    else:
        out_specs = [grp.row_spec(hd)] * 3
        out_shape = [jax.ShapeDtypeStruct((n, hd), F32), jax.ShapeDtypeStruct((n, hd), F32),
                     jax.ShapeDtypeStruct((n, hd), BF16)]
    ins += [b_mod, b_mod, b_g.reshape(1, d), w_q]
    in_specs += [grp.mod_spec(0, d), grp.mod_spec(1, d), _const_spec((1, d)), _const_spec(w_q.shape)]
    return pl.pallas_call(
        functools.partial(_proj_kernel, q_scale=q_scale, transposed=transposed),
        grid=(grp.n_blocks,),
        in_specs=in_specs, out_specs=out_specs, out_shape=out_shape,
        compiler_params=_params("arbitrary"),
        name="kv_q_projection",
    )(*ins)


def _later_key_matrix(tk):
    j = lax.broadcasted_iota(I32, (2 * tk, tk), 0) % tk
    return jnp.where(j > lax.broadcasted_iota(I32, (2 * tk, tk), 1), 1.0, 0.0).astype(BF16)


LOG2E = 1.4426950408889634


def _sb_block(z2, tri2, carry, mask):
    log_b, tail, row_sum = _sb_logs(z2, tri2, mask)
    return _sb_weights(log_b, tail, carry, mask), carry + row_sum


def _sb_logs(z2, tri2, mask):
    e = jnp.exp2(-jnp.abs(z2))
    neg_l1m = jnp.maximum(z2, 0.0) + jnp.log2(1.0 + e)
    log_b = z2 - neg_l1m
    if mask is not None:
        neg_l1m = jnp.where(mask, neg_l1m, 0.0)
    hi = neg_l1m.astype(BF16)
    lo = (neg_l1m - hi.astype(F32)).astype(BF16)
    tail = _dot(jnp.concatenate([hi, lo], axis=1), tri2)
    return log_b, tail, jnp.sum(neg_l1m, axis=1, keepdims=True)


def _sb_weights(log_b, tail, carry, mask):
    w = jnp.exp2(log_b - tail - carry)
    if mask is not None:
        w = jnp.where(mask, w, 0.0)
    return w


def _attn_prompt_kernel(bias_ref, q_ref, k_ref, v_ref, o_ref, acc_sc, z_sc, lb_sc, tail_sc, *, tq, tk):
    hp = pl.program_id(1)
    qi = pl.program_id(2)
    q = q_ref[0]
    half = q.shape[-1] // 2
    tri2 = _later_key_matrix(tk)
    q_pos = qi * tq + lax.broadcasted_iota(I32, (tq, tk), 0)
    k_off = lax.broadcasted_iota(I32, (tq, tk), 1)
    n_diag = tq // tk
    assert n_diag == 2, "the block pipeline below alternates two slots per loop trip"
    bias2 = [bias_ref[hp * 2] * LOG2E, bias_ref[hp * 2 + 1] * LOG2E]
    first_head = lax.broadcasted_iota(I32, (tk, q.shape[-1]), 1) < half
    acc_sc[...] = jnp.zeros(acc_sc.shape, F32)

    def split_heads(a):
        zero = jnp.zeros_like(a)
        return jnp.concatenate([jnp.where(first_head, a, zero), jnp.where(first_head, zero, a)], axis=0)

    def keys(ref, j):
        return split_heads(ref[0, pl.ds(pl.multiple_of(j * tk, tk), tk), :])

    def logits(j):
        return _dot_nt(q, keys(k_ref, j))

    def logs(z_pair, mask):
        parts = [_sb_logs(z_pair[:, hh * tk:(hh + 1) * tk] + bias2[hh], tri2, mask) for hh in range(2)]
        return (jnp.concatenate([p[0] for p in parts], axis=1),
                jnp.concatenate([p[1] for p in parts], axis=1), tuple(p[2] for p in parts))

    def accumulate(j, log_b, tail, carries, mask):
        ws = [_sb_weights(log_b[:, hh * tk:(hh + 1) * tk], tail[:, hh * tk:(hh + 1) * tk], carries[hh],
                          mask).astype(BF16) for hh in range(2)]
        acc_sc[...] += _dot(jnp.concatenate(ws, axis=1), keys(v_ref, j))

    carries = (jnp.zeros((tq, 1), F32), jnp.zeros((tq, 1), F32))
    for dj in reversed(range(n_diag)):
        j = qi * n_diag + dj
        mask = pl.multiple_of(j * tk, tk) + k_off < q_pos
        log_b, tail, sums = logs(logits(j), mask)
        accumulate(j, log_b, tail, carries, mask)
        carries = tuple(c + s for c, s in zip(carries, sums))

    n_full = qi * n_diag
    last = jnp.maximum(n_full - 1, 0)
    z_sc[1] = logits(last)
    lb_sc[0] = jnp.full(lb_sc.shape[1:], -1e30, F32)
    tail_sc[0] = jnp.zeros(tail_sc.shape[1:], F32)

    def body(it, c):
        cur, prev = c
        for u in range(2):
            j = n_full - 1 - 2 * it - u
            z_sc[u] = logits(jnp.maximum(j - 1, 0))
            log_b, tail, sums = logs(z_sc[1 - u], None)
            accumulate(jnp.minimum(j + 1, last), lb_sc[u], tail_sc[u], prev, None)
            lb_sc[1 - u] = log_b
            tail_sc[1 - u] = tail
            prev = cur
            cur = tuple(a + s for a, s in zip(cur, sums))
        return cur, prev

    _, prev = lax.fori_loop(0, qi, body, (carries, carries))
    accumulate(0, lb_sc[0], tail_sc[0], prev, None)
    o_ref[0] = acc_sc[...].astype(o_ref.dtype)


def _attn_prompt(q, k, v, bias):
    b, t, hd = q.shape
    tq, tk = ATTN_Q_ROWS, ATTN_K_ROWS
    pair = 2 * (hd // N_HEADS)
    return pl.pallas_call(
        functools.partial(_attn_prompt_kernel, tq=tq, tk=tk),
        grid=(b, hd // pair, t // tq),
        in_specs=[pl.BlockSpec(memory_space=pltpu.SMEM),
                  pl.BlockSpec((1, tq, pair), lambda bi, hp, qi: (bi, qi, hp)),
                  pl.BlockSpec((1, t, pair), lambda bi, hp, qi: (bi, 0, hp)),
                  pl.BlockSpec((1, t, pair), lambda bi, hp, qi: (bi, 0, hp))],
        out_specs=pl.BlockSpec((1, tq, pair), lambda bi, hp, qi: (bi, qi, hp)),
        out_shape=jax.ShapeDtypeStruct((b, t, hd), BF16),
        scratch_shapes=[pltpu.VMEM((tq, pair), F32)] + [pltpu.VMEM((2, tq, 2 * tk), F32)] * 3,
        compiler_params=_params("arbitrary", "arbitrary", "arbitrary"),
        name="sb_attention_prompt",
    )(bias, q, k, v)


def _attn_sample_kernel(pt_ref, bias_ref, q_ref, kn_ref, vn_ref, *rest, page, q_pad, ppg):
    del pt_ref
    kc_refs, vc_refs = rest[:ppg], rest[ppg:2 * ppg]
    o_ref, acc_sc, carry_sc = rest[2 * ppg:]
    j = pl.program_id(1)
    heads = q_ref.shape[1]
    rows = heads * q_pad
    tri2 = _later_key_matrix(page)

    @pl.when(j == 0)
    def _():
        acc_sc[...] = jnp.zeros(acc_sc.shape, F32)
        carry_sc[...] = jnp.zeros(carry_sc.shape, F32)

    def process(k_ref, v_ref, masked):
        k3 = k_ref[0].astype(BF16)
        v3 = v_ref[0].astype(BF16)
        z3 = lax.dot_general(q_ref[0], k3, (((2,), (1,)), ((0,), (0,))), preferred_element_type=F32)
        z2 = z3.reshape(rows, page) + bias_ref[...]
        mask = None
        if masked:
            qidx = lax.broadcasted_iota(I32, (rows, page), 0) % q_pad
            mask = lax.broadcasted_iota(I32, (rows, page), 1) < qidx
        w, carry = _sb_block(z2, tri2, carry_sc[...][:, 0:1], mask)
        carry_sc[...] = jnp.broadcast_to(carry, carry_sc.shape)
        w3 = w.reshape(heads, q_pad, page).astype(BF16)
        acc_sc[...] += lax.dot_general(w3, v3, (((2,), (2,)), ((0,), (0,))), preferred_element_type=F32)

    @pl.when(j == 0)
    def _():
        process(kn_ref, vn_ref, True)

    @pl.when(j > 0)
    def _():
        for p in range(ppg):
            process(kc_refs[p], vc_refs[p], False)

    @pl.when(j == pl.num_programs(1) - 1)
    def _():
        o_ref[0] = acc_sc[...]


def _attn_sample(page_table, bias_rows, q3, k_new, v_new, cache_kt, cache_vt, ppg):
    bs, heads, q_pad, dh = q3.shape
    page = cache_kt.shape[-1]
    n_pages = page_table.shape[1]
    rows = heads * q_pad
    assert n_pages % ppg == 0

    def cache_spec(p):
        def index(b, j, pt):
            logical = n_pages - 1 - ((jnp.maximum(j, 1) - 1) * ppg + p)
            return (pt[b, logical], 0, 0, 0)
        return pl.BlockSpec((1, heads, dh, page), index)

    new_spec = pl.BlockSpec((1, heads, dh, page), lambda b, j, pt: (b, 0, 0, 0))
    q_spec = pl.BlockSpec((1, heads, q_pad, dh), lambda b, j, pt: (b, 0, 0, 0))
    grid_spec = pltpu.PrefetchScalarGridSpec(
        num_scalar_prefetch=1,
        grid=(bs, n_pages // ppg + 1),
        in_specs=[pl.BlockSpec((rows, page), lambda b, j, pt: (0, 0)), q_spec, new_spec, new_spec]
                 + [cache_spec(p) for p in range(ppg)] * 2,
        out_specs=q_spec,
        scratch_shapes=[pltpu.VMEM((heads, q_pad, dh), F32), pltpu.VMEM((rows, LANES), F32)],
    )
    return pl.pallas_call(
        functools.partial(_attn_sample_kernel, page=page, q_pad=q_pad, ppg=ppg),
        grid_spec=grid_spec,
        out_shape=jax.ShapeDtypeStruct((bs, heads, q_pad, dh), F32),
        compiler_params=_params("arbitrary", "arbitrary"),
        name="sb_attention_sample",
    )(page_table, bias_rows, q3, k_new, v_new, *([cache_kt] * ppg), *([cache_vt] * ppg))


def _permute_experts(a):
    per = N_EXPERTS // N_EXPERT_GROUPS
    return a.reshape(N_EXPERT_GROUPS, per, *a.shape[1:]).swapaxes(0, 1).reshape(a.shape)


def _moe_layer(groups, cgroups, xs_rows, mods, pres, g, w_router, r_bias, w_gu, w_down, ws_gu, ws_down,
               fg, final):
    d = xs_rows[0].shape[-1]
    w_rt = _permute_experts(w_router.T)
    rbias = jnp.broadcast_to(_permute_experts(r_bias.reshape(N_EXPERTS, 1)), (N_EXPERTS, LANES))
    cnt = jnp.zeros((N_EXPERTS, LANES), F32)
    routed = []
    x_new = []
    for grp, x, mod, pre in zip(groups, xs_rows, mods, pres):
        outs = _router(grp, x, mod, g, w_rt, rbias, cnt, pre)
        if pre is not None:
            x, outs = outs[0], outs[1:]
        hn, e_t, p_t, w_t, cnt = outs
        routed.append((hn, e_t, p_t, w_t))
        x_new.append(x)
    per = N_EXPERTS // N_EXPERT_GROUPS
    counts = cnt[:, 0].astype(I32).reshape(per, N_EXPERT_GROUPS).T.reshape(N_EXPERTS)
    start = jnp.cumsum(counts) - counts
    m = sum(grp.n_rows for grp in groups) * TOP_K
    steps = _expert_steps(counts, m)
    xs = None
    dests = []
    expert_ids = jnp.arange(N_EXPERTS, dtype=I32)[:, None, None]
    for grp, (hn, e_t, p_t, w_t) in zip(groups, routed):
        start_sel = jnp.sum(jnp.where(e_t[None] == expert_ids, start[:, None, None], 0), axis=0)
        dest = (start_sel + p_t).T.reshape(-1)
        dests.append(dest)
        xs = _dispatch(dest, hn, xs, min(DISPATCH_ROWS, grp.n_rows), m)
    yb = _experts(steps, xs, w_gu, w_down)
    outs = []
    for cgrp, x, mod, dest, (hn, e_t, p_t, w_t) in zip(cgroups, x_new, mods, dests, routed):
        outs.append(_combine(cgrp, dest, w_t.T, x, hn, mod, ws_gu, ws_down, fg, yb, final))
    return outs


def kernel(x_prompt, x_sample, c_prompt, c_sample, state_conv, state_h, cache_k, cache_v, page_table, a_norm_g, a_mod_w, a_mod_b, a_w_in, a_conv_w, a_conv_b, a_w_gate_r, a_b_gate_r, a_w_gate_i, a_b_gate_i, a_lambda, a_w_out, kv_norm_g, kv_mod_w, kv_mod_b, w_kv, b_norm_g, b_mod_w, b_mod_b, b_w_q, b_sb_bias, b_w_o, m_norm_g, m_mod_w, m_mod_b, m_w_router, m_router_bias, m_w_gate_up, m_w_down, m_ws_gate_up, m_ws_down, final_norm_g):
    bp, t, d = x_prompt.shape
    bs, ts, _ = x_sample.shape
    depth = m_norm_g.shape[0]
    n_a = a_norm_g.shape[0]
    d_rnn = a_w_out.shape[1]
    heads = N_HEADS
    dh = b_w_q.shape[-1] // heads
    page = cache_k.shape[1]
    np_rows = bp * t
    ns_rows = bs * ts
    assert n_a == 1 and depth == 2, "layer pattern of this step: one self-decoder, one cross-decoder layer"

    n_c = bp + bs
    c_rows = -(-n_c // SUBLANES) * SUBLANES
    c_all = jnp.zeros((c_rows, d), F32).at[:bp].set(c_prompt).at[bp:n_c].set(c_sample)

    def mods(w, b):
        m = _modulation(c_all, w, b)
        mp = m[:bp].reshape(bp, 1, -1)
        ms = jnp.tile(m[bp:n_c], (ts, 1)).reshape(1, ns_rows, -1)
        return mp, ms

    a_mod = mods(a_mod_w[0], a_mod_b[0])
    kv_mod = mods(kv_mod_w, kv_mod_b)
    b_mod = mods(b_mod_w[0], b_mod_b[0])
    m_mod = [mods(m_mod_w[l], m_mod_b[l]) for l in range(depth)]

    bf = lambda w: w.astype(BF16)

    rec_w = (a_norm_g[0], bf(a_w_in[0]), a_conv_w[0], a_conv_b[0], bf(a_w_gate_r[0]), a_b_gate_r[0],
             bf(a_w_gate_i[0]), a_b_gate_i[0], a_lambda[0], bf(a_w_out[0]))
    hist_p = jnp.zeros((bp, CONV_W - 1, d_rnn), F32)
    h0_p = jnp.zeros((bp, 1, d_rnn), F32)
    x1_p, conv_p, h_p = _recurrent(x_prompt, a_mod[0], hist_p, h0_p, *rec_w,
                                   stride=1, steps=REC_ROWS, mod_rows=1)
    xs_tm = x_sample.swapaxes(0, 1).reshape(1, ns_rows, d)
    hist_s = state_conv[0].swapaxes(0, 1).reshape(1, (CONV_W - 1) * bs, d_rnn)
    h0_s = state_h[0].reshape(1, bs, d_rnn)
    x1_s, conv_s, h_s = _recurrent(xs_tm, a_mod[1], hist_s, h0_s, *rec_w,
                                   stride=bs, steps=ts, mod_rows=ns_rows)

    grp_p = _Group(np_rows, ROUTER_ROWS, t // ROUTER_ROWS, 1)
    grp_s = _Group(ns_rows, ns_rows, 1, ns_rows)
    groups = [grp_p, grp_s]
    cgroups = [_Group(np_rows, COMBINE_ROWS, t // COMBINE_ROWS, 1), grp_s]

    x2_p, x2_s = _moe_layer(groups, cgroups, [x1_p.reshape(np_rows, d), x1_s.reshape(ns_rows, d)],
                            [m_mod[0][0], m_mod[0][1]], [None, None], m_norm_g[0], m_w_router[0],
                            m_router_bias[0], m_w_gate_up[0], m_w_down[0], bf(m_ws_gate_up[0]),
                            bf(m_ws_down[0]), final_norm_g, False)

    q_scale = float(dh) ** -0.5 * LOG2E
    w_kv_b, w_q_b = bf(w_kv), bf(b_w_q[0])
    pgrp_p = _Group(np_rows, PROJ_ROWS, t // PROJ_ROWS, 1)
    kt_p, vt_p, kb_p, vb_p, q_p = _proj(pgrp_p, x2_p, kv_mod[0], kv_norm_g, w_kv_b, b_mod[0], b_norm_g[0],
                                        w_q_b, q_scale, n_seq=bp)
    k_s, v_s, q_s = _proj(grp_s, x2_s, kv_mod[1], kv_norm_g, w_kv_b, b_mod[1], b_norm_g[0],
                          w_q_b, q_scale)

    hd = heads * dh
    o_p = _attn_prompt(q_p.reshape(bp, t, hd), kb_p.reshape(bp, t, hd), vb_p.reshape(bp, t, hd),
                       b_sb_bias[0]).reshape(np_rows, hd)

    q_pad = SUBLANES
    q3 = jnp.zeros((bs, heads, q_pad, dh), BF16).at[:, :, :ts].set(
        q_s.reshape(ts, bs, heads, dh).transpose(1, 2, 0, 3))

    def new_keys(a):
        a = a.reshape(ts, bs, heads, dh).transpose(1, 2, 3, 0)
        return jnp.zeros((bs, heads, dh, page), F32).at[..., :ts].set(a)

    bias_rows = jnp.broadcast_to(jnp.repeat(b_sb_bias[0] * LOG2E, q_pad)[:, None], (heads * q_pad, page))
    o3 = _attn_sample(page_table, bias_rows, q3, new_keys(k_s), new_keys(v_s),
                      cache_k.transpose(0, 2, 3, 1), cache_v.transpose(0, 2, 3, 1), SAMPLE_PAGES_PER_STEP)
    o_s = o3[:, :, :ts].transpose(2, 0, 1, 3).reshape(ns_rows, hd).astype(BF16)

    w_o_b = bf(b_w_o[0])
    y_p, y_s = _moe_layer(groups, cgroups, [x2_p, x2_s], [m_mod[1][0], m_mod[1][1]],
                          [(o_p, w_o_b, b_mod[0]), (o_s, w_o_b, b_mod[1])], m_norm_g[1], m_w_router[1],
                          m_router_bias[1], m_w_gate_up[1], m_w_down[1], bf(m_ws_gate_up[1]),
                          bf(m_ws_down[1]), final_norm_g, True)

    def from_tm(a, *tail):
        return a.reshape(ts, bs, *tail).swapaxes(0, 1)

    return (y_p.reshape(bp, t, d),
            from_tm(y_s, d),
            conv_p[None],
            h_p.reshape(1, bp, d_rnn),
            kt_p.reshape(bp, heads, dh, t).transpose(0, 3, 1, 2),
            vt_p.reshape(bp, heads, dh, t).transpose(0, 3, 1, 2),
            conv_s.reshape(CONV_W - 1, bs, d_rnn).swapaxes(0, 1)[None],
            h_s.reshape(1, bs, d_rnn),
            from_tm(k_s, heads, dh),
            from_tm(v_s, heads, dh))
```

```python
import functools

import jax
import jax.numpy as jnp
from jax import lax
from jax.experimental import pallas as pl
from jax.experimental.pallas import tpu as pltpu

F32 = jnp.float32
BF16 = jnp.bfloat16
I32 = jnp.int32

CONV_W = 4
LRU_BLOCKS = 4
RGLRU_C = 8.0
N_HEADS = 16
N_EXPERTS = 64
TOP_K = 8
N_EXPERT_GROUPS = 8
TOPK_GROUPS = 4
ROUTED_SCALE = 2.5
EPS = 1e-6
NEG_INF = float("-inf")

LANES = 128
SUBLANES = 8
TILE_CHUNKS = SUBLANES
VMEM_LIMIT_BYTES = 56 * 1024 * 1024

REC_ROWS = 256
ROUTER_ROWS = 512
PROJ_ROWS = 512
DISPATCH_ROWS = 256
COMBINE_ROWS = 128
EXPERT_ROWS = 512
ATTN_Q_ROWS = 256
ATTN_K_ROWS = 128
SAMPLE_PAGES_PER_STEP = 4


def _params(*sem):
    return pltpu.CompilerParams(dimension_semantics=sem, vmem_limit_bytes=VMEM_LIMIT_BYTES)


def _dot(a, b):
    return jnp.dot(a, b, preferred_element_type=F32)


def _dot_nt(a, b, precision=None):
    return lax.dot_general(a, b, (((1,), (1,)), ((), ())), precision=precision,
                           preferred_element_type=F32)


def _sigmoid(x):
    return 1.0 / (1.0 + jnp.exp(-x))


def _silu(x):
    return x * _sigmoid(x)


def _gelu_tanh(x):
    return 0.5 * x * (1.0 + jnp.tanh(0.7978845608028654 * (x + 0.044715 * (x * x * x))))


def _rms(x, g):
    ms = jnp.mean(x * x, axis=-1, keepdims=True)
    return x * lax.rsqrt(ms + EPS) * g


def _ada(x, g, shift, scale):
    return _rms(x, g) * (1.0 + scale) + shift


def _swiglu(xb, w_gu, w_down):
    gu = _dot(xb, w_gu)
    half = gu.shape[-1] // 2
    act = _silu(gu[:, :half]) * gu[:, half:]
    return _dot(act.astype(BF16), w_down)


def _mod_kernel(c_ref, w_ref, b_ref, o_ref):
    c = c_ref[...]
    o_ref[...] = _dot(_silu(c).astype(BF16), w_ref[...].astype(BF16)) + b_ref[...]


def _modulation(c_all, w, b):
    rows, d = c_all.shape
    n_out = w.shape[1]
    tn = 1024
    return pl.pallas_call(
        _mod_kernel,
        grid=(n_out // tn,),
        in_specs=[pl.BlockSpec((rows, d), lambda j: (0, 0)),
                  pl.BlockSpec((d, tn), lambda j: (0, j)),
                  pl.BlockSpec((1, tn), lambda j: (0, j))],
        out_specs=pl.BlockSpec((rows, tn), lambda j: (0, j)),
        out_shape=jax.ShapeDtypeStruct((rows, n_out), F32),
        compiler_params=_params("arbitrary"),
        name="modulation",
    )(c_all, w, b.reshape(1, n_out))


class _Group:
    def __init__(self, n_rows, rows, blocks_per_seq, mod_rows):
        self.n_rows = n_rows
        self.rows = rows
        self.blocks_per_seq = blocks_per_seq
        self.mod_rows = mod_rows
        self.n_blocks = n_rows // rows

    def mod_spec(self, col, d):
        bps = self.blocks_per_seq
        return pl.BlockSpec((1, self.mod_rows, d), lambda i: (i // bps, 0, col))

    def row_spec(self, width):
        return pl.BlockSpec((self.rows, width), lambda i: (i, 0))


def _const_spec(shape):
    nd = len(shape)
    return pl.BlockSpec(shape, lambda *_: (0,) * nd)


def _scan_rows(a, u, stride, steps):
    row = lax.broadcasted_iota(I32, a.shape, 0)
    s = 1
    while s < steps:
        sh = s * stride
        a_p = pltpu.roll(a, sh, 0)
        u_p = pltpu.roll(u, sh, 0)
        m = row >= sh
        u = jnp.where(m, a * u_p + u, u)
        a = jnp.where(m, a * a_p, a)
        s *= 2
    return a, u


def _rec_kernel(x_ref, sh_ref, sc_ref, gt_ref, hist_ref, h0_ref, g_ref, w_in_ref, cw_ref, cb_ref,
                wr_ref, br_ref, wi_ref, bi_ref, lam_ref, w_out_ref,
                xo_ref, conv_ref, hl_ref, xpad_sc, h_sc, *, stride, steps, hist0):
    t = pl.program_id(1)
    rows = stride * steps
    hrows = (CONV_W - 1) * stride
    d_rnn = h_sc.shape[-1]

    @pl.when(t == 0)
    def _():
        xpad_sc[pl.ds(hist0 - hrows, hrows), :] = hist_ref[0]
        h_sc[...] = h0_ref[0]

    x = x_ref[0]
    hn = _ada(x, g_ref[...], sh_ref[0], sc_ref[0])
    proj = _dot(hn.astype(BF16), w_in_ref[...])
    gate_br = proj[:, :d_rnn]
    xb = proj[:, d_rnn:]
    xpad_sc[pl.ds(hist0, rows), :] = xb
    xc = cb_ref[...] + cw_ref[0:1, :] * xpad_sc[pl.ds(hist0 - hrows, rows), :]
    for j in range(1, CONV_W):
        xc = xc + cw_ref[j:j + 1, :] * xpad_sc[pl.ds(hist0 - hrows + j * stride, rows), :]
    new_hist = xpad_sc[pl.ds(hist0 + rows - hrows, hrows), :]
    xpad_sc[pl.ds(hist0 - hrows, hrows), :] = new_hist
    conv_ref[0] = new_hist

    bw = d_rnn // LRU_BLOCKS
    r_parts, i_parts = [], []
    for n in range(LRU_BLOCKS):
        xblk = xc[:, n * bw:(n + 1) * bw].astype(BF16)
        r_parts.append(_dot(xblk, wr_ref[n]))
        i_parts.append(_dot(xblk, wi_ref[n]))
    r = _sigmoid(jnp.concatenate(r_parts, axis=-1) + br_ref[...])
    ig = _sigmoid(jnp.concatenate(i_parts, axis=-1) + bi_ref[...])
    neg_lam = -lam_ref[...]
    softplus = jnp.maximum(neg_lam, 0.0) + jnp.log1p(jnp.exp(-jnp.abs(neg_lam)))
    log_a = (-RGLRU_C * r) * softplus
    a = jnp.exp(log_a)
    u = jnp.sqrt(-jnp.tanh(log_a) * (a * a + 1.0)) * (ig * xc)
    a_cum, u_cum = _scan_rows(a, u, stride, steps)
    h_prev = h_sc[...]
    if stride == 1:
        h = a_cum * h_prev + u_cum
    else:
        h = a_cum * jnp.concatenate([h_prev] * steps, axis=0) + u_cum
    h_last = h[rows - stride:, :]
    h_sc[...] = h_last
    hl_ref[0] = h_last
    y = _dot((_gelu_tanh(gate_br) * h).astype(BF16), w_out_ref[...])
    xo_ref[0] = x + gt_ref[0] * y


def _recurrent(x3, mod, hist, h0, g, w_in, conv_w, conv_b, w_r, b_r, w_i, b_i, lam, w_out,
               *, stride, steps, mod_rows):
    nb, t_rows, d = x3.shape
    d_rnn = w_out.shape[0]
    rows = stride * steps
    nt = t_rows // rows
    hrows = (CONV_W - 1) * stride
    hist0 = -(-hrows // SUBLANES) * SUBLANES
    assert rows >= hrows and rows % SUBLANES == 0
    x_spec = pl.BlockSpec((1, rows, d), lambda b, t: (b, t, 0))

    def mspec(col):
        if mod_rows == 1:
            return pl.BlockSpec((1, 1, d), lambda b, t: (b, 0, col))
        return pl.BlockSpec((1, rows, d), lambda b, t: (b, 0, col))

    def cspec(shape):
        nd = len(shape)
        return pl.BlockSpec(shape, lambda b, t: (0,) * nd)

    kern = functools.partial(_rec_kernel, stride=stride, steps=steps, hist0=hist0)
    return pl.pallas_call(
        kern,
        grid=(nb, nt),
        in_specs=[x_spec, mspec(0), mspec(1), mspec(2),
                  pl.BlockSpec((1, hrows, d_rnn), lambda b, t: (b, 0, 0)),
                  pl.BlockSpec((1, stride, d_rnn), lambda b, t: (b, 0, 0)),
                  cspec((1, d)), cspec(w_in.shape), cspec(conv_w.shape), cspec((1, d_rnn)),
                  cspec(w_r.shape), cspec((1, d_rnn)), cspec(w_i.shape), cspec((1, d_rnn)),
                  cspec((1, d_rnn)), cspec(w_out.shape)],
        out_specs=[x_spec,
                   pl.BlockSpec((1, hrows, d_rnn), lambda b, t: (b, 0, 0)),
                   pl.BlockSpec((1, stride, d_rnn), lambda b, t: (b, 0, 0))],
        out_shape=[jax.ShapeDtypeStruct(x3.shape, F32),
                   jax.ShapeDtypeStruct((nb, hrows, d_rnn), F32),
                   jax.ShapeDtypeStruct((nb, stride, d_rnn), F32)],
        scratch_shapes=[pltpu.VMEM((hist0 + rows, d_rnn), F32), pltpu.VMEM((stride, d_rnn), F32)],
        compiler_params=_params("arbitrary", "arbitrary"),
        name="recurrent_block",
    )(x3, mod, mod, mod, hist, h0, g.reshape(1, d), w_in, conv_w, conv_b.reshape(1, d_rnn),
      w_r, b_r.reshape(1, d_rnn), w_i, b_i.reshape(1, d_rnn), lam.reshape(1, d_rnn), w_out)


def _tree(op, xs):
    xs = list(xs)
    while len(xs) > 1:
        nxt = [op(xs[i], xs[i + 1]) for i in range(0, len(xs) - 1, 2)]
        if len(xs) % 2:
            nxt.append(xs[-1])
        xs = nxt
    return xs[0]


def _to_token_tiles(ref, x):
    rows = x.shape[0]
    for c in range(TILE_CHUNKS):
        ref[pl.ds(c, rows, stride=TILE_CHUNKS), :] = x[:, c * LANES:(c + 1) * LANES]


def _from_token_tiles(ref, rows):
    return jnp.concatenate([ref[pl.ds(c, rows, stride=TILE_CHUNKS), :] for c in range(TILE_CHUNKS)], axis=1)


def _router_kernel(*refs, pre):
    if pre:
        (x_ref, o_ref, wo_ref, agt_ref, sh_ref, sc_ref, g_ref, wrt_ref, rb_ref, cnt_in_ref, wsgu_ref, wsd_ref,
         xo_ref, hn_ref, shared_ref, e_ref, p_ref, w_ref, cnt_ref, run_sc) = refs
    else:
        (x_ref, sh_ref, sc_ref, g_ref, wrt_ref, rb_ref, cnt_in_ref, wsgu_ref, wsd_ref,
         hn_ref, shared_ref, e_ref, p_ref, w_ref, cnt_ref, run_sc) = refs
    i = pl.program_id(0)
    per = N_EXPERTS // N_EXPERT_GROUPS

    @pl.when(i == 0)
    def _():
        run_sc[...] = cnt_in_ref[...]

    x = x_ref[...]
    if pre:
        x = x + agt_ref[0] * _dot(o_ref[...], wo_ref[...])
        xo_ref[...] = x
    hn = _ada(x, g_ref[...], sh_ref[0], sc_ref[0])
    _to_token_tiles(hn_ref, hn)
    shared_ref[...] = _swiglu(hn.astype(BF16), wsgu_ref[...], wsd_ref[...])
    tb = hn.shape[0]
    logits = _dot_nt(wrt_ref[...], hn, precision=lax.Precision.HIGHEST)
    scores = _sigmoid(logits)
    sel = scores + rb_ref[...][:, 0:1]
    sel_j = [sel[j * SUBLANES:(j + 1) * SUBLANES, :] for j in range(per)]
    sc_j = [scores[j * SUBLANES:(j + 1) * SUBLANES, :] for j in range(per)]
    m1 = _tree(jnp.maximum, sel_j)
    j1 = _tree(jnp.minimum, [jnp.where(sel_j[j] == m1, j, per) for j in range(per)])
    m2 = _tree(jnp.maximum, [jnp.where(j1 == j, NEG_INF, sel_j[j]) for j in range(per)])
    gs = m1 + m2
    g_iota = lax.broadcasted_iota(I32, gs.shape, 0)
    rank = jnp.zeros(gs.shape, I32)
    for dlt in range(1, N_EXPERT_GROUPS):
        other = pltpu.roll(gs, dlt, 0)
        tie = jnp.where(g_iota >= dlt, 1, 0)
        rank = rank + jnp.where(other > gs, 1, jnp.where(other == gs, tie, 0))
    gmask = rank < TOPK_GROUPS
    masked = [jnp.where(gmask, s, NEG_INF) for s in sel_j]
    eid = [g_iota * per + j for j in range(per)]
    chosen = [jnp.zeros(gs.shape, F32) for _ in range(per)]
    e_rows, s_rows = [], []
    for k in range(TOP_K):
        m = jnp.max(_tree(jnp.maximum, masked), axis=0, keepdims=True)
        cand = _tree(jnp.minimum, [jnp.where(masked[j] == m, eid[j], N_EXPERTS) for j in range(per)])
        ek = jnp.min(cand, axis=0, keepdims=True)
        sk = jnp.zeros(gs.shape, F32)
        for j in range(per):
            oh = eid[j] == ek
            masked[j] = jnp.where(oh, NEG_INF, masked[j])
            chosen[j] = jnp.where(oh, 1.0, chosen[j])
            sk = sk + jnp.where(oh, sc_j[j], 0.0)
        e_rows.append(ek)
        s_rows.append(jnp.sum(sk, axis=0, keepdims=True))
    denom = _tree(lambda p, q: p + q, s_rows)
    sel_mask = jnp.concatenate(chosen, axis=0)
    upper = (lax.broadcasted_iota(I32, (tb, tb), 0) < lax.broadcasted_iota(I32, (tb, tb), 1))
    prefix = _dot(sel_mask.astype(BF16), jnp.where(upper, 1.0, 0.0).astype(BF16)) + run_sc[...][:, 0:1]
    pre_j = [prefix[j * SUBLANES:(j + 1) * SUBLANES, :] for j in range(per)]
    for k in range(TOP_K):
        pk = jnp.zeros(gs.shape, F32)
        for j in range(per):
            pk = pk + jnp.where(eid[j] == e_rows[k], pre_j[j], 0.0)
        e_ref[pl.ds(k, 1), :] = e_rows[k]
        p_ref[pl.ds(k, 1), :] = jnp.sum(pk, axis=0, keepdims=True).astype(I32)
        w_ref[pl.ds(k, 1), :] = s_rows[k] / denom * ROUTED_SCALE
    run = run_sc[...] + jnp.sum(sel_mask, axis=1, keepdims=True)
    run_sc[...] = run
    cnt_ref[...] = run


def _router(grp, x, mod, g, w_rt, rbias, cnt_in, ws_gu, ws_down, pre=None):
    d = x.shape[-1]
    assert d == TILE_CHUNKS * LANES
    n, tb = grp.n_rows, grp.rows
    ins = [x]
    in_specs = [grp.row_spec(d)]
    if pre is not None:
        o, w_o, amod = pre
        ins += [o, w_o, amod]
        in_specs += [grp.row_spec(o.shape[-1]), _const_spec(w_o.shape), grp.mod_spec(2, d)]
    ins += [mod, mod, g.reshape(1, d), w_rt, rbias, cnt_in, ws_gu, ws_down]
    in_specs += [grp.mod_spec(0, d), grp.mod_spec(1, d), _const_spec((1, d)), _const_spec(w_rt.shape),
                 _const_spec(rbias.shape), _const_spec(cnt_in.shape), _const_spec(ws_gu.shape),
                 _const_spec(ws_down.shape)]
    tok_spec = pl.BlockSpec((TOP_K, tb), lambda i: (0, i))
    out_specs = [pl.BlockSpec((tb * TILE_CHUNKS, LANES), lambda i: (i, 0)), grp.row_spec(d),
                 tok_spec, tok_spec, tok_spec, _const_spec(cnt_in.shape)]
    out_shape = [jax.ShapeDtypeStruct((n * TILE_CHUNKS, LANES), F32), jax.ShapeDtypeStruct((n, d), F32),
                 jax.ShapeDtypeStruct((TOP_K, n), I32),
                 jax.ShapeDtypeStruct((TOP_K, n), I32), jax.ShapeDtypeStruct((TOP_K, n), F32),
                 jax.ShapeDtypeStruct(cnt_in.shape, F32)]
    if pre is not None:
        out_specs = [grp.row_spec(d)] + out_specs
        out_shape = [jax.ShapeDtypeStruct((n, d), F32)] + out_shape
    return pl.pallas_call(
        functools.partial(_router_kernel, pre=pre is not None),
        grid=(grp.n_blocks,),
        in_specs=in_specs, out_specs=out_specs, out_shape=out_shape,
        scratch_shapes=[pltpu.VMEM(cnt_in.shape, F32)],
        compiler_params=_params("arbitrary"),
        name="router",
    )(*ins)


def _dispatch_kernel(dest_ref, hn_ref, *rest, rows):
    xs_ref, sem = rest[-2:]

    def tile(ref, r):
        return ref.at[pl.ds(pl.multiple_of(r * TILE_CHUNKS, TILE_CHUNKS), TILE_CHUNKS)]

    def row_copy(r, k):
        return pltpu.make_async_copy(tile(hn_ref, r), tile(xs_ref, dest_ref[r * TOP_K + k]), sem)

    def issue(r, c):
        for k in range(TOP_K):
            row_copy(r, k).start()
        return c

    def drain(r, c):
        for k in range(TOP_K):
            row_copy(r, k).wait()
        return c

    lax.fori_loop(0, rows, issue, 0)
    lax.fori_loop(0, rows, drain, 0)


def _dispatch(dest, hn, xs, rows, m):
    n = hn.shape[0] // TILE_CHUNKS
    d = LANES
    m = m * TILE_CHUNKS
    ins = [dest, hn]
    in_specs = [pl.BlockSpec((rows * TOP_K,), lambda i: (i,), memory_space=pltpu.SMEM),
                pl.BlockSpec((rows * TILE_CHUNKS, LANES), lambda i: (i, 0))]
    aliases = {}
    if xs is not None:
        ins.append(xs)
        in_specs.append(pl.BlockSpec(memory_space=pl.ANY))
        aliases = {2: 0}
    return pl.pallas_call(
        functools.partial(_dispatch_kernel, rows=rows),
        grid=(n // rows,),
        in_specs=in_specs,
        out_specs=pl.BlockSpec(memory_space=pl.ANY),
        out_shape=jax.ShapeDtypeStruct((m, d), F32),
        scratch_shapes=[pltpu.SemaphoreType.DMA(())],
        input_output_aliases=aliases,
        compiler_params=_params("arbitrary"),
        name="dispatch",
    )(*ins)


def _experts_kernel(blk_ref, exp_ref, lo_ref, hi_ref, first_ref, last_ref, newe_ref,
                    xs_ref, wgu_ref, wd_ref, y_ref, wgu_sc, wd_sc, x_sc, y_sc):
    del blk_ref, exp_ref
    s = pl.program_id(0)
    lo = lo_ref[s]
    hi = hi_ref[s]
    rb = x_sc.shape[0]

    @pl.when(newe_ref[s] == 1)
    def _():
        wgu_sc[...] = wgu_ref[0].astype(BF16)
        wd_sc[...] = wd_ref[0].astype(BF16)

    @pl.when(first_ref[s] == 1)
    def _():
        x_sc[...] = _from_token_tiles(xs_ref, rb).astype(BF16)
        y_sc[...] = jnp.zeros(y_sc.shape, F32)

    @pl.when(hi > lo)
    def _():
        gu = _dot(x_sc[...], wgu_sc[...])
        half = gu.shape[-1] // 2
        act = _silu(gu[:, :half]) * gu[:, half:]
        row = lax.broadcasted_iota(I32, act.shape, 0)
        act = jnp.where((row >= lo) & (row < hi), act, 0.0)
        y_sc[...] += _dot(act.astype(BF16), wd_sc[...])

    @pl.when(last_ref[s] == 1)
    def _():
        _to_token_tiles(y_ref, y_sc[...])


def _experts(steps, xs, w_gu, w_down):
    m = xs.shape[0] // TILE_CHUNKS
    d = TILE_CHUNKS * LANES
    rb = EXPERT_ROWS
    n_steps = steps[0].shape[0]
    de2 = w_gu.shape[-1]
    tile_spec = pl.BlockSpec((rb * TILE_CHUNKS, LANES), lambda s, blk, ex, *_: (blk[s], 0))
    grid_spec = pltpu.PrefetchScalarGridSpec(
        num_scalar_prefetch=7,
        grid=(n_steps,),
        in_specs=[tile_spec,
                  pl.BlockSpec((1, d, de2), lambda s, blk, ex, *_: (ex[s], 0, 0)),
                  pl.BlockSpec((1, de2 // 2, d), lambda s, blk, ex, *_: (ex[s], 0, 0))],
        out_specs=tile_spec,
        scratch_shapes=[pltpu.VMEM((d, de2), BF16), pltpu.VMEM((de2 // 2, d), BF16),
                        pltpu.VMEM((rb, d), BF16), pltpu.VMEM((rb, d), F32)],
    )
    return pl.pallas_call(
        _experts_kernel,
        grid_spec=grid_spec,
        out_shape=jax.ShapeDtypeStruct((m * TILE_CHUNKS, LANES), F32),
        compiler_params=_params("arbitrary"),
        name="routed_experts",
    )(*steps, xs, w_gu, w_down)


def _expert_steps(counts, m):
    rb = EXPERT_ROWS
    nb = m // rb
    start = jnp.cumsum(counts) - counts
    end = start + counts
    cuts = jnp.sort(jnp.concatenate([jnp.arange(nb, dtype=I32) * rb, start.astype(I32)]))
    nxt = jnp.concatenate([cuts[1:], jnp.array([m], I32)])
    blk = jnp.minimum(cuts // rb, nb - 1)
    ex = jnp.minimum(jnp.sum((end[None, :] <= cuts[:, None]).astype(I32), axis=1), N_EXPERTS - 1)
    lo = cuts - blk * rb
    hi = nxt - blk * rb
    changes = (blk[1:] != blk[:-1]).astype(I32)
    first = jnp.concatenate([jnp.ones((1,), I32), changes])
    last = jnp.concatenate([changes, jnp.ones((1,), I32)])
    newe = jnp.concatenate([jnp.ones((1,), I32), (ex[1:] != ex[:-1]).astype(I32)])
    return blk, ex, lo, hi, first, last, newe


def _combine_kernel(dest_ref, ew_ref, x_ref, shared_ref, gt_ref, fg_ref, yb_ref,
                    o_ref, buf, routed_sc, sem, *, rows, final):
    def tile(ref, r):
        return ref.at[pl.ds(pl.multiple_of(r * TILE_CHUNKS, TILE_CHUNKS), TILE_CHUNKS)]

    def row_copy(r, k):
        return pltpu.make_async_copy(tile(yb_ref, dest_ref[r * TOP_K + k]), tile(buf.at[k], r), sem)

    def issue(r, c):
        for k in range(TOP_K):
            row_copy(r, k).start()
        return c

    def drain(r, c):
        for k in range(TOP_K):
            row_copy(r, k).wait()
        return c

    def weigh(r, c):
        r0 = pl.multiple_of(r * TILE_CHUNKS, TILE_CHUNKS)
        acc = ew_ref[r * TOP_K] * buf[0, pl.ds(r0, TILE_CHUNKS), :]
        for k in range(1, TOP_K):
            acc = acc + ew_ref[r * TOP_K + k] * buf[k, pl.ds(r0, TILE_CHUNKS), :]
        routed_sc[pl.ds(r0, TILE_CHUNKS), :] = acc
        return c

    lax.fori_loop(0, rows, issue, 0)
    lax.fori_loop(0, rows, drain, 0)
    lax.fori_loop(0, rows, weigh, 0)
    routed = _from_token_tiles(routed_sc, rows)
    x = x_ref[...] + gt_ref[0] * (routed + shared_ref[...])
    if final:
        x = _rms(x, fg_ref[...])
    o_ref[...] = x


def _combine(grp, dest, ew, x, shared, mod, fg, yb, final):
    d = x.shape[-1]
    rows = grp.rows
    return pl.pallas_call(
        functools.partial(_combine_kernel, rows=rows, final=final),
        grid=(grp.n_blocks,),
        in_specs=[pl.BlockSpec((rows * TOP_K,), lambda i: (i,), memory_space=pltpu.SMEM),
                  pl.BlockSpec((rows * TOP_K,), lambda i: (i,), memory_space=pltpu.SMEM),
                  grp.row_spec(d), grp.row_spec(d), grp.mod_spec(2, d), _const_spec((1, d)),
                  pl.BlockSpec(memory_space=pl.ANY)],
        out_specs=grp.row_spec(d),
        out_shape=jax.ShapeDtypeStruct(x.shape, F32),
        scratch_shapes=[pltpu.VMEM((TOP_K, rows * TILE_CHUNKS, LANES), F32),
                        pltpu.VMEM((rows * TILE_CHUNKS, LANES), F32), pltpu.SemaphoreType.DMA(())],
        compiler_params=_params("arbitrary"),
        name="combine",
    )(dest, ew, x, shared, mod, fg.reshape(1, d), yb)


def _proj_kernel(*refs, q_scale, transposed):
    if transposed:
        (x_ref, ksh_ref, ksc_ref, kg_ref, wkv_ref, wkvt_ref, bsh_ref, bsc_ref, bg_ref, wq_ref,
         k_ref, v_ref, kb_ref, vb_ref, q_ref) = refs
    else:
        (x_ref, ksh_ref, ksc_ref, kg_ref, wkv_ref, bsh_ref, bsc_ref, bg_ref, wq_ref,
         k_ref, v_ref, q_ref) = refs
    x = x_ref[...]
    kvn = _ada(x, kg_ref[...], ksh_ref[0], ksc_ref[0]).astype(BF16)
    kv = _dot(kvn, wkv_ref[...])
    half = kv.shape[-1] // 2
    if transposed:
        kvt = _dot_nt(wkvt_ref[...], kvn)
        k_ref[0] = kvt[:half]
        v_ref[0] = kvt[half:]
        kb_ref[...] = kv[:, :half].astype(BF16)
        vb_ref[...] = kv[:, half:].astype(BF16)
    else:
        k_ref[...] = kv[:, :half]
        v_ref[...] = kv[:, half:]
    hn = _ada(x, bg_ref[...], bsh_ref[0], bsc_ref[0])
    q_ref[...] = (_dot(hn.astype(BF16), wq_ref[...]) * q_scale).astype(BF16)


def _proj(grp, x, kv_mod, kv_g, w_kv, b_mod, b_g, w_q, q_scale, n_seq=None):
    d = x.shape[-1]
    n = grp.n_rows
    hd = w_q.shape[-1]
    transposed = n_seq is not None
    ins = [x, kv_mod, kv_mod, kv_g.reshape(1, d), w_kv]
    in_specs = [grp.row_spec(d), grp.mod_spec(0, d), grp.mod_spec(1, d), _const_spec((1, d)),
                _const_spec(w_kv.shape)]
    if transposed:
        ins.append(w_kv.T)
        in_specs.append(_const_spec((w_kv.shape[1], w_kv.shape[0])))
        bps = grp.blocks_per_seq
        t_spec = pl.BlockSpec((1, hd, grp.rows), lambda i: (i // bps, 0, i % bps))
        out_specs = [t_spec, t_spec, grp.row_spec(hd), grp.row_spec(hd), grp.row_spec(hd)]
        t_shape = jax.ShapeDtypeStruct((n_seq, hd, n // n_seq), F32)
        out_shape = [t_shape, t_shape] + [jax.ShapeDtypeStruct((n, hd), BF16)] * 3
    else:
        out_specs = [grp.row_spec(hd)] * 3
        out_shape = [jax.ShapeDtypeStruct((n, hd), F32), jax.ShapeDtypeStruct((n, hd), F32),
                     jax.ShapeDtypeStruct((n, hd), BF16)]
    ins += [b_mod, b_mod, b_g.reshape(1, d), w_q]
    in_specs += [grp.mod_spec(0, d), grp.mod_spec(1, d), _const_spec((1, d)), _const_spec(w_q.shape)]
    return pl.pallas_call(
        functools.partial(_proj_kernel, q_scale=q_scale, transposed=transposed),
        grid=(grp.n_blocks,),
        in_specs=in_specs, out_specs=out_specs, out_shape=out_shape,
        compiler_params=_params("arbitrary"),
        name="kv_q_projection",
    )(*ins)


def _later_key_matrix(tk, n_heads=1):
    size = n_heads * tk
    j = lax.broadcasted_iota(I32, (size, size), 0)
    s = lax.broadcasted_iota(I32, (size, size), 1)
    later = jnp.where(j % tk > s % tk, 1.0, 0.0)
    return jnp.where(j // tk == s // tk, later, 0.0).astype(BF16)


LOG2E = 1.4426950408889634
SIGN_BIT = 0x80000000


def _sb_logs(z2, tri, mask):
    neg_abs = lax.bitcast_convert_type(lax.bitcast_convert_type(z2, jnp.uint32) | jnp.uint32(SIGN_BIT), F32)
    neg_l1m = jnp.maximum(z2, 0.0) + jnp.log2(1.0 + jnp.exp2(neg_abs))
    log_b = z2 - neg_l1m
    if mask is not None:
        neg_l1m = jnp.where(mask, neg_l1m, 0.0)
    return log_b, _dot(neg_l1m.astype(BF16), tri), neg_l1m


def _sb_weights(log_b, tail, carry, mask):
    w = jnp.exp2(log_b - tail - carry)
    if mask is not None:
        w = jnp.where(mask, w, 0.0)
    return w


def _attn_prompt_kernel(bias_ref, q_ref, k_ref, v_ref, o_ref, acc_sc, z_sc, lb_sc, tail_sc, *, tq, tk):
    hp = pl.program_id(1)
    qi = pl.program_id(2)
    q = q_ref[0]
    half = q.shape[-1] // 2
    tri = _later_key_matrix(tk, 2)
    pair_lane = lax.broadcasted_iota(I32, (tq, 2 * tk), 1)
    q_pos = qi * tq + lax.broadcasted_iota(I32, (tq, 2 * tk), 0)
    k_off = pair_lane % tk
    q_pos_head = qi * tq + lax.broadcasted_iota(I32, (tq, tk), 0)
    k_off_head = lax.broadcasted_iota(I32, (tq, tk), 1)
    n_diag = tq // tk
    assert n_diag == 2, "the block pipeline below alternates two slots per loop trip"
    bias2 = jnp.where(pair_lane < tk, bias_ref[hp * 2] * LOG2E, bias_ref[hp * 2 + 1] * LOG2E)
    first_head = lax.broadcasted_iota(I32, (tk, q.shape[-1]), 1) < half
    acc_sc[...] = jnp.zeros(acc_sc.shape, F32)

    def split_heads(a):
        zero = jnp.zeros_like(a)
        return jnp.concatenate([jnp.where(first_head, a, zero), jnp.where(first_head, zero, a)], axis=0)

    def keys(ref, j):
        return split_heads(ref[0, pl.ds(pl.multiple_of(j * tk, tk), tk), :])

    def logits(j):
        return _dot_nt(q, keys(k_ref, j))

    def logs(z_pair, mask):
        log_b, tail, neg_l1m = _sb_logs(z_pair + bias2, tri, mask)
        sums = tuple(jnp.sum(neg_l1m[:, hh * tk:(hh + 1) * tk], axis=1, keepdims=True) for hh in range(2))
        return log_b, tail, sums

    def accumulate(j, log_b, tail, carries, mask):
        ws = [_sb_weights(log_b[:, hh * tk:(hh + 1) * tk], tail[:, hh * tk:(hh + 1) * tk], carries[hh],
                          mask).astype(BF16) for hh in range(2)]
        acc_sc[...] += _dot(jnp.concatenate(ws, axis=1), keys(v_ref, j))

    carries = (jnp.zeros((tq, 1), F32), jnp.zeros((tq, 1), F32))
    for dj in reversed(range(n_diag)):
        j = qi * n_diag + dj
        k0 = pl.multiple_of(j * tk, tk)
        log_b, tail, sums = logs(logits(j), k0 + k_off < q_pos)
        accumulate(j, log_b, tail, carries, k0 + k_off_head < q_pos_head)
        carries = tuple(c + s for c, s in zip(carries, sums))

    n_full = qi * n_diag
    last = jnp.maximum(n_full - 1, 0)
    z_sc[1] = logits(last)
    lb_sc[0] = jnp.full(lb_sc.shape[1:], -1e30, F32)
    tail_sc[0] = jnp.zeros(tail_sc.shape[1:], F32)

    def body(it, c):
        cur, prev = c
        for u in range(2):
            j = n_full - 1 - 2 * it - u
            z_sc[u] = logits(jnp.maximum(j - 1, 0))
            log_b, tail, sums = logs(z_sc[1 - u], None)
            accumulate(jnp.minimum(j + 1, last), lb_sc[u], tail_sc[u], prev, None)
            lb_sc[1 - u] = log_b
            tail_sc[1 - u] = tail
            prev = cur
            cur = tuple(a + s for a, s in zip(cur, sums))
        return cur, prev

    _, prev = lax.fori_loop(0, qi, body, (carries, carries))
    accumulate(0, lb_sc[0], tail_sc[0], prev, None)
    o_ref[0] = acc_sc[...].astype(o_ref.dtype)


def _attn_prompt(q, k, v, bias):
    b, t, hd = q.shape
    tq, tk = ATTN_Q_ROWS, ATTN_K_ROWS
    pair = 2 * (hd // N_HEADS)
    return pl.pallas_call(
        functools.partial(_attn_prompt_kernel, tq=tq, tk=tk),
        grid=(b, hd // pair, t // tq),
        in_specs=[pl.BlockSpec(memory_space=pltpu.SMEM),
                  pl.BlockSpec((1, tq, pair), lambda bi, hp, qi: (bi, qi, hp)),
                  pl.BlockSpec((1, t, pair), lambda bi, hp, qi: (bi, 0, hp)),
                  pl.BlockSpec((1, t, pair), lambda bi, hp, qi: (bi, 0, hp))],
        out_specs=pl.BlockSpec((1, tq, pair), lambda bi, hp, qi: (bi, qi, hp)),
        out_shape=jax.ShapeDtypeStruct((b, t, hd), BF16),
        scratch_shapes=[pltpu.VMEM((tq, pair), F32)] + [pltpu.VMEM((2, tq, 2 * tk), F32)] * 3,
        compiler_params=_params("arbitrary", "arbitrary", "arbitrary"),
        name="sb_attention_prompt",
    )(bias, q, k, v)


def _attn_sample_kernel(pt_ref, bias_ref, q_ref, kn_ref, vn_ref, *rest, page, q_pad, ppg):
    del pt_ref
    kc_refs, vc_refs = rest[:ppg], rest[ppg:2 * ppg]
    o_ref, acc_sc, carry_sc = rest[2 * ppg:]
    j = pl.program_id(1)
    heads = q_ref.shape[1]
    rows = heads * q_pad
    tri = _later_key_matrix(page)

    @pl.when(j == 0)
    def _():
        acc_sc[...] = jnp.zeros(acc_sc.shape, F32)
        carry_sc[...] = jnp.zeros(carry_sc.shape, F32)

    def process(k_ref, v_ref, masked):
        k3 = k_ref[0].astype(BF16)
        v3 = v_ref[0].astype(BF16)
        z3 = lax.dot_general(q_ref[0], k3, (((2,), (1,)), ((0,), (0,))), preferred_element_type=F32)
        z2 = z3.reshape(rows, page) + bias_ref[...]
        mask = None
        if masked:
            qidx = lax.broadcasted_iota(I32, (rows, page), 0) % q_pad
            mask = lax.broadcasted_iota(I32, (rows, page), 1) < qidx
        carry = carry_sc[...][:, 0:1]
        log_b, tail, neg_l1m = _sb_logs(z2, tri, mask)
        w = _sb_weights(log_b, tail, carry, mask)
        carry_sc[...] = jnp.broadcast_to(carry + jnp.sum(neg_l1m, axis=1, keepdims=True), carry_sc.shape)
        w3 = w.reshape(heads, q_pad, page).astype(BF16)
        acc_sc[...] += lax.dot_general(w3, v3, (((2,), (2,)), ((0,), (0,))), preferred_element_type=F32)

    @pl.when(j == 0)
    def _():
        process(kn_ref, vn_ref, True)

    @pl.when(j > 0)
    def _():
        for p in range(ppg):
            process(kc_refs[p], vc_refs[p], False)

    @pl.when(j == pl.num_programs(1) - 1)
    def _():
        o_ref[0] = acc_sc[...]


def _attn_sample(page_table, bias_rows, q3, k_new, v_new, cache_kt, cache_vt, ppg):
    bs, heads, q_pad, dh = q3.shape
    page = cache_kt.shape[-1]
    n_pages = page_table.shape[1]
    rows = heads * q_pad
    assert n_pages % ppg == 0

    def cache_spec(p):
        def index(b, j, pt):
            logical = n_pages - 1 - ((jnp.maximum(j, 1) - 1) * ppg + p)
            return (pt[b, logical], 0, 0, 0)
        return pl.BlockSpec((1, heads, dh, page), index)

    new_spec = pl.BlockSpec((1, heads, dh, page), lambda b, j, pt: (b, 0, 0, 0))
    q_spec = pl.BlockSpec((1, heads, q_pad, dh), lambda b, j, pt: (b, 0, 0, 0))
    grid_spec = pltpu.PrefetchScalarGridSpec(
        num_scalar_prefetch=1,
        grid=(bs, n_pages // ppg + 1),
        in_specs=[pl.BlockSpec((rows, page), lambda b, j, pt: (0, 0)), q_spec, new_spec, new_spec]
                 + [cache_spec(p) for p in range(ppg)] * 2,
        out_specs=q_spec,
        scratch_shapes=[pltpu.VMEM((heads, q_pad, dh), F32), pltpu.VMEM((rows, LANES), F32)],
    )
    return pl.pallas_call(
        functools.partial(_attn_sample_kernel, page=page, q_pad=q_pad, ppg=ppg),
        grid_spec=grid_spec,
        out_shape=jax.ShapeDtypeStruct((bs, heads, q_pad, dh), F32),
        compiler_params=_params("arbitrary", "arbitrary"),
        name="sb_attention_sample",
    )(page_table, bias_rows, q3, k_new, v_new, *([cache_kt] * ppg), *([cache_vt] * ppg))


def _permute_experts(a):
    per = N_EXPERTS // N_EXPERT_GROUPS
    return a.reshape(N_EXPERT_GROUPS, per, *a.shape[1:]).swapaxes(0, 1).reshape(a.shape)


def _moe_layer(groups, cgroups, xs_rows, mods, pres, g, w_router, r_bias, w_gu, w_down, ws_gu, ws_down,
               fg, final):
    d = xs_rows[0].shape[-1]
    w_rt = _permute_experts(w_router.T)
    rbias = jnp.broadcast_to(_permute_experts(r_bias.reshape(N_EXPERTS, 1)), (N_EXPERTS, LANES))
    cnt = jnp.zeros((N_EXPERTS, LANES), F32)
    routed = []
    x_new = []
    for grp, x, mod, pre in zip(groups, xs_rows, mods, pres):
        outs = _router(grp, x, mod, g, w_rt, rbias, cnt, ws_gu, ws_down, pre)
        if pre is not None:
            x, outs = outs[0], outs[1:]
        hn, shared, e_t, p_t, w_t, cnt = outs
        routed.append((hn, shared, e_t, p_t, w_t))
        x_new.append(x)
    per = N_EXPERTS // N_EXPERT_GROUPS
    counts = cnt[:, 0].astype(I32).reshape(per, N_EXPERT_GROUPS).T.reshape(N_EXPERTS)
    start = jnp.cumsum(counts) - counts
    m = sum(grp.n_rows for grp in groups) * TOP_K
    steps = _expert_steps(counts, m)
    xs = None
    dests = []
    expert_ids = jnp.arange(N_EXPERTS, dtype=I32)[:, None, None]
    for grp, (hn, shared, e_t, p_t, w_t) in zip(groups, routed):
        start_sel = jnp.sum(jnp.where(e_t[None] == expert_ids, start[:, None, None], 0), axis=0)
        dest = (start_sel + p_t).T.reshape(-1)
        dests.append(dest)
        xs = _dispatch(dest, hn, xs, min(DISPATCH_ROWS, grp.n_rows), m)
    yb = _experts(steps, xs, w_gu, w_down)
    outs = []
    for cgrp, x, mod, dest, (hn, shared, e_t, p_t, w_t) in zip(cgroups, x_new, mods, dests, routed):
        outs.append(_combine(cgrp, dest, w_t.T.reshape(-1), x, shared, mod, fg, yb, final))
    return outs


def kernel(x_prompt, x_sample, c_prompt, c_sample, state_conv, state_h, cache_k, cache_v, page_table, a_norm_g, a_mod_w, a_mod_b, a_w_in, a_conv_w, a_conv_b, a_w_gate_r, a_b_gate_r, a_w_gate_i, a_b_gate_i, a_lambda, a_w_out, kv_norm_g, kv_mod_w, kv_mod_b, w_kv, b_norm_g, b_mod_w, b_mod_b, b_w_q, b_sb_bias, b_w_o, m_norm_g, m_mod_w, m_mod_b, m_w_router, m_router_bias, m_w_gate_up, m_w_down, m_ws_gate_up, m_ws_down, final_norm_g):
    bp, t, d = x_prompt.shape
    bs, ts, _ = x_sample.shape
    depth = m_norm_g.shape[0]
    n_a = a_norm_g.shape[0]
    d_rnn = a_w_out.shape[1]
    heads = N_HEADS
    dh = b_w_q.shape[-1] // heads
    page = cache_k.shape[1]
    np_rows = bp * t
    ns_rows = bs * ts
    assert n_a == 1 and depth == 2, "layer pattern of this step: one self-decoder, one cross-decoder layer"

    n_c = bp + bs
    c_rows = -(-n_c // SUBLANES) * SUBLANES
    c_all = jnp.zeros((c_rows, d), F32).at[:bp].set(c_prompt).at[bp:n_c].set(c_sample)

    def mods(w, b):
        m = _modulation(c_all, w, b)
        mp = m[:bp].reshape(bp, 1, -1)
        ms = jnp.tile(m[bp:n_c], (ts, 1)).reshape(1, ns_rows, -1)
        return mp, ms

    a_mod = mods(a_mod_w[0], a_mod_b[0])
    kv_mod = mods(kv_mod_w, kv_mod_b)
    b_mod = mods(b_mod_w[0], b_mod_b[0])
    m_mod = [mods(m_mod_w[l], m_mod_b[l]) for l in range(depth)]

    bf = lambda w: w.astype(BF16)

    rec_w = (a_norm_g[0], bf(a_w_in[0]), a_conv_w[0], a_conv_b[0], bf(a_w_gate_r[0]), a_b_gate_r[0],
             bf(a_w_gate_i[0]), a_b_gate_i[0], a_lambda[0], bf(a_w_out[0]))
    hist_p = jnp.zeros((bp, CONV_W - 1, d_rnn), F32)
    h0_p = jnp.zeros((bp, 1, d_rnn), F32)
    x1_p, conv_p, h_p = _recurrent(x_prompt, a_mod[0], hist_p, h0_p, *rec_w,
                                   stride=1, steps=REC_ROWS, mod_rows=1)
    xs_tm = x_sample.swapaxes(0, 1).reshape(1, ns_rows, d)
    hist_s = state_conv[0].swapaxes(0, 1).reshape(1, (CONV_W - 1) * bs, d_rnn)
    h0_s = state_h[0].reshape(1, bs, d_rnn)
    x1_s, conv_s, h_s = _recurrent(xs_tm, a_mod[1], hist_s, h0_s, *rec_w,
                                   stride=bs, steps=ts, mod_rows=ns_rows)

    grp_p = _Group(np_rows, ROUTER_ROWS, t // ROUTER_ROWS, 1)
    grp_s = _Group(ns_rows, ns_rows, 1, ns_rows)
    groups = [grp_p, grp_s]
    cgroups = [_Group(np_rows, COMBINE_ROWS, t // COMBINE_ROWS, 1), grp_s]

    x2_p, x2_s = _moe_layer(groups, cgroups, [x1_p.reshape(np_rows, d), x1_s.reshape(ns_rows, d)],
                            [m_mod[0][0], m_mod[0][1]], [None, None], m_norm_g[0], m_w_router[0],
                            m_router_bias[0], m_w_gate_up[0], m_w_down[0], bf(m_ws_gate_up[0]),
                            bf(m_ws_down[0]), final_norm_g, False)

    q_scale = float(dh) ** -0.5 * LOG2E
    w_kv_b, w_q_b = bf(w_kv), bf(b_w_q[0])
    pgrp_p = _Group(np_rows, PROJ_ROWS, t // PROJ_ROWS, 1)
    kt_p, vt_p, kb_p, vb_p, q_p = _proj(pgrp_p, x2_p, kv_mod[0], kv_norm_g, w_kv_b, b_mod[0], b_norm_g[0],
                                        w_q_b, q_scale, n_seq=bp)
    k_s, v_s, q_s = _proj(grp_s, x2_s, kv_mod[1], kv_norm_g, w_kv_b, b_mod[1], b_norm_g[0],
                          w_q_b, q_scale)

    hd = heads * dh
    o_p = _attn_prompt(q_p.reshape(bp, t, hd), kb_p.reshape(bp, t, hd), vb_p.reshape(bp, t, hd),
                       b_sb_bias[0]).reshape(np_rows, hd)

    q_pad = SUBLANES
    q3 = jnp.zeros((bs, heads, q_pad, dh), BF16).at[:, :, :ts].set(
        q_s.reshape(ts, bs, heads, dh).transpose(1, 2, 0, 3))

    def new_keys(a):
        a = a.reshape(ts, bs, heads, dh).transpose(1, 2, 3, 0)
        return jnp.zeros((bs, heads, dh, page), F32).at[..., :ts].set(a)

    bias_rows = jnp.broadcast_to(jnp.repeat(b_sb_bias[0] * LOG2E, q_pad)[:, None], (heads * q_pad, page))
    o3 = _attn_sample(page_table, bias_rows, q3, new_keys(k_s), new_keys(v_s),
                      cache_k.transpose(0, 2, 3, 1), cache_v.transpose(0, 2, 3, 1), SAMPLE_PAGES_PER_STEP)
    o_s = o3[:, :, :ts].transpose(2, 0, 1, 3).reshape(ns_rows, hd).astype(BF16)

    w_o_b = bf(b_w_o[0])
    y_p, y_s = _moe_layer(groups, cgroups, [x2_p, x2_s], [m_mod[1][0], m_mod[1][1]],
                          [(o_p, w_o_b, b_mod[0]), (o_s, w_o_b, b_mod[1])], m_norm_g[1], m_w_router[1],
                          m_router_bias[1], m_w_gate_up[1], m_w_down[1], bf(m_ws_gate_up[1]),
                          bf(m_ws_down[1]), final_norm_g, True)

    def from_tm(a, *tail):
        return a.reshape(ts, bs, *tail).swapaxes(0, 1)

    return (y_p.reshape(bp, t, d),
            from_tm(y_s, d),
            conv_p[None],
            h_p.reshape(1, bp, d_rnn),
            kt_p.reshape(bp, heads, dh, t).transpose(0, 3, 1, 2),
            vt_p.reshape(bp, heads, dh, t).transpose(0, 3, 1, 2),
            conv_s.reshape(CONV_W - 1, bs, d_rnn).swapaxes(0, 1)[None],
            h_s.reshape(1, bs, d_rnn),
            from_tm(k_s, heads, dh),
            from_tm(v_s, heads, dh))
```

```python
import functools

import jax
import jax.numpy as jnp
from jax import lax
from jax.experimental import pallas as pl
from jax.experimental.pallas import tpu as pltpu

F32 = jnp.float32
BF16 = jnp.bfloat16
I32 = jnp.int32

CONV_W = 4
LRU_BLOCKS = 4
RGLRU_C = 8.0
N_HEADS = 16
N_EXPERTS = 64
TOP_K = 8
N_EXPERT_GROUPS = 8
TOPK_GROUPS = 4
ROUTED_SCALE = 2.5
EPS = 1e-6
NEG_INF = float("-inf")

LANES = 128
SUBLANES = 8
TILE_CHUNKS = SUBLANES
VMEM_LIMIT_BYTES = 56 * 1024 * 1024

REC_ROWS = 256
ROUTER_ROWS = 512
PROJ_ROWS = 512
DISPATCH_ROWS = 256
COMBINE_ROWS = 128
EXPERT_ROWS = 512
ATTN_Q_ROWS = 256
ATTN_K_ROWS = 128
SAMPLE_PAGES_PER_STEP = 4


def _params(*sem):
    return pltpu.CompilerParams(dimension_semantics=sem, vmem_limit_bytes=VMEM_LIMIT_BYTES)


def _dot(a, b):
    return jnp.dot(a, b, preferred_element_type=F32)


def _dot_nt(a, b, precision=None):
    return lax.dot_general(a, b, (((1,), (1,)), ((), ())), precision=precision,
                           preferred_element_type=F32)


def _sigmoid(x):
    return 1.0 / (1.0 + jnp.exp(-x))


def _silu(x):
    return x * _sigmoid(x)


def _gelu_tanh(x):
    return 0.5 * x * (1.0 + jnp.tanh(0.7978845608028654 * (x + 0.044715 * (x * x * x))))


def _rms(x, g):
    ms = jnp.mean(x * x, axis=-1, keepdims=True)
    return x * lax.rsqrt(ms + EPS) * g


def _ada(x, g, shift, scale):
    return _rms(x, g) * (1.0 + scale) + shift


def _swiglu(xb, w_gu, w_down):
    gu = _dot(xb, w_gu)
    half = gu.shape[-1] // 2
    act = _silu(gu[:, :half]) * gu[:, half:]
    return _dot(act.astype(BF16), w_down)


def _mod_kernel(c_ref, w_ref, b_ref, o_ref):
    c = c_ref[...]
    o_ref[...] = _dot(_silu(c).astype(BF16), w_ref[...].astype(BF16)) + b_ref[...]


def _modulation(c_all, w, b):
    rows, d = c_all.shape
    n_out = w.shape[1]
    tn = 1024
    return pl.pallas_call(
        _mod_kernel,
        grid=(n_out // tn,),
        in_specs=[pl.BlockSpec((rows, d), lambda j: (0, 0)),
                  pl.BlockSpec((d, tn), lambda j: (0, j)),
                  pl.BlockSpec((1, tn), lambda j: (0, j))],
        out_specs=pl.BlockSpec((rows, tn), lambda j: (0, j)),
        out_shape=jax.ShapeDtypeStruct((rows, n_out), F32),
        compiler_params=_params("arbitrary"),
        name="modulation",
    )(c_all, w, b.reshape(1, n_out))


class _Group:
    def __init__(self, n_rows, rows, blocks_per_seq, mod_rows):
        self.n_rows = n_rows
        self.rows = rows
        self.blocks_per_seq = blocks_per_seq
        self.mod_rows = mod_rows
        self.n_blocks = n_rows // rows

    def mod_spec(self, col, d):
        bps = self.blocks_per_seq
        return pl.BlockSpec((1, self.mod_rows, d), lambda i: (i // bps, 0, col))

    def row_spec(self, width):
        return pl.BlockSpec((self.rows, width), lambda i: (i, 0))


def _const_spec(shape):
    nd = len(shape)
    return pl.BlockSpec(shape, lambda *_: (0,) * nd)


def _scan_rows(a, u, stride, steps):
    row = lax.broadcasted_iota(I32, a.shape, 0)
    s = 1
    while s < steps:
        sh = s * stride
        a_p = pltpu.roll(a, sh, 0)
        u_p = pltpu.roll(u, sh, 0)
        m = row >= sh
        u = jnp.where(m, a * u_p + u, u)
        a = jnp.where(m, a * a_p, a)
        s *= 2
    return a, u


def _rec_kernel(x_ref, sh_ref, sc_ref, gt_ref, hist_ref, h0_ref, g_ref, w_in_ref, cw_ref, cb_ref,
                wr_ref, br_ref, wi_ref, bi_ref, lam_ref, w_out_ref,
                xo_ref, conv_ref, hl_ref, xpad_sc, h_sc, *, stride, steps, hist0):
    t = pl.program_id(1)
    rows = stride * steps
    hrows = (CONV_W - 1) * stride
    d_rnn = h_sc.shape[-1]

    @pl.when(t == 0)
    def _():
        xpad_sc[pl.ds(hist0 - hrows, hrows), :] = hist_ref[0]
        h_sc[...] = h0_ref[0]

    x = x_ref[0]
    hn = _ada(x, g_ref[...], sh_ref[0], sc_ref[0])
    proj = _dot(hn.astype(BF16), w_in_ref[...])
    gate_br = proj[:, :d_rnn]
    xb = proj[:, d_rnn:]
    xpad_sc[pl.ds(hist0, rows), :] = xb
    xc = cb_ref[...] + cw_ref[0:1, :] * xpad_sc[pl.ds(hist0 - hrows, rows), :]
    for j in range(1, CONV_W):
        xc = xc + cw_ref[j:j + 1, :] * xpad_sc[pl.ds(hist0 - hrows + j * stride, rows), :]
    new_hist = xpad_sc[pl.ds(hist0 + rows - hrows, hrows), :]
    xpad_sc[pl.ds(hist0 - hrows, hrows), :] = new_hist
    conv_ref[0] = new_hist

    bw = d_rnn // LRU_BLOCKS
    r_parts, i_parts = [], []
    for n in range(LRU_BLOCKS):
        xblk = xc[:, n * bw:(n + 1) * bw].astype(BF16)
        r_parts.append(_dot(xblk, wr_ref[n]))
        i_parts.append(_dot(xblk, wi_ref[n]))
    r = _sigmoid(jnp.concatenate(r_parts, axis=-1) + br_ref[...])
    ig = _sigmoid(jnp.concatenate(i_parts, axis=-1) + bi_ref[...])
    neg_lam = -lam_ref[...]
    softplus = jnp.maximum(neg_lam, 0.0) + jnp.log1p(jnp.exp(-jnp.abs(neg_lam)))
    log_a = (-RGLRU_C * r) * softplus
    a = jnp.exp(log_a)
    u = jnp.sqrt(-jnp.tanh(log_a) * (a * a + 1.0)) * (ig * xc)
    a_cum, u_cum = _scan_rows(a, u, stride, steps)
    h_prev = h_sc[...]
    if stride == 1:
        h = a_cum * h_prev + u_cum
    else:
        h = a_cum * jnp.concatenate([h_prev] * steps, axis=0) + u_cum
    h_last = h[rows - stride:, :]
    h_sc[...] = h_last
    hl_ref[0] = h_last
    y = _dot((_gelu_tanh(gate_br) * h).astype(BF16), w_out_ref[...])
    xo_ref[0] = x + gt_ref[0] * y


def _recurrent(x3, mod, hist, h0, g, w_in, conv_w, conv_b, w_r, b_r, w_i, b_i, lam, w_out,
               *, stride, steps, mod_rows):
    nb, t_rows, d = x3.shape
    d_rnn = w_out.shape[0]
    rows = stride * steps
    nt = t_rows // rows
    hrows = (CONV_W - 1) * stride
    hist0 = -(-hrows // SUBLANES) * SUBLANES
    assert rows >= hrows and rows % SUBLANES == 0
    x_spec = pl.BlockSpec((1, rows, d), lambda b, t: (b, t, 0))

    def mspec(col):
        if mod_rows == 1:
            return pl.BlockSpec((1, 1, d), lambda b, t: (b, 0, col))
        return pl.BlockSpec((1, rows, d), lambda b, t: (b, 0, col))

    def cspec(shape):
        nd = len(shape)
        return pl.BlockSpec(shape, lambda b, t: (0,) * nd)

    kern = functools.partial(_rec_kernel, stride=stride, steps=steps, hist0=hist0)
    return pl.pallas_call(
        kern,
        grid=(nb, nt),
        in_specs=[x_spec, mspec(0), mspec(1), mspec(2),
                  pl.BlockSpec((1, hrows, d_rnn), lambda b, t: (b, 0, 0)),
                  pl.BlockSpec((1, stride, d_rnn), lambda b, t: (b, 0, 0)),
                  cspec((1, d)), cspec(w_in.shape), cspec(conv_w.shape), cspec((1, d_rnn)),
                  cspec(w_r.shape), cspec((1, d_rnn)), cspec(w_i.shape), cspec((1, d_rnn)),
                  cspec((1, d_rnn)), cspec(w_out.shape)],
        out_specs=[x_spec,
                   pl.BlockSpec((1, hrows, d_rnn), lambda b, t: (b, 0, 0)),
                   pl.BlockSpec((1, stride, d_rnn), lambda b, t: (b, 0, 0))],
        out_shape=[jax.ShapeDtypeStruct(x3.shape, F32),
                   jax.ShapeDtypeStruct((nb, hrows, d_rnn), F32),
                   jax.ShapeDtypeStruct((nb, stride, d_rnn), F32)],
        scratch_shapes=[pltpu.VMEM((hist0 + rows, d_rnn), F32), pltpu.VMEM((stride, d_rnn), F32)],
        compiler_params=_params("arbitrary", "arbitrary"),
        name="recurrent_block",
    )(x3, mod, mod, mod, hist, h0, g.reshape(1, d), w_in, conv_w, conv_b.reshape(1, d_rnn),
      w_r, b_r.reshape(1, d_rnn), w_i, b_i.reshape(1, d_rnn), lam.reshape(1, d_rnn), w_out)


def _tree(op, xs):
    xs = list(xs)
    while len(xs) > 1:
        nxt = [op(xs[i], xs[i + 1]) for i in range(0, len(xs) - 1, 2)]
        if len(xs) % 2:
            nxt.append(xs[-1])
        xs = nxt
    return xs[0]


def _to_token_tiles(ref, x):
    rows = x.shape[0]
    for c in range(TILE_CHUNKS):
        ref[pl.ds(c, rows, stride=TILE_CHUNKS), :] = x[:, c * LANES:(c + 1) * LANES]


def _from_token_tiles(ref, rows):
    return jnp.concatenate([ref[pl.ds(c, rows, stride=TILE_CHUNKS), :] for c in range(TILE_CHUNKS)], axis=1)


def _router_kernel(*refs, pre):
    if pre:
        (x_ref, o_ref, wo_ref, agt_ref, sh_ref, sc_ref, g_ref, wrt_ref, rb_ref, cnt_in_ref, wsgu_ref, wsd_ref,
         xo_ref, hn_ref, shared_ref, e_ref, p_ref, w_ref, cnt_ref, run_sc) = refs
    else:
        (x_ref, sh_ref, sc_ref, g_ref, wrt_ref, rb_ref, cnt_in_ref, wsgu_ref, wsd_ref,
         hn_ref, shared_ref, e_ref, p_ref, w_ref, cnt_ref, run_sc) = refs
    i = pl.program_id(0)
    per = N_EXPERTS // N_EXPERT_GROUPS

    @pl.when(i == 0)
    def _():
        run_sc[...] = cnt_in_ref[...]

    x = x_ref[...]
    if pre:
        x = x + agt_ref[0] * _dot(o_ref[...], wo_ref[...])
        xo_ref[...] = x
    hn = _ada(x, g_ref[...], sh_ref[0], sc_ref[0])
    _to_token_tiles(hn_ref, hn)
    shared_ref[...] = _swiglu(hn.astype(BF16), wsgu_ref[...], wsd_ref[...])
    tb = hn.shape[0]
    logits = _dot_nt(wrt_ref[...], hn, precision=lax.Precision.HIGHEST)
    scores = _sigmoid(logits)
    sel = scores + rb_ref[...][:, 0:1]
    sel_j = [sel[j * SUBLANES:(j + 1) * SUBLANES, :] for j in range(per)]
    sc_j = [scores[j * SUBLANES:(j + 1) * SUBLANES, :] for j in range(per)]
    m1 = _tree(jnp.maximum, sel_j)
    j1 = _tree(jnp.minimum, [jnp.where(sel_j[j] == m1, j, per) for j in range(per)])
    m2 = _tree(jnp.maximum, [jnp.where(j1 == j, NEG_INF, sel_j[j]) for j in range(per)])
    gs = m1 + m2
    g_iota = lax.broadcasted_iota(I32, gs.shape, 0)
    rank = jnp.zeros(gs.shape, I32)
    for dlt in range(1, N_EXPERT_GROUPS):
        other = pltpu.roll(gs, dlt, 0)
        tie = jnp.where(g_iota >= dlt, 1, 0)
        rank = rank + jnp.where(other > gs, 1, jnp.where(other == gs, tie, 0))
    gmask = rank < TOPK_GROUPS
    masked = [jnp.where(gmask, s, NEG_INF) for s in sel_j]
    eid = [g_iota * per + j for j in range(per)]
    chosen = [jnp.zeros(gs.shape, F32) for _ in range(per)]
    e_rows, s_rows = [], []
    for k in range(TOP_K):
        m = jnp.max(_tree(jnp.maximum, masked), axis=0, keepdims=True)
        cand = _tree(jnp.minimum, [jnp.where(masked[j] == m, eid[j], N_EXPERTS) for j in range(per)])
        ek = jnp.min(cand, axis=0, keepdims=True)
        sk = jnp.zeros(gs.shape, F32)
        for j in range(per):
            oh = eid[j] == ek
            masked[j] = jnp.where(oh, NEG_INF, masked[j])
            chosen[j] = jnp.where(oh, 1.0, chosen[j])
            sk = sk + jnp.where(oh, sc_j[j], 0.0)
        e_rows.append(ek)
        s_rows.append(jnp.sum(sk, axis=0, keepdims=True))
    denom = _tree(lambda p, q: p + q, s_rows)
    sel_mask = jnp.concatenate(chosen, axis=0)
    upper = (lax.broadcasted_iota(I32, (tb, tb), 0) < lax.broadcasted_iota(I32, (tb, tb), 1))
    prefix = _dot(sel_mask.astype(BF16), jnp.where(upper, 1.0, 0.0).astype(BF16)) + run_sc[...][:, 0:1]
    pre_j = [prefix[j * SUBLANES:(j + 1) * SUBLANES, :] for j in range(per)]
    for k in range(TOP_K):
        pk = jnp.zeros(gs.shape, F32)
        for j in range(per):
            pk = pk + jnp.where(eid[j] == e_rows[k], pre_j[j], 0.0)
        e_ref[pl.ds(k, 1), :] = e_rows[k]
        p_ref[pl.ds(k, 1), :] = jnp.sum(pk, axis=0, keepdims=True).astype(I32)
        w_ref[pl.ds(k, 1), :] = s_rows[k] / denom * ROUTED_SCALE
    run = run_sc[...] + jnp.sum(sel_mask, axis=1, keepdims=True)
    run_sc[...] = run
    cnt_ref[...] = run


def _router(grp, x, mod, g, w_rt, rbias, cnt_in, ws_gu, ws_down, pre=None):
    d = x.shape[-1]
    assert d == TILE_CHUNKS * LANES
    n, tb = grp.n_rows, grp.rows
    ins = [x]
    in_specs = [grp.row_spec(d)]
    if pre is not None:
        o, w_o, amod = pre
        ins += [o, w_o, amod]
        in_specs += [grp.row_spec(o.shape[-1]), _const_spec(w_o.shape), grp.mod_spec(2, d)]
    ins += [mod, mod, g.reshape(1, d), w_rt, rbias, cnt_in, ws_gu, ws_down]
    in_specs += [grp.mod_spec(0, d), grp.mod_spec(1, d), _const_spec((1, d)), _const_spec(w_rt.shape),
                 _const_spec(rbias.shape), _const_spec(cnt_in.shape), _const_spec(ws_gu.shape),
                 _const_spec(ws_down.shape)]
    tok_spec = pl.BlockSpec((TOP_K, tb), lambda i: (0, i))
    out_specs = [pl.BlockSpec((tb * TILE_CHUNKS, LANES), lambda i: (i, 0)), grp.row_spec(d),
                 tok_spec, tok_spec, tok_spec, _const_spec(cnt_in.shape)]
    out_shape = [jax.ShapeDtypeStruct((n * TILE_CHUNKS, LANES), F32), jax.ShapeDtypeStruct((n, d), F32),
                 jax.ShapeDtypeStruct((TOP_K, n), I32),
                 jax.ShapeDtypeStruct((TOP_K, n), I32), jax.ShapeDtypeStruct((TOP_K, n), F32),
                 jax.ShapeDtypeStruct(cnt_in.shape, F32)]
    if pre is not None:
        out_specs = [grp.row_spec(d)] + out_specs
        out_shape = [jax.ShapeDtypeStruct((n, d), F32)] + out_shape
    return pl.pallas_call(
        functools.partial(_router_kernel, pre=pre is not None),
        grid=(grp.n_blocks,),
        in_specs=in_specs, out_specs=out_specs, out_shape=out_shape,
        scratch_shapes=[pltpu.VMEM(cnt_in.shape, F32)],
        compiler_params=_params("arbitrary"),
        name="router",
    )(*ins)


def _dispatch_kernel(dest_ref, hn_ref, *rest, rows):
    xs_ref, sem = rest[-2:]

    def tile(ref, r):
        return ref.at[pl.ds(pl.multiple_of(r * TILE_CHUNKS, TILE_CHUNKS), TILE_CHUNKS)]

    def row_copy(r, k):
        return pltpu.make_async_copy(tile(hn_ref, r), tile(xs_ref, dest_ref[r * TOP_K + k]), sem)

    def issue(r, c):
        for k in range(TOP_K):
            row_copy(r, k).start()
        return c

    def drain(r, c):
        for k in range(TOP_K):
            row_copy(r, k).wait()
        return c

    lax.fori_loop(0, rows, issue, 0)
    lax.fori_loop(0, rows, drain, 0)


def _dispatch(dest, hn, xs, rows, m):
    n = hn.shape[0] // TILE_CHUNKS
    d = LANES
    m = m * TILE_CHUNKS
    ins = [dest, hn]
    in_specs = [pl.BlockSpec((rows * TOP_K,), lambda i: (i,), memory_space=pltpu.SMEM),
                pl.BlockSpec((rows * TILE_CHUNKS, LANES), lambda i: (i, 0))]
    aliases = {}
    if xs is not None:
        ins.append(xs)
        in_specs.append(pl.BlockSpec(memory_space=pl.ANY))
        aliases = {2: 0}
    return pl.pallas_call(
        functools.partial(_dispatch_kernel, rows=rows),
        grid=(n // rows,),
        in_specs=in_specs,
        out_specs=pl.BlockSpec(memory_space=pl.ANY),
        out_shape=jax.ShapeDtypeStruct((m, d), F32),
        scratch_shapes=[pltpu.SemaphoreType.DMA(())],
        input_output_aliases=aliases,
        compiler_params=_params("arbitrary"),
        name="dispatch",
    )(*ins)


def _experts_kernel(blk_ref, exp_ref, lo_ref, hi_ref, first_ref, last_ref, newe_ref,
                    xs_ref, pair_ref, wgu_ref, wd_ref, slots_ref, wgu_sc, wd_sc, x_sc, y_sc, out_sc, sems):
    del exp_ref
    s = pl.program_id(0)
    lo = lo_ref[s]
    hi = hi_ref[s]
    rb = x_sc.shape[0]
    blk = blk_ref[s]

    def tile(ref, r):
        return ref.at[pl.ds(pl.multiple_of(r * TILE_CHUNKS, TILE_CHUNKS), TILE_CHUNKS)]

    def row_copy(slot, r):
        return pltpu.make_async_copy(tile(out_sc.at[slot], r), tile(slots_ref, pair_ref[0, 0, r]),
                                     sems.at[slot])

    def drain(slot):
        def body(r, c):
            row_copy(slot, r).wait()
            return c
        lax.fori_loop(0, rb, body, 0, unroll=8)

    @pl.when(newe_ref[s] == 1)
    def _():
        wgu_sc[...] = wgu_ref[0].astype(BF16)
        wd_sc[...] = wd_ref[0].astype(BF16)

    @pl.when(first_ref[s] == 1)
    def _():
        x_sc[...] = _from_token_tiles(xs_ref, rb).astype(BF16)
        y_sc[...] = jnp.zeros(y_sc.shape, F32)

    @pl.when(hi > lo)
    def _():
        gu = _dot(x_sc[...], wgu_sc[...])
        half = gu.shape[-1] // 2
        act = _silu(gu[:, :half]) * gu[:, half:]
        row = lax.broadcasted_iota(I32, act.shape, 0)
        act = jnp.where((row >= lo) & (row < hi), act, 0.0)
        y_sc[...] += _dot(act.astype(BF16), wd_sc[...])

    @pl.when(last_ref[s] == 1)
    def _():
        for slot in range(2):
            @pl.when(blk % 2 == slot)
            def _():
                @pl.when(blk >= 2)
                def _():
                    drain(slot)

                _to_token_tiles(out_sc.at[slot], y_sc[...])

                def issue(r, c):
                    row_copy(slot, r).start()
                    return c
                lax.fori_loop(0, rb, issue, 0, unroll=8)

    @pl.when(s == pl.num_programs(0) - 1)
    def _():
        for slot in range(2):
            @pl.when((blk % 2 == slot) | (blk >= 1))
            def _():
                drain(slot)


def _experts(steps, xs, row_pair, w_gu, w_down):
    m = xs.shape[0] // TILE_CHUNKS
    d = TILE_CHUNKS * LANES
    rb = EXPERT_ROWS
    n_steps = steps[0].shape[0]
    de2 = w_gu.shape[-1]
    grid_spec = pltpu.PrefetchScalarGridSpec(
        num_scalar_prefetch=7,
        grid=(n_steps,),
        in_specs=[pl.BlockSpec((rb * TILE_CHUNKS, LANES), lambda s, blk, ex, *_: (blk[s], 0)),
                  pl.BlockSpec((1, 1, rb), lambda s, blk, ex, *_: (blk[s], 0, 0), memory_space=pltpu.SMEM),
                  pl.BlockSpec((1, d, de2), lambda s, blk, ex, *_: (ex[s], 0, 0)),
                  pl.BlockSpec((1, de2 // 2, d), lambda s, blk, ex, *_: (ex[s], 0, 0))],
        out_specs=pl.BlockSpec(memory_space=pl.ANY),
        scratch_shapes=[pltpu.VMEM((d, de2), BF16), pltpu.VMEM((de2 // 2, d), BF16),
                        pltpu.VMEM((rb, d), BF16), pltpu.VMEM((rb, d), F32),
                        pltpu.VMEM((2, rb * TILE_CHUNKS, LANES), F32), pltpu.SemaphoreType.DMA((2,))],
    )
    return pl.pallas_call(
        _experts_kernel,
        grid_spec=grid_spec,
        out_shape=jax.ShapeDtypeStruct((m * TILE_CHUNKS, LANES), F32),
        compiler_params=_params("arbitrary"),
        name="routed_experts",
    )(*steps, xs, row_pair.reshape(m // rb, 1, rb), w_gu, w_down)


def _expert_steps(counts, m):
    rb = EXPERT_ROWS
    nb = m // rb
    start = jnp.cumsum(counts) - counts
    end = start + counts
    cuts = jnp.sort(jnp.concatenate([jnp.arange(nb, dtype=I32) * rb, start.astype(I32)]))
    nxt = jnp.concatenate([cuts[1:], jnp.array([m], I32)])
    blk = jnp.minimum(cuts // rb, nb - 1)
    ex = jnp.minimum(jnp.sum((end[None, :] <= cuts[:, None]).astype(I32), axis=1), N_EXPERTS - 1)
    lo = cuts - blk * rb
    hi = nxt - blk * rb
    changes = (blk[1:] != blk[:-1]).astype(I32)
    first = jnp.concatenate([jnp.ones((1,), I32), changes])
    last = jnp.concatenate([changes, jnp.ones((1,), I32)])
    newe = jnp.concatenate([jnp.ones((1,), I32), (ex[1:] != ex[:-1]).astype(I32)])
    return blk, ex, lo, hi, first, last, newe


def _combine_kernel(ew_ref, x_ref, shared_ref, gt_ref, fg_ref, slots_ref, o_ref, routed_sc, *, rows, final):
    def weigh(r, c):
        acc = None
        for k in range(TOP_K):
            p0 = pl.multiple_of((r * TOP_K + k) * TILE_CHUNKS, TILE_CHUNKS)
            term = ew_ref[r * TOP_K + k] * slots_ref[pl.ds(p0, TILE_CHUNKS), :]
            acc = term if acc is None else acc + term
        routed_sc[pl.ds(pl.multiple_of(r * TILE_CHUNKS, TILE_CHUNKS), TILE_CHUNKS), :] = acc
        return c

    lax.fori_loop(0, rows, weigh, 0, unroll=2)
    routed = _from_token_tiles(routed_sc, rows)
    x = x_ref[...] + gt_ref[0] * (routed + shared_ref[...])
    if final:
        x = _rms(x, fg_ref[...])
    o_ref[...] = x


def _combine(grp, row0, ew, x, shared, mod, fg, slots, final):
    d = x.shape[-1]
    rows = grp.rows
    blk0 = row0 // rows
    assert row0 % rows == 0
    return pl.pallas_call(
        functools.partial(_combine_kernel, rows=rows, final=final),
        grid=(grp.n_blocks,),
        in_specs=[pl.BlockSpec((rows * TOP_K,), lambda i: (i,), memory_space=pltpu.SMEM),
                  grp.row_spec(d), grp.row_spec(d), grp.mod_spec(2, d), _const_spec((1, d)),
                  pl.BlockSpec((rows * TOP_K * TILE_CHUNKS, LANES), lambda i: (i + blk0, 0))],
        out_specs=grp.row_spec(d),
        out_shape=jax.ShapeDtypeStruct(x.shape, F32),
        scratch_shapes=[pltpu.VMEM((rows * TILE_CHUNKS, LANES), F32)],
        compiler_params=_params("arbitrary"),
        name="combine",
    )(ew, x, shared, mod, fg.reshape(1, d), slots)


def _proj_kernel(*refs, q_scale, transposed):
    if transposed:
        (x_ref, ksh_ref, ksc_ref, kg_ref, wkv_ref, wkvt_ref, bsh_ref, bsc_ref, bg_ref, wq_ref,
         k_ref, v_ref, kb_ref, vb_ref, q_ref) = refs
    else:
        (x_ref, ksh_ref, ksc_ref, kg_ref, wkv_ref, bsh_ref, bsc_ref, bg_ref, wq_ref,
         k_ref, v_ref, q_ref) = refs
    x = x_ref[...]
    kvn = _ada(x, kg_ref[...], ksh_ref[0], ksc_ref[0]).astype(BF16)
    kv = _dot(kvn, wkv_ref[...])
    half = kv.shape[-1] // 2
    if transposed:
        kvt = _dot_nt(wkvt_ref[...], kvn)
        k_ref[0] = kvt[:half]
        v_ref[0] = kvt[half:]
        kb_ref[...] = kv[:, :half].astype(BF16)
        vb_ref[...] = kv[:, half:].astype(BF16)
    else:
        k_ref[...] = kv[:, :half]
        v_ref[...] = kv[:, half:]
    hn = _ada(x, bg_ref[...], bsh_ref[0], bsc_ref[0])
    q_ref[...] = (_dot(hn.astype(BF16), wq_ref[...]) * q_scale).astype(BF16)


def _proj(grp, x, kv_mod, kv_g, w_kv, b_mod, b_g, w_q, q_scale, n_seq=None):
    d = x.shape[-1]
    n = grp.n_rows
    hd = w_q.shape[-1]
    transposed = n_seq is not None
    ins = [x, kv_mod, kv_mod, kv_g.reshape(1, d), w_kv]
    in_specs = [grp.row_spec(d), grp.mod_spec(0, d), grp.mod_spec(1, d), _const_spec((1, d)),
                _const_spec(w_kv.shape)]
    if transposed:
        ins.append(w_kv.T)
        in_specs.append(_const_spec((w_kv.shape[1], w_kv.shape[0])))
        bps = grp.blocks_per_seq
        t_spec = pl.BlockSpec((1, hd, grp.rows), lambda i: (i // bps, 0, i % bps))
        out_specs = [t_spec, t_spec, grp.row_spec(hd), grp.row_spec(hd), grp.row_spec(hd)]
        t_shape = jax.ShapeDtypeStruct((n_seq, hd, n // n_seq), F32)
        out_shape = [t_shape, t_shape] + [jax.ShapeDtypeStruct((n, hd), BF16)] * 3
    else:
        out_specs = [grp.row_spec(hd)] * 3
        out_shape = [jax.ShapeDtypeStruct((n, hd), F32), jax.ShapeDtypeStruct((n, hd), F32),
                     jax.ShapeDtypeStruct((n, hd), BF16)]
    ins += [b_mod, b_mod, b_g.reshape(1, d), w_q]
    in_specs += [grp.mod_spec(0, d), grp.mod_spec(1, d), _const_spec((1, d)), _const_spec(w_q.shape)]
    return pl.pallas_call(
        functools.partial(_proj_kernel, q_scale=q_scale, transposed=transposed),
        grid=(grp.n_blocks,),
        in_specs=in_specs, out_specs=out_specs, out_shape=out_shape,
        compiler_params=_params("arbitrary"),
        name="kv_q_projection",
    )(*ins)


def _later_key_matrix(tk, n_heads=1):
    size = n_heads * tk
    j = lax.broadcasted_iota(I32, (size, size), 0)
    s = lax.broadcasted_iota(I32, (size, size), 1)
    later = jnp.where(j % tk > s % tk, 1.0, 0.0)
    return jnp.where(j // tk == s // tk, later, 0.0).astype(BF16)


LOG2E = 1.4426950408889634
SIGN_BIT = 0x80000000


def _sb_logs(z2, tri, mask):
    neg_abs = lax.bitcast_convert_type(lax.bitcast_convert_type(z2, jnp.uint32) | jnp.uint32(SIGN_BIT), F32)
    neg_l1m = jnp.maximum(z2, 0.0) + jnp.log2(1.0 + jnp.exp2(neg_abs))
    log_b = z2 - neg_l1m
    if mask is not None:
        neg_l1m = jnp.where(mask, neg_l1m, 0.0)
    return log_b, _dot(neg_l1m.astype(BF16), tri), neg_l1m


def _sb_weights(log_b, tail, carry, mask):
    w = jnp.exp2(log_b - tail - carry)
    if mask is not None:
        w = jnp.where(mask, w, 0.0)
    return w


def _attn_prompt_kernel(bias_ref, q_ref, k_ref, v_ref, o_ref, acc_sc, z_sc, lb_sc, tail_sc, *, tq, tk):
    hp = pl.program_id(1)
    qi = pl.program_id(2)
    q = q_ref[0]
    half = q.shape[-1] // 2
    tri = _later_key_matrix(tk, 2)
    pair_lane = lax.broadcasted_iota(I32, (tq, 2 * tk), 1)
    q_pos = qi * tq + lax.broadcasted_iota(I32, (tq, 2 * tk), 0)
    k_off = pair_lane % tk
    q_pos_head = qi * tq + lax.broadcasted_iota(I32, (tq, tk), 0)
    k_off_head = lax.broadcasted_iota(I32, (tq, tk), 1)
    n_diag = tq // tk
    assert n_diag == 2, "the block pipeline below alternates two slots per loop trip"
    bias2 = jnp.where(pair_lane < tk, bias_ref[hp * 2] * LOG2E, bias_ref[hp * 2 + 1] * LOG2E)
    first_head = lax.broadcasted_iota(I32, (tk, q.shape[-1]), 1) < half
    acc_sc[...] = jnp.zeros(acc_sc.shape, F32)

    def split_heads(a):
        zero = jnp.zeros_like(a)
        return jnp.concatenate([jnp.where(first_head, a, zero), jnp.where(first_head, zero, a)], axis=0)

    def keys(ref, j):
        return split_heads(ref[0, pl.ds(pl.multiple_of(j * tk, tk), tk), :])

    def logits(j):
        return _dot_nt(q, keys(k_ref, j))

    def logs(z_pair, mask):
        log_b, tail, neg_l1m = _sb_logs(z_pair + bias2, tri, mask)
        sums = tuple(jnp.sum(neg_l1m[:, hh * tk:(hh + 1) * tk], axis=1, keepdims=True) for hh in range(2))
        return log_b, tail, sums

    def accumulate(j, log_b, tail, carries, mask):
        ws = [_sb_weights(log_b[:, hh * tk:(hh + 1) * tk], tail[:, hh * tk:(hh + 1) * tk], carries[hh],
                          mask).astype(BF16) for hh in range(2)]
        acc_sc[...] += _dot(jnp.concatenate(ws, axis=1), keys(v_ref, j))

    carries = (jnp.zeros((tq, 1), F32), jnp.zeros((tq, 1), F32))
    for dj in reversed(range(n_diag)):
        j = qi * n_diag + dj
        k0 = pl.multiple_of(j * tk, tk)
        log_b, tail, sums = logs(logits(j), k0 + k_off < q_pos)
        accumulate(j, log_b, tail, carries, k0 + k_off_head < q_pos_head)
        carries = tuple(c + s for c, s in zip(carries, sums))

    n_full = qi * n_diag
    last = jnp.maximum(n_full - 1, 0)
    z_sc[1] = logits(last)
    lb_sc[0] = jnp.full(lb_sc.shape[1:], -1e30, F32)
    tail_sc[0] = jnp.zeros(tail_sc.shape[1:], F32)

    def body(it, c):
        cur, prev = c
        for u in range(2):
            j = n_full - 1 - 2 * it - u
            z_sc[u] = logits(jnp.maximum(j - 1, 0))
            log_b, tail, sums = logs(z_sc[1 - u], None)
            accumulate(jnp.minimum(j + 1, last), lb_sc[u], tail_sc[u], prev, None)
            lb_sc[1 - u] = log_b
            tail_sc[1 - u] = tail
            prev = cur
            cur = tuple(a + s for a, s in zip(cur, sums))
        return cur, prev

    _, prev = lax.fori_loop(0, qi, body, (carries, carries))
    accumulate(0, lb_sc[0], tail_sc[0], prev, None)
    o_ref[0] = acc_sc[...].astype(o_ref.dtype)


def _attn_prompt(q, k, v, bias):
    b, t, hd = q.shape
    tq, tk = ATTN_Q_ROWS, ATTN_K_ROWS
    pair = 2 * (hd // N_HEADS)
    return pl.pallas_call(
        functools.partial(_attn_prompt_kernel, tq=tq, tk=tk),
        grid=(b, hd // pair, t // tq),
        in_specs=[pl.BlockSpec(memory_space=pltpu.SMEM),
                  pl.BlockSpec((1, tq, pair), lambda bi, hp, qi: (bi, qi, hp)),
                  pl.BlockSpec((1, t, pair), lambda bi, hp, qi: (bi, 0, hp)),
                  pl.BlockSpec((1, t, pair), lambda bi, hp, qi: (bi, 0, hp))],
        out_specs=pl.BlockSpec((1, tq, pair), lambda bi, hp, qi: (bi, qi, hp)),
        out_shape=jax.ShapeDtypeStruct((b, t, hd), BF16),
        scratch_shapes=[pltpu.VMEM((tq, pair), F32)] + [pltpu.VMEM((2, tq, 2 * tk), F32)] * 3,
        compiler_params=_params("arbitrary", "arbitrary", "arbitrary"),
        name="sb_attention_prompt",
    )(bias, q, k, v)


def _attn_sample_kernel(pt_ref, bias_ref, q_ref, kn_ref, vn_ref, *rest, page, q_pad, ppg):
    del pt_ref
    kc_refs, vc_refs = rest[:ppg], rest[ppg:2 * ppg]
    o_ref, acc_sc, carry_sc = rest[2 * ppg:]
    j = pl.program_id(1)
    heads = q_ref.shape[1]
    rows = heads * q_pad
    tri = _later_key_matrix(page)

    @pl.when(j == 0)
    def _():
        acc_sc[...] = jnp.zeros(acc_sc.shape, F32)
        carry_sc[...] = jnp.zeros(carry_sc.shape, F32)

    def process(k_ref, v_ref, masked):
        k3 = k_ref[0].astype(BF16)
        v3 = v_ref[0].astype(BF16)
        z3 = lax.dot_general(q_ref[0], k3, (((2,), (1,)), ((0,), (0,))), preferred_element_type=F32)
        z2 = z3.reshape(rows, page) + bias_ref[...]
        mask = None
        if masked:
            qidx = lax.broadcasted_iota(I32, (rows, page), 0) % q_pad
            mask = lax.broadcasted_iota(I32, (rows, page), 1) < qidx
        carry = carry_sc[...][:, 0:1]
        log_b, tail, neg_l1m = _sb_logs(z2, tri, mask)
        w = _sb_weights(log_b, tail, carry, mask)
        carry_sc[...] = jnp.broadcast_to(carry + jnp.sum(neg_l1m, axis=1, keepdims=True), carry_sc.shape)
        w3 = w.reshape(heads, q_pad, page).astype(BF16)
        acc_sc[...] += lax.dot_general(w3, v3, (((2,), (2,)), ((0,), (0,))), preferred_element_type=F32)

    @pl.when(j == 0)
    def _():
        process(kn_ref, vn_ref, True)

    @pl.when(j > 0)
    def _():
        for p in range(ppg):
            process(kc_refs[p], vc_refs[p], False)

    @pl.when(j == pl.num_programs(1) - 1)
    def _():
        o_ref[0] = acc_sc[...]


def _attn_sample(page_table, bias_rows, q3, k_new, v_new, cache_kt, cache_vt, ppg):
    bs, heads, q_pad, dh = q3.shape
    page = cache_kt.shape[-1]
    n_pages = page_table.shape[1]
    rows = heads * q_pad
    assert n_pages % ppg == 0

    def cache_spec(p):
        def index(b, j, pt):
            logical = n_pages - 1 - ((jnp.maximum(j, 1) - 1) * ppg + p)
            return (pt[b, logical], 0, 0, 0)
        return pl.BlockSpec((1, heads, dh, page), index)

    new_spec = pl.BlockSpec((1, heads, dh, page), lambda b, j, pt: (b, 0, 0, 0))
    q_spec = pl.BlockSpec((1, heads, q_pad, dh), lambda b, j, pt: (b, 0, 0, 0))
    grid_spec = pltpu.PrefetchScalarGridSpec(
        num_scalar_prefetch=1,
        grid=(bs, n_pages // ppg + 1),
        in_specs=[pl.BlockSpec((rows, page), lambda b, j, pt: (0, 0)), q_spec, new_spec, new_spec]
                 + [cache_spec(p) for p in range(ppg)] * 2,
        out_specs=q_spec,
        scratch_shapes=[pltpu.VMEM((heads, q_pad, dh), F32), pltpu.VMEM((rows, LANES), F32)],
    )
    return pl.pallas_call(
        functools.partial(_attn_sample_kernel, page=page, q_pad=q_pad, ppg=ppg),
        grid_spec=grid_spec,
        out_shape=jax.ShapeDtypeStruct((bs, heads, q_pad, dh), F32),
        compiler_params=_params("arbitrary", "arbitrary"),
        name="sb_attention_sample",
    )(page_table, bias_rows, q3, k_new, v_new, *([cache_kt] * ppg), *([cache_vt] * ppg))


def _permute_experts(a):
    per = N_EXPERTS // N_EXPERT_GROUPS
    return a.reshape(N_EXPERT_GROUPS, per, *a.shape[1:]).swapaxes(0, 1).reshape(a.shape)


def _moe_layer(groups, cgroups, xs_rows, mods, pres, g, w_router, r_bias, w_gu, w_down, ws_gu, ws_down,
               fg, final):
    d = xs_rows[0].shape[-1]
    w_rt = _permute_experts(w_router.T)
    rbias = jnp.broadcast_to(_permute_experts(r_bias.reshape(N_EXPERTS, 1)), (N_EXPERTS, LANES))
    cnt = jnp.zeros((N_EXPERTS, LANES), F32)
    routed = []
    x_new = []
    for grp, x, mod, pre in zip(groups, xs_rows, mods, pres):
        outs = _router(grp, x, mod, g, w_rt, rbias, cnt, ws_gu, ws_down, pre)
        if pre is not None:
            x, outs = outs[0], outs[1:]
        hn, shared, e_t, p_t, w_t, cnt = outs
        routed.append((hn, shared, e_t, p_t, w_t))
        x_new.append(x)
    per = N_EXPERTS // N_EXPERT_GROUPS
    counts = cnt[:, 0].astype(I32).reshape(per, N_EXPERT_GROUPS).T.reshape(N_EXPERTS)
    start = jnp.cumsum(counts) - counts
    m = sum(grp.n_rows for grp in groups) * TOP_K
    steps = _expert_steps(counts, m)
    xs = None
    dests = []
    expert_ids = jnp.arange(N_EXPERTS, dtype=I32)[:, None, None]
    for grp, (hn, shared, e_t, p_t, w_t) in zip(groups, routed):
        start_sel = jnp.sum(jnp.where(e_t[None] == expert_ids, start[:, None, None], 0), axis=0)
        dest = (start_sel + p_t).T.reshape(-1)
        dests.append(dest)
        xs = _dispatch(dest, hn, xs, min(DISPATCH_ROWS, grp.n_rows), m)
    row_pair = jnp.argsort(jnp.concatenate(dests)).astype(I32)
    slots = _experts(steps, xs, row_pair, w_gu, w_down)
    outs = []
    row0 = 0
    for cgrp, x, mod, (hn, shared, e_t, p_t, w_t) in zip(cgroups, x_new, mods, routed):
        outs.append(_combine(cgrp, row0, w_t.T.reshape(-1), x, shared, mod, fg, slots, final))
        row0 += cgrp.n_rows
    return outs


def kernel(x_prompt, x_sample, c_prompt, c_sample, state_conv, state_h, cache_k, cache_v, page_table, a_norm_g, a_mod_w, a_mod_b, a_w_in, a_conv_w, a_conv_b, a_w_gate_r, a_b_gate_r, a_w_gate_i, a_b_gate_i, a_lambda, a_w_out, kv_norm_g, kv_mod_w, kv_mod_b, w_kv, b_norm_g, b_mod_w, b_mod_b, b_w_q, b_sb_bias, b_w_o, m_norm_g, m_mod_w, m_mod_b, m_w_router, m_router_bias, m_w_gate_up, m_w_down, m_ws_gate_up, m_ws_down, final_norm_g):
    bp, t, d = x_prompt.shape
    bs, ts, _ = x_sample.shape
    depth = m_norm_g.shape[0]
    n_a = a_norm_g.shape[0]
    d_rnn = a_w_out.shape[1]
    heads = N_HEADS
    dh = b_w_q.shape[-1] // heads
    page = cache_k.shape[1]
    np_rows = bp * t
    ns_rows = bs * ts
    assert n_a == 1 and depth == 2, "layer pattern of this step: one self-decoder, one cross-decoder layer"

    n_c = bp + bs
    c_rows = -(-n_c // SUBLANES) * SUBLANES
    c_all = jnp.zeros((c_rows, d), F32).at[:bp].set(c_prompt).at[bp:n_c].set(c_sample)

    def mods(w, b):
        m = _modulation(c_all, w, b)
        mp = m[:bp].reshape(bp, 1, -1)
        ms = jnp.tile(m[bp:n_c], (ts, 1)).reshape(1, ns_rows, -1)
        return mp, ms

    a_mod = mods(a_mod_w[0], a_mod_b[0])
    kv_mod = mods(kv_mod_w, kv_mod_b)
    b_mod = mods(b_mod_w[0], b_mod_b[0])
    m_mod = [mods(m_mod_w[l], m_mod_b[l]) for l in range(depth)]

    bf = lambda w: w.astype(BF16)

    rec_w = (a_norm_g[0], bf(a_w_in[0]), a_conv_w[0], a_conv_b[0], bf(a_w_gate_r[0]), a_b_gate_r[0],
             bf(a_w_gate_i[0]), a_b_gate_i[0], a_lambda[0], bf(a_w_out[0]))
    hist_p = jnp.zeros((bp, CONV_W - 1, d_rnn), F32)
    h0_p = jnp.zeros((bp, 1, d_rnn), F32)
    x1_p, conv_p, h_p = _recurrent(x_prompt, a_mod[0], hist_p, h0_p, *rec_w,
                                   stride=1, steps=REC_ROWS, mod_rows=1)
    xs_tm = x_sample.swapaxes(0, 1).reshape(1, ns_rows, d)
    hist_s = state_conv[0].swapaxes(0, 1).reshape(1, (CONV_W - 1) * bs, d_rnn)
    h0_s = state_h[0].reshape(1, bs, d_rnn)
    x1_s, conv_s, h_s = _recurrent(xs_tm, a_mod[1], hist_s, h0_s, *rec_w,
                                   stride=bs, steps=ts, mod_rows=ns_rows)

    grp_p = _Group(np_rows, ROUTER_ROWS, t // ROUTER_ROWS, 1)
    grp_s = _Group(ns_rows, ns_rows, 1, ns_rows)
    groups = [grp_p, grp_s]
    cgroups = [_Group(np_rows, COMBINE_ROWS, t // COMBINE_ROWS, 1), grp_s]

    x2_p, x2_s = _moe_layer(groups, cgroups, [x1_p.reshape(np_rows, d), x1_s.reshape(ns_rows, d)],
                            [m_mod[0][0], m_mod[0][1]], [None, None], m_norm_g[0], m_w_router[0],
                            m_router_bias[0], m_w_gate_up[0], m_w_down[0], bf(m_ws_gate_up[0]),
                            bf(m_ws_down[0]), final_norm_g, False)

    q_scale = float(dh) ** -0.5 * LOG2E
    w_kv_b, w_q_b = bf(w_kv), bf(b_w_q[0])
    pgrp_p = _Group(np_rows, PROJ_ROWS, t // PROJ_ROWS, 1)
    kt_p, vt_p, kb_p, vb_p, q_p = _proj(pgrp_p, x2_p, kv_mod[0], kv_norm_g, w_kv_b, b_mod[0], b_norm_g[0],
                                        w_q_b, q_scale, n_seq=bp)
    k_s, v_s, q_s = _proj(grp_s, x2_s, kv_mod[1], kv_norm_g, w_kv_b, b_mod[1], b_norm_g[0],
                          w_q_b, q_scale)

    hd = heads * dh
    o_p = _attn_prompt(q_p.reshape(bp, t, hd), kb_p.reshape(bp, t, hd), vb_p.reshape(bp, t, hd),
                       b_sb_bias[0]).reshape(np_rows, hd)

    q_pad = SUBLANES
    q3 = jnp.zeros((bs, heads, q_pad, dh), BF16).at[:, :, :ts].set(
        q_s.reshape(ts, bs, heads, dh).transpose(1, 2, 0, 3))

    def new_keys(a):
        a = a.reshape(ts, bs, heads, dh).transpose(1, 2, 3, 0)
        return jnp.zeros((bs, heads, dh, page), F32).at[..., :ts].set(a)

    bias_rows = jnp.broadcast_to(jnp.repeat(b_sb_bias[0] * LOG2E, q_pad)[:, None], (heads * q_pad, page))
    o3 = _attn_sample(page_table, bias_rows, q3, new_keys(k_s), new_keys(v_s),
                      cache_k.transpose(0, 2, 3, 1), cache_v.transpose(0, 2, 3, 1), SAMPLE_PAGES_PER_STEP)
    o_s = o3[:, :, :ts].transpose(2, 0, 1, 3).reshape(ns_rows, hd).astype(BF16)

    w_o_b = bf(b_w_o[0])
    y_p, y_s = _moe_layer(groups, cgroups, [x2_p, x2_s], [m_mod[1][0], m_mod[1][1]],
                          [(o_p, w_o_b, b_mod[0]), (o_s, w_o_b, b_mod[1])], m_norm_g[1], m_w_router[1],
                          m_router_bias[1], m_w_gate_up[1], m_w_down[1], bf(m_ws_gate_up[1]),
                          bf(m_ws_down[1]), final_norm_g, True)

    def from_tm(a, *tail):
        return a.reshape(ts, bs, *tail).swapaxes(0, 1)

    return (y_p.reshape(bp, t, d),
            from_tm(y_s, d),
            conv_p[None],
            h_p.reshape(1, bp, d_rnn),
            kt_p.reshape(bp, heads, dh, t).transpose(0, 3, 1, 2),
            vt_p.reshape(bp, heads, dh, t).transpose(0, 3, 1, 2),
            conv_s.reshape(CONV_W - 1, bs, d_rnn).swapaxes(0, 1)[None],
            h_s.reshape(1, bs, d_rnn),
            from_tm(k_s, heads, dh),
            from_tm(v_s, heads, dh))
```

```python
import functools

import jax
import jax.numpy as jnp
from jax import lax
from jax.experimental import pallas as pl
from jax.experimental.pallas import tpu as pltpu

F32 = jnp.float32
BF16 = jnp.bfloat16
I32 = jnp.int32

CONV_W = 4
LRU_BLOCKS = 4
RGLRU_C = 8.0
N_HEADS = 16
N_EXPERTS = 64
TOP_K = 8
N_EXPERT_GROUPS = 8
TOPK_GROUPS = 4
ROUTED_SCALE = 2.5
EPS = 1e-6
NEG_INF = float("-inf")

LANES = 128
SUBLANES = 8
TILE_CHUNKS = SUBLANES
PACKED_CHUNKS = TILE_CHUNKS // 2
VMEM_LIMIT_BYTES = 56 * 1024 * 1024

REC_ROWS = 256
ROUTER_ROWS = 512
PROJ_ROWS = 512
DISPATCH_ROWS = 256
COMBINE_ROWS = 128
EXPERT_ROWS = 256
ATTN_Q_ROWS = 256
ATTN_K_ROWS = 128
SAMPLE_PAGES_PER_STEP = 4


def _params(*sem):
    return pltpu.CompilerParams(dimension_semantics=sem, vmem_limit_bytes=VMEM_LIMIT_BYTES)


def _dot(a, b):
    return jnp.dot(a, b, preferred_element_type=F32)


def _dot_nt(a, b, precision=None):
    return lax.dot_general(a, b, (((1,), (1,)), ((), ())), precision=precision,
                           preferred_element_type=F32)


def _sigmoid(x):
    return 1.0 / (1.0 + jnp.exp(-x))


def _silu(x):
    return x * _sigmoid(x)


def _gelu_tanh(x):
    return 0.5 * x * (1.0 + jnp.tanh(0.7978845608028654 * (x + 0.044715 * (x * x * x))))


def _rms(x, g):
    ms = jnp.mean(x * x, axis=-1, keepdims=True)
    return x * lax.rsqrt(ms + EPS) * g


def _ada(x, g, shift, scale):
    return _rms(x, g) * (1.0 + scale) + shift


def _swiglu(xb, w_gu, w_down):
    gu = _dot(xb, w_gu)
    half = gu.shape[-1] // 2
    act = _silu(gu[:, :half]) * gu[:, half:]
    return _dot(act.astype(BF16), w_down)


def _mod_kernel(c_ref, w_ref, b_ref, o_ref):
    c = c_ref[...]
    o_ref[...] = _dot(_silu(c).astype(BF16), w_ref[...].astype(BF16)) + b_ref[...]


def _modulation(c_all, w, b):
    rows, d = c_all.shape
    n_out = w.shape[1]
    tn = 1024
    return pl.pallas_call(
        _mod_kernel,
        grid=(n_out // tn,),
        in_specs=[pl.BlockSpec((rows, d), lambda j: (0, 0)),
                  pl.BlockSpec((d, tn), lambda j: (0, j)),
                  pl.BlockSpec((1, tn), lambda j: (0, j))],
        out_specs=pl.BlockSpec((rows, tn), lambda j: (0, j)),
        out_shape=jax.ShapeDtypeStruct((rows, n_out), F32),
        compiler_params=_params("arbitrary"),
        name="modulation",
    )(c_all, w, b.reshape(1, n_out))


class _Group:
    def __init__(self, n_rows, rows, blocks_per_seq, mod_rows):
        self.n_rows = n_rows
        self.rows = rows
        self.blocks_per_seq = blocks_per_seq
        self.mod_rows = mod_rows
        self.n_blocks = n_rows // rows

    def mod_spec(self, col, d):
        bps = self.blocks_per_seq
        return pl.BlockSpec((1, self.mod_rows, d), lambda i: (i // bps, 0, col))

    def row_spec(self, width):
        return pl.BlockSpec((self.rows, width), lambda i: (i, 0))


def _const_spec(shape):
    nd = len(shape)
    return pl.BlockSpec(shape, lambda *_: (0,) * nd)


def _scan_rows(a, u, stride, steps):
    row = lax.broadcasted_iota(I32, a.shape, 0)
    s = 1
    while s < steps:
        sh = s * stride
        a_p = pltpu.roll(a, sh, 0)
        u_p = pltpu.roll(u, sh, 0)
        m = row >= sh
        u = jnp.where(m, a * u_p + u, u)
        a = jnp.where(m, a * a_p, a)
        s *= 2
    return a, u


def _rec_kernel(x_ref, sh_ref, sc_ref, gt_ref, hist_ref, h0_ref, g_ref, w_in_ref, cw_ref, cb_ref,
                wr_ref, br_ref, wi_ref, bi_ref, lam_ref, w_out_ref,
                xo_ref, conv_ref, hl_ref, xpad_sc, h_sc, *, stride, steps, hist0):
    t = pl.program_id(1)
    rows = stride * steps
    hrows = (CONV_W - 1) * stride
    d_rnn = h_sc.shape[-1]

    @pl.when(t == 0)
    def _():
        xpad_sc[pl.ds(hist0 - hrows, hrows), :] = hist_ref[0]
        h_sc[...] = h0_ref[0]

    x = x_ref[0]
    hn = _ada(x, g_ref[...], sh_ref[0], sc_ref[0])
    proj = _dot(hn.astype(BF16), w_in_ref[...])
    gate_br = proj[:, :d_rnn]
    xb = proj[:, d_rnn:]
    xpad_sc[pl.ds(hist0, rows), :] = xb
    xc = cb_ref[...] + cw_ref[0:1, :] * xpad_sc[pl.ds(hist0 - hrows, rows), :]
    for j in range(1, CONV_W):
        xc = xc + cw_ref[j:j + 1, :] * xpad_sc[pl.ds(hist0 - hrows + j * stride, rows), :]
    new_hist = xpad_sc[pl.ds(hist0 + rows - hrows, hrows), :]
    xpad_sc[pl.ds(hist0 - hrows, hrows), :] = new_hist
    conv_ref[0] = new_hist

    bw = d_rnn // LRU_BLOCKS
    r_parts, i_parts = [], []
    for n in range(LRU_BLOCKS):
        xblk = xc[:, n * bw:(n + 1) * bw].astype(BF16)
        r_parts.append(_dot(xblk, wr_ref[n]))
        i_parts.append(_dot(xblk, wi_ref[n]))
    r = _sigmoid(jnp.concatenate(r_parts, axis=-1) + br_ref[...])
    ig = _sigmoid(jnp.concatenate(i_parts, axis=-1) + bi_ref[...])
    neg_lam = -lam_ref[...]
    softplus = jnp.maximum(neg_lam, 0.0) + jnp.log1p(jnp.exp(-jnp.abs(neg_lam)))
    log_a = (-RGLRU_C * r) * softplus
    a = jnp.exp(log_a)
    u = jnp.sqrt(-jnp.tanh(log_a) * (a * a + 1.0)) * (ig * xc)
    a_cum, u_cum = _scan_rows(a, u, stride, steps)
    h_prev = h_sc[...]
    if stride == 1:
        h = a_cum * h_prev + u_cum
    else:
        h = a_cum * jnp.concatenate([h_prev] * steps, axis=0) + u_cum
    h_last = h[rows - stride:, :]
    h_sc[...] = h_last
    hl_ref[0] = h_last
    y = _dot((_gelu_tanh(gate_br) * h).astype(BF16), w_out_ref[...])
    xo_ref[0] = x + gt_ref[0] * y


def _recurrent(x3, mod, hist, h0, g, w_in, conv_w, conv_b, w_r, b_r, w_i, b_i, lam, w_out,
               *, stride, steps, mod_rows):
    nb, t_rows, d = x3.shape
    d_rnn = w_out.shape[0]
    rows = stride * steps
    nt = t_rows // rows
    hrows = (CONV_W - 1) * stride
    hist0 = -(-hrows // SUBLANES) * SUBLANES
    assert rows >= hrows and rows % SUBLANES == 0
    x_spec = pl.BlockSpec((1, rows, d), lambda b, t: (b, t, 0))

    def mspec(col):
        if mod_rows == 1:
            return pl.BlockSpec((1, 1, d), lambda b, t: (b, 0, col))
        return pl.BlockSpec((1, rows, d), lambda b, t: (b, 0, col))

    def cspec(shape):
        nd = len(shape)
        return pl.BlockSpec(shape, lambda b, t: (0,) * nd)

    kern = functools.partial(_rec_kernel, stride=stride, steps=steps, hist0=hist0)
    return pl.pallas_call(
        kern,
        grid=(nb, nt),
        in_specs=[x_spec, mspec(0), mspec(1), mspec(2),
                  pl.BlockSpec((1, hrows, d_rnn), lambda b, t: (b, 0, 0)),
                  pl.BlockSpec((1, stride, d_rnn), lambda b, t: (b, 0, 0)),
                  cspec((1, d)), cspec(w_in.shape), cspec(conv_w.shape), cspec((1, d_rnn)),
                  cspec(w_r.shape), cspec((1, d_rnn)), cspec(w_i.shape), cspec((1, d_rnn)),
                  cspec((1, d_rnn)), cspec(w_out.shape)],
        out_specs=[x_spec,
                   pl.BlockSpec((1, hrows, d_rnn), lambda b, t: (b, 0, 0)),
                   pl.BlockSpec((1, stride, d_rnn), lambda b, t: (b, 0, 0))],
        out_shape=[jax.ShapeDtypeStruct(x3.shape, F32),
                   jax.ShapeDtypeStruct((nb, hrows, d_rnn), F32),
                   jax.ShapeDtypeStruct((nb, stride, d_rnn), F32)],
        scratch_shapes=[pltpu.VMEM((hist0 + rows, d_rnn), F32), pltpu.VMEM((stride, d_rnn), F32)],
        compiler_params=_params("arbitrary", "arbitrary"),
        name="recurrent_block",
    )(x3, mod, mod, mod, hist, h0, g.reshape(1, d), w_in, conv_w, conv_b.reshape(1, d_rnn),
      w_r, b_r.reshape(1, d_rnn), w_i, b_i.reshape(1, d_rnn), lam.reshape(1, d_rnn), w_out)


def _tree(op, xs):
    xs = list(xs)
    while len(xs) > 1:
        nxt = [op(xs[i], xs[i + 1]) for i in range(0, len(xs) - 1, 2)]
        if len(xs) % 2:
            nxt.append(xs[-1])
        xs = nxt
    return xs[0]


def _to_token_tiles(ref, x):
    rows = x.shape[0]
    for c in range(TILE_CHUNKS):
        ref[pl.ds(c, rows, stride=TILE_CHUNKS), :] = x[:, c * LANES:(c + 1) * LANES]


def _from_token_tiles(ref, rows):
    return jnp.concatenate([ref[pl.ds(c, rows, stride=TILE_CHUNKS), :] for c in range(TILE_CHUNKS)], axis=1)


HIGH_HALF = 0xFFFF0000


def _to_packed_tiles(ref, xb):
    rows, d = xb.shape
    bits = lax.bitcast_convert_type(xb.astype(F32), jnp.uint32)
    packed = bits[:, d // 2:] | (bits[:, :d // 2] >> 16)
    for c in range(PACKED_CHUNKS):
        ref[pl.ds(c, rows, stride=PACKED_CHUNKS), :] = packed[:, c * LANES:(c + 1) * LANES]


def _from_packed_tiles(ref, rows):
    words = [ref[pl.ds(c, rows, stride=PACKED_CHUNKS), :] for c in range(PACKED_CHUNKS)]
    low = [lax.bitcast_convert_type(w << 16, F32) for w in words]
    high = [lax.bitcast_convert_type(w & jnp.uint32(HIGH_HALF), F32) for w in words]
    return jnp.concatenate(low + high, axis=1).astype(BF16)


def _router_kernel(*refs, pre):
    if pre:
        (x_ref, o_ref, wo_ref, agt_ref, sh_ref, sc_ref, g_ref, wrt_ref, rb_ref, cnt_in_ref, wsgu_ref, wsd_ref,
         xo_ref, hn_ref, shared_ref, e_ref, p_ref, w_ref, cnt_ref, run_sc) = refs
    else:
        (x_ref, sh_ref, sc_ref, g_ref, wrt_ref, rb_ref, cnt_in_ref, wsgu_ref, wsd_ref,
         hn_ref, shared_ref, e_ref, p_ref, w_ref, cnt_ref, run_sc) = refs
    i = pl.program_id(0)
    per = N_EXPERTS // N_EXPERT_GROUPS

    @pl.when(i == 0)
    def _():
        run_sc[...] = cnt_in_ref[...]

    x = x_ref[...]
    if pre:
        x = x + agt_ref[0] * _dot(o_ref[...], wo_ref[...])
        xo_ref[...] = x
    hn = _ada(x, g_ref[...], sh_ref[0], sc_ref[0])
    hb = hn.astype(BF16)
    _to_packed_tiles(hn_ref, hb)
    shared_ref[...] = _swiglu(hb, wsgu_ref[...], wsd_ref[...])
    tb = hn.shape[0]
    logits = _dot_nt(wrt_ref[...], hn, precision=lax.Precision.HIGHEST)
    scores = _sigmoid(logits)
    sel = scores + rb_ref[...][:, 0:1]
    sel_j = [sel[j * SUBLANES:(j + 1) * SUBLANES, :] for j in range(per)]
    sc_j = [scores[j * SUBLANES:(j + 1) * SUBLANES, :] for j in range(per)]
    m1 = _tree(jnp.maximum, sel_j)
    j1 = _tree(jnp.minimum, [jnp.where(sel_j[j] == m1, j, per) for j in range(per)])
    m2 = _tree(jnp.maximum, [jnp.where(j1 == j, NEG_INF, sel_j[j]) for j in range(per)])
    gs = m1 + m2
    g_iota = lax.broadcasted_iota(I32, gs.shape, 0)
    rank = jnp.zeros(gs.shape, I32)
    for dlt in range(1, N_EXPERT_GROUPS):
        other = pltpu.roll(gs, dlt, 0)
        tie = jnp.where(g_iota >= dlt, 1, 0)
        rank = rank + jnp.where(other > gs, 1, jnp.where(other == gs, tie, 0))
    gmask = rank < TOPK_GROUPS
    masked = [jnp.where(gmask, s, NEG_INF) for s in sel_j]
    eid = [g_iota * per + j for j in range(per)]
    chosen = [jnp.zeros(gs.shape, F32) for _ in range(per)]
    e_rows, s_rows = [], []
    for k in range(TOP_K):
        m = jnp.max(_tree(jnp.maximum, masked), axis=0, keepdims=True)
        cand = _tree(jnp.minimum, [jnp.where(masked[j] == m, eid[j], N_EXPERTS) for j in range(per)])
        ek = jnp.min(cand, axis=0, keepdims=True)
        sk = jnp.zeros(gs.shape, F32)
        for j in range(per):
            oh = eid[j] == ek
            masked[j] = jnp.where(oh, NEG_INF, masked[j])
            chosen[j] = jnp.where(oh, 1.0, chosen[j])
            sk = sk + jnp.where(oh, sc_j[j], 0.0)
        e_rows.append(ek)
        s_rows.append(jnp.sum(sk, axis=0, keepdims=True))
    denom = _tree(lambda p, q: p + q, s_rows)
    sel_mask = jnp.concatenate(chosen, axis=0)
    upper = (lax.broadcasted_iota(I32, (tb, tb), 0) < lax.broadcasted_iota(I32, (tb, tb), 1))
    prefix = _dot(sel_mask.astype(BF16), jnp.where(upper, 1.0, 0.0).astype(BF16)) + run_sc[...][:, 0:1]
    pre_j = [prefix[j * SUBLANES:(j + 1) * SUBLANES, :] for j in range(per)]
    for k in range(TOP_K):
        pk = jnp.zeros(gs.shape, F32)
        for j in range(per):
            pk = pk + jnp.where(eid[j] == e_rows[k], pre_j[j], 0.0)
        e_ref[pl.ds(k, 1), :] = e_rows[k]
        p_ref[pl.ds(k, 1), :] = jnp.sum(pk, axis=0, keepdims=True).astype(I32)
        w_ref[pl.ds(k, 1), :] = s_rows[k] / denom * ROUTED_SCALE
    run = run_sc[...] + jnp.sum(sel_mask, axis=1, keepdims=True)
    run_sc[...] = run
    cnt_ref[...] = run


def _router(grp, x, mod, g, w_rt, rbias, cnt_in, ws_gu, ws_down, pre=None):
    d = x.shape[-1]
    assert d == TILE_CHUNKS * LANES
    n, tb = grp.n_rows, grp.rows
    ins = [x]
    in_specs = [grp.row_spec(d)]
    if pre is not None:
        o, w_o, amod = pre
        ins += [o, w_o, amod]
        in_specs += [grp.row_spec(o.shape[-1]), _const_spec(w_o.shape), grp.mod_spec(2, d)]
    ins += [mod, mod, g.reshape(1, d), w_rt, rbias, cnt_in, ws_gu, ws_down]
    in_specs += [grp.mod_spec(0, d), grp.mod_spec(1, d), _const_spec((1, d)), _const_spec(w_rt.shape),
                 _const_spec(rbias.shape), _const_spec(cnt_in.shape), _const_spec(ws_gu.shape),
                 _const_spec(ws_down.shape)]
    tok_spec = pl.BlockSpec((TOP_K, tb), lambda i: (0, i))
    out_specs = [pl.BlockSpec((tb * PACKED_CHUNKS, LANES), lambda i: (i, 0)), grp.row_spec(d),
                 tok_spec, tok_spec, tok_spec, _const_spec(cnt_in.shape)]
    out_shape = [jax.ShapeDtypeStruct((n * PACKED_CHUNKS, LANES), jnp.uint32), jax.ShapeDtypeStruct((n, d), F32),
                 jax.ShapeDtypeStruct((TOP_K, n), I32),
                 jax.ShapeDtypeStruct((TOP_K, n), I32), jax.ShapeDtypeStruct((TOP_K, n), F32),
                 jax.ShapeDtypeStruct(cnt_in.shape, F32)]
    if pre is not None:
        out_specs = [grp.row_spec(d)] + out_specs
        out_shape = [jax.ShapeDtypeStruct((n, d), F32)] + out_shape
    return pl.pallas_call(
        functools.partial(_router_kernel, pre=pre is not None),
        grid=(grp.n_blocks,),
        in_specs=in_specs, out_specs=out_specs, out_shape=out_shape,
        scratch_shapes=[pltpu.VMEM(cnt_in.shape, F32)],
        compiler_params=_params("arbitrary"),
        name="router",
    )(*ins)


def _experts_kernel(blk_ref, exp_ref, lo_ref, hi_ref, first_ref, last_ref, newe_ref, *refs, n_src):
    del exp_ref
    pair_ref = refs[0]
    src_refs = refs[1:1 + n_src]
    (wgu_ref, wd_ref, slots_ref, wgu_sc, wd_sc, x_sc, y_sc, out_sc, tok_sc, xg_sc, sems, load_sem
     ) = refs[1 + n_src:]
    s = pl.program_id(0)
    lo = lo_ref[s]
    hi = hi_ref[s]
    rb = x_sc.shape[0]
    blk = blk_ref[s]

    @pl.when(s == 0)
    def _():
        row = 0
        copies = []
        for src in src_refs:
            copies.append(pltpu.make_async_copy(src, tok_sc.at[pl.ds(row, src.shape[0])], load_sem))
            row += src.shape[0]
        for cp in copies:
            cp.start()
        for cp in copies:
            cp.wait()

    def packed_tile(ref, r):
        return ref.at[pl.ds(pl.multiple_of(r * PACKED_CHUNKS, PACKED_CHUNKS), PACKED_CHUNKS)]

    def tile(ref, r):
        return ref.at[pl.ds(pl.multiple_of(r * TILE_CHUNKS, TILE_CHUNKS), TILE_CHUNKS)]

    def row_copy(slot, r):
        return pltpu.make_async_copy(tile(out_sc.at[slot], r), tile(slots_ref, pair_ref[0, 0, r]),
                                     sems.at[slot])

    def drain(slot):
        def body(r, c):
            row_copy(slot, r).wait()
            return c
        lax.fori_loop(0, rb, body, 0, unroll=8)

    @pl.when(newe_ref[s] == 1)
    def _():
        wgu_sc[...] = wgu_ref[0].astype(BF16)
        wd_sc[...] = wd_ref[0].astype(BF16)

    @pl.when(first_ref[s] == 1)
    def _():
        def gather(r, c):
            tok = lax.shift_right_logical(pair_ref[0, 0, r], TOP_K.bit_length() - 1)
            packed_tile(xg_sc, r)[...] = packed_tile(tok_sc, tok)[...]
            return c
        lax.fori_loop(0, rb, gather, 0, unroll=8)
        x_sc[...] = _from_packed_tiles(xg_sc, rb)
        y_sc[...] = jnp.zeros(y_sc.shape, F32)

    @pl.when(hi > lo)
    def _():
        gu = _dot(x_sc[...], wgu_sc[...])
        half = gu.shape[-1] // 2
        act = _silu(gu[:, :half]) * gu[:, half:]
        row = lax.broadcasted_iota(I32, act.shape, 0)
        act = jnp.where((row >= lo) & (row < hi), act, 0.0)
        y_sc[...] += _dot(act.astype(BF16), wd_sc[...])

    @pl.when(last_ref[s] == 1)
    def _():
        for slot in range(2):
            @pl.when(blk % 2 == slot)
            def _():
                @pl.when(blk >= 2)
                def _():
                    drain(slot)

                _to_token_tiles(out_sc.at[slot], y_sc[...])

                def issue(r, c):
                    row_copy(slot, r).start()
                    return c
                lax.fori_loop(0, rb, issue, 0, unroll=8)

    @pl.when(s == pl.num_programs(0) - 1)
    def _():
        for slot in range(2):
            @pl.when((blk % 2 == slot) | (blk >= 1))
            def _():
                drain(slot)


def _experts(steps, packed_rows, row_pair, w_gu, w_down):
    n_tok = sum(p.shape[0] for p in packed_rows) // PACKED_CHUNKS
    m = n_tok * TOP_K
    d = TILE_CHUNKS * LANES
    rb = EXPERT_ROWS
    n_steps = steps[0].shape[0]
    de2 = w_gu.shape[-1]
    grid_spec = pltpu.PrefetchScalarGridSpec(
        num_scalar_prefetch=7,
        grid=(n_steps,),
        in_specs=[pl.BlockSpec((1, 1, rb), lambda s, blk, ex, *_: (blk[s], 0, 0), memory_space=pltpu.SMEM)]
                 + [pl.BlockSpec(memory_space=pl.ANY)] * len(packed_rows)
                 + [pl.BlockSpec((1, d, de2), lambda s, blk, ex, *_: (ex[s], 0, 0)),
                    pl.BlockSpec((1, de2 // 2, d), lambda s, blk, ex, *_: (ex[s], 0, 0))],
        out_specs=pl.BlockSpec(memory_space=pl.ANY),
        scratch_shapes=[pltpu.VMEM((d, de2), BF16), pltpu.VMEM((de2 // 2, d), BF16),
                        pltpu.VMEM((rb, d), BF16), pltpu.VMEM((rb, d), F32),
                        pltpu.VMEM((2, rb * TILE_CHUNKS, LANES), F32),
                        pltpu.VMEM((n_tok * PACKED_CHUNKS, LANES), jnp.uint32),
                        pltpu.VMEM((rb * PACKED_CHUNKS, LANES), jnp.uint32),
                        pltpu.SemaphoreType.DMA((2,)), pltpu.SemaphoreType.DMA(())],
    )
    return pl.pallas_call(
        functools.partial(_experts_kernel, n_src=len(packed_rows)),
        grid_spec=grid_spec,
        out_shape=jax.ShapeDtypeStruct((m * TILE_CHUNKS, LANES), F32),
        compiler_params=_params("arbitrary"),
        name="routed_experts",
    )(*steps, row_pair.reshape(m // rb, 1, rb), *packed_rows, w_gu, w_down)


def _expert_steps(counts, m):
    rb = EXPERT_ROWS
    nb = m // rb
    start = jnp.cumsum(counts) - counts
    end = start + counts
    cuts = jnp.sort(jnp.concatenate([jnp.arange(nb, dtype=I32) * rb, start.astype(I32)]))
    nxt = jnp.concatenate([cuts[1:], jnp.array([m], I32)])
    blk = jnp.minimum(cuts // rb, nb - 1)
    ex = jnp.minimum(jnp.sum((end[None, :] <= cuts[:, None]).astype(I32), axis=1), N_EXPERTS - 1)
    lo = cuts - blk * rb
    hi = nxt - blk * rb
    changes = (blk[1:] != blk[:-1]).astype(I32)
    first = jnp.concatenate([jnp.ones((1,), I32), changes])
    last = jnp.concatenate([changes, jnp.ones((1,), I32)])
    newe = jnp.concatenate([jnp.ones((1,), I32), (ex[1:] != ex[:-1]).astype(I32)])
    return blk, ex, lo, hi, first, last, newe


def _combine_kernel(ew_ref, x_ref, shared_ref, gt_ref, fg_ref, slots_ref, o_ref, routed_sc, *, rows, final):
    def weigh(r, c):
        acc = None
        for k in range(TOP_K):
            p0 = pl.multiple_of((r * TOP_K + k) * TILE_CHUNKS, TILE_CHUNKS)
            term = ew_ref[r * TOP_K + k] * slots_ref[pl.ds(p0, TILE_CHUNKS), :]
            acc = term if acc is None else acc + term
        routed_sc[pl.ds(pl.multiple_of(r * TILE_CHUNKS, TILE_CHUNKS), TILE_CHUNKS), :] = acc
        return c

    lax.fori_loop(0, rows, weigh, 0, unroll=2)
    routed = _from_token_tiles(routed_sc, rows)
    x = x_ref[...] + gt_ref[0] * (routed + shared_ref[...])
    if final:
        x = _rms(x, fg_ref[...])
    o_ref[...] = x


def _combine(grp, row0, ew, x, shared, mod, fg, slots, final):
    d = x.shape[-1]
    rows = grp.rows
    blk0 = row0 // rows
    assert row0 % rows == 0
    return pl.pallas_call(
        functools.partial(_combine_kernel, rows=rows, final=final),
        grid=(grp.n_blocks,),
        in_specs=[pl.BlockSpec((rows * TOP_K,), lambda i: (i,), memory_space=pltpu.SMEM),
                  grp.row_spec(d), grp.row_spec(d), grp.mod_spec(2, d), _const_spec((1, d)),
                  pl.BlockSpec((rows * TOP_K * TILE_CHUNKS, LANES), lambda i: (i + blk0, 0))],
        out_specs=grp.row_spec(d),
        out_shape=jax.ShapeDtypeStruct(x.shape, F32),
        scratch_shapes=[pltpu.VMEM((rows * TILE_CHUNKS, LANES), F32)],
        compiler_params=_params("arbitrary"),
        name="combine",
    )(ew, x, shared, mod, fg.reshape(1, d), slots)


def _proj_kernel(*refs, q_scale, transposed):
    if transposed:
        (x_ref, ksh_ref, ksc_ref, kg_ref, wkv_ref, wkvt_ref, bsh_ref, bsc_ref, bg_ref, wq_ref,
         k_ref, v_ref, kb_ref, vb_ref, q_ref) = refs
    else:
        (x_ref, ksh_ref, ksc_ref, kg_ref, wkv_ref, bsh_ref, bsc_ref, bg_ref, wq_ref,
         k_ref, v_ref, q_ref) = refs
    x = x_ref[...]
    kvn = _ada(x, kg_ref[...], ksh_ref[0], ksc_ref[0]).astype(BF16)
    kv = _dot(kvn, wkv_ref[...])
    half = kv.shape[-1] // 2
    if transposed:
        kvt = _dot_nt(wkvt_ref[...], kvn)
        k_ref[0] = kvt[:half]
        v_ref[0] = kvt[half:]
        kb_ref[...] = kv[:, :half].astype(BF16)
        vb_ref[...] = kv[:, half:].astype(BF16)
    else:
        k_ref[...] = kv[:, :half]
        v_ref[...] = kv[:, half:]
    hn = _ada(x, bg_ref[...], bsh_ref[0], bsc_ref[0])
    q_ref[...] = (_dot(hn.astype(BF16), wq_ref[...]) * q_scale).astype(BF16)


def _proj(grp, x, kv_mod, kv_g, w_kv, b_mod, b_g, w_q, q_scale, n_seq=None):
    d = x.shape[-1]
    n = grp.n_rows
    hd = w_q.shape[-1]
    transposed = n_seq is not None
    ins = [x, kv_mod, kv_mod, kv_g.reshape(1, d), w_kv]
    in_specs = [grp.row_spec(d), grp.mod_spec(0, d), grp.mod_spec(1, d), _const_spec((1, d)),
                _const_spec(w_kv.shape)]
    if transposed:
        ins.append(w_kv.T)
        in_specs.append(_const_spec((w_kv.shape[1], w_kv.shape[0])))
        bps = grp.blocks_per_seq
        t_spec = pl.BlockSpec((1, hd, grp.rows), lambda i: (i // bps, 0, i % bps))
        out_specs = [t_spec, t_spec, grp.row_spec(hd), grp.row_spec(hd), grp.row_spec(hd)]
        t_shape = jax.ShapeDtypeStruct((n_seq, hd, n // n_seq), F32)
        out_shape = [t_shape, t_shape] + [jax.ShapeDtypeStruct((n, hd), BF16)] * 3
    else:
        out_specs = [grp.row_spec(hd)] * 3
        out_shape = [jax.ShapeDtypeStruct((n, hd), F32), jax.ShapeDtypeStruct((n, hd), F32),
                     jax.ShapeDtypeStruct((n, hd), BF16)]
    ins += [b_mod, b_mod, b_g.reshape(1, d), w_q]
    in_specs += [grp.mod_spec(0, d), grp.mod_spec(1, d), _const_spec((1, d)), _const_spec(w_q.shape)]
    return pl.pallas_call(
        functools.partial(_proj_kernel, q_scale=q_scale, transposed=transposed),
        grid=(grp.n_blocks,),
        in_specs=in_specs, out_specs=out_specs, out_shape=out_shape,
        compiler_params=_params("arbitrary"),
        name="kv_q_projection",
    )(*ins)


def _later_key_matrix(tk, n_heads=1):
    size = n_heads * tk
    j = lax.broadcasted_iota(I32, (size, size), 0)
    s = lax.broadcasted_iota(I32, (size, size), 1)
    later = jnp.where(j % tk > s % tk, 1.0, 0.0)
    return jnp.where(j // tk == s // tk, later, 0.0).astype(BF16)


LOG2E = 1.4426950408889634
SIGN_BIT = 0x80000000


def _sb_logs(z2, tri, mask):
    neg_abs = lax.bitcast_convert_type(lax.bitcast_convert_type(z2, jnp.uint32) | jnp.uint32(SIGN_BIT), F32)
    neg_l1m = jnp.maximum(z2, 0.0) + jnp.log2(1.0 + jnp.exp2(neg_abs))
    log_b = z2 - neg_l1m
    if mask is not None:
        neg_l1m = jnp.where(mask, neg_l1m, 0.0)
    return log_b, _dot(neg_l1m.astype(BF16), tri), neg_l1m


def _sb_weights(log_b, tail, carry, mask):
    w = jnp.exp2(log_b - tail - carry)
    if mask is not None:
        w = jnp.where(mask, w, 0.0)
    return w


def _attn_prompt_kernel(bias_ref, q_ref, k_ref, v_ref, o_ref, acc_sc, z_sc, lb_sc, tail_sc, *, tq, tk):
    hp = pl.program_id(1)
    qi = pl.program_id(2)
    q = q_ref[0]
    half = q.shape[-1] // 2
    tri = _later_key_matrix(tk, 2)
    pair_lane = lax.broadcasted_iota(I32, (tq, 2 * tk), 1)
    q_pos = qi * tq + lax.broadcasted_iota(I32, (tq, 2 * tk), 0)
    k_off = pair_lane % tk
    q_pos_head = qi * tq + lax.broadcasted_iota(I32, (tq, tk), 0)
    k_off_head = lax.broadcasted_iota(I32, (tq, tk), 1)
    n_diag = tq // tk
    assert n_diag == 2, "the block pipeline below alternates two slots per loop trip"
    bias2 = jnp.where(pair_lane < tk, bias_ref[hp * 2] * LOG2E, bias_ref[hp * 2 + 1] * LOG2E)
    first_head = lax.broadcasted_iota(I32, (tk, q.shape[-1]), 1) < half
    acc_sc[...] = jnp.zeros(acc_sc.shape, F32)

    def split_heads(a):
        zero = jnp.zeros_like(a)
        return jnp.concatenate([jnp.where(first_head, a, zero), jnp.where(first_head, zero, a)], axis=0)

    def keys(ref, j):
        return split_heads(ref[0, pl.ds(pl.multiple_of(j * tk, tk), tk), :])

    def logits(j):
        return _dot_nt(q, keys(k_ref, j))

    def logs(z_pair, mask):
        log_b, tail, neg_l1m = _sb_logs(z_pair + bias2, tri, mask)
        sums = tuple(jnp.sum(neg_l1m[:, hh * tk:(hh + 1) * tk], axis=1, keepdims=True) for hh in range(2))
        return log_b, tail, sums

    def accumulate(j, log_b, tail, carries, mask):
        ws = [_sb_weights(log_b[:, hh * tk:(hh + 1) * tk], tail[:, hh * tk:(hh + 1) * tk], carries[hh],
                          mask).astype(BF16) for hh in range(2)]
        acc_sc[...] += _dot(jnp.concatenate(ws, axis=1), keys(v_ref, j))

    carries = (jnp.zeros((tq, 1), F32), jnp.zeros((tq, 1), F32))
    for dj in reversed(range(n_diag)):
        j = qi * n_diag + dj
        k0 = pl.multiple_of(j * tk, tk)
        log_b, tail, sums = logs(logits(j), k0 + k_off < q_pos)
        accumulate(j, log_b, tail, carries, k0 + k_off_head < q_pos_head)
        carries = tuple(c + s for c, s in zip(carries, sums))

    n_full = qi * n_diag
    last = jnp.maximum(n_full - 1, 0)
    z_sc[1] = logits(last)
    lb_sc[0] = jnp.full(lb_sc.shape[1:], -1e30, F32)
    tail_sc[0] = jnp.zeros(tail_sc.shape[1:], F32)

    def body(it, c):
        cur, prev = c
        for u in range(2):
            j = n_full - 1 - 2 * it - u
            z_sc[u] = logits(jnp.maximum(j - 1, 0))
            log_b, tail, sums = logs(z_sc[1 - u], None)
            accumulate(jnp.minimum(j + 1, last), lb_sc[u], tail_sc[u], prev, None)
            lb_sc[1 - u] = log_b
            tail_sc[1 - u] = tail
            prev = cur
            cur = tuple(a + s for a, s in zip(cur, sums))
        return cur, prev

    _, prev = lax.fori_loop(0, qi, body, (carries, carries))
    accumulate(0, lb_sc[0], tail_sc[0], prev, None)
    o_ref[0] = acc_sc[...].astype(o_ref.dtype)


def _attn_prompt(q, k, v, bias):
    b, t, hd = q.shape
    tq, tk = ATTN_Q_ROWS, ATTN_K_ROWS
    pair = 2 * (hd // N_HEADS)
    return pl.pallas_call(
        functools.partial(_attn_prompt_kernel, tq=tq, tk=tk),
        grid=(b, hd // pair, t // tq),
        in_specs=[pl.BlockSpec(memory_space=pltpu.SMEM),
                  pl.BlockSpec((1, tq, pair), lambda bi, hp, qi: (bi, qi, hp)),
                  pl.BlockSpec((1, t, pair), lambda bi, hp, qi: (bi, 0, hp)),
                  pl.BlockSpec((1, t, pair), lambda bi, hp, qi: (bi, 0, hp))],
        out_specs=pl.BlockSpec((1, tq, pair), lambda bi, hp, qi: (bi, qi, hp)),
        out_shape=jax.ShapeDtypeStruct((b, t, hd), BF16),
        scratch_shapes=[pltpu.VMEM((tq, pair), F32)] + [pltpu.VMEM((2, tq, 2 * tk), F32)] * 3,
        compiler_params=_params("arbitrary", "arbitrary", "arbitrary"),
        name="sb_attention_prompt",
    )(bias, q, k, v)


def _attn_sample_kernel(pt_ref, bias_ref, q_ref, kn_ref, vn_ref, *rest, page, q_pad, ppg):
    del pt_ref
    kc_refs, vc_refs = rest[:ppg], rest[ppg:2 * ppg]
    o_ref, acc_sc, carry_sc = rest[2 * ppg:]
    j = pl.program_id(1)
    heads = q_ref.shape[1]
    rows = heads * q_pad
    tri = _later_key_matrix(page)

    @pl.when(j == 0)
    def _():
        acc_sc[...] = jnp.zeros(acc_sc.shape, F32)
        carry_sc[...] = jnp.zeros(carry_sc.shape, F32)

    def process(k_ref, v_ref, masked):
        k3 = k_ref[0].astype(BF16)
        v3 = v_ref[0].astype(BF16)
        z3 = lax.dot_general(q_ref[0], k3, (((2,), (1,)), ((0,), (0,))), preferred_element_type=F32)
        z2 = z3.reshape(rows, page) + bias_ref[...]
        mask = None
        if masked:
            qidx = lax.broadcasted_iota(I32, (rows, page), 0) % q_pad
            mask = lax.broadcasted_iota(I32, (rows, page), 1) < qidx
        carry = carry_sc[...][:, 0:1]
        log_b, tail, neg_l1m = _sb_logs(z2, tri, mask)
        w = _sb_weights(log_b, tail, carry, mask)
        carry_sc[...] = jnp.broadcast_to(carry + jnp.sum(neg_l1m, axis=1, keepdims=True), carry_sc.shape)
        w3 = w.reshape(heads, q_pad, page).astype(BF16)
        acc_sc[...] += lax.dot_general(w3, v3, (((2,), (2,)), ((0,), (0,))), preferred_element_type=F32)

    @pl.when(j == 0)
    def _():
        process(kn_ref, vn_ref, True)

    @pl.when(j > 0)
    def _():
        for p in range(ppg):
            process(kc_refs[p], vc_refs[p], False)

    @pl.when(j == pl.num_programs(1) - 1)
    def _():
        o_ref[0] = acc_sc[...]


def _attn_sample(page_table, bias_rows, q3, k_new, v_new, cache_kt, cache_vt, ppg):
    bs, heads, q_pad, dh = q3.shape
    page = cache_kt.shape[-1]
    n_pages = page_table.shape[1]
    rows = heads * q_pad
    assert n_pages % ppg == 0

    def cache_spec(p):
        def index(b, j, pt):
            logical = n_pages - 1 - ((jnp.maximum(j, 1) - 1) * ppg + p)
            return (pt[b, logical], 0, 0, 0)
        return pl.BlockSpec((1, heads, dh, page), index)

    new_spec = pl.BlockSpec((1, heads, dh, page), lambda b, j, pt: (b, 0, 0, 0))
    q_spec = pl.BlockSpec((1, heads, q_pad, dh), lambda b, j, pt: (b, 0, 0, 0))
    grid_spec = pltpu.PrefetchScalarGridSpec(
        num_scalar_prefetch=1,
        grid=(bs, n_pages // ppg + 1),
        in_specs=[pl.BlockSpec((rows, page), lambda b, j, pt: (0, 0)), q_spec, new_spec, new_spec]
                 + [cache_spec(p) for p in range(ppg)] * 2,
        out_specs=q_spec,
        scratch_shapes=[pltpu.VMEM((heads, q_pad, dh), F32), pltpu.VMEM((rows, LANES), F32)],
    )
    return pl.pallas_call(
        functools.partial(_attn_sample_kernel, page=page, q_pad=q_pad, ppg=ppg),
        grid_spec=grid_spec,
        out_shape=jax.ShapeDtypeStruct((bs, heads, q_pad, dh), F32),
        compiler_params=_params("arbitrary", "arbitrary"),
        name="sb_attention_sample",
    )(page_table, bias_rows, q3, k_new, v_new, *([cache_kt] * ppg), *([cache_vt] * ppg))


def _permute_experts(a):
    per = N_EXPERTS // N_EXPERT_GROUPS
    return a.reshape(N_EXPERT_GROUPS, per, *a.shape[1:]).swapaxes(0, 1).reshape(a.shape)


def _moe_layer(groups, cgroups, xs_rows, mods, pres, g, w_router, r_bias, w_gu, w_down, ws_gu, ws_down,
               fg, final):
    d = xs_rows[0].shape[-1]
    w_rt = _permute_experts(w_router.T)
    rbias = jnp.broadcast_to(_permute_experts(r_bias.reshape(N_EXPERTS, 1)), (N_EXPERTS, LANES))
    cnt = jnp.zeros((N_EXPERTS, LANES), F32)
    routed = []
    x_new = []
    for grp, x, mod, pre in zip(groups, xs_rows, mods, pres):
        outs = _router(grp, x, mod, g, w_rt, rbias, cnt, ws_gu, ws_down, pre)
        if pre is not None:
            x, outs = outs[0], outs[1:]
        hn, shared, e_t, p_t, w_t, cnt = outs
        routed.append((hn, shared, e_t, p_t, w_t))
        x_new.append(x)
    per = N_EXPERTS // N_EXPERT_GROUPS
    counts = cnt[:, 0].astype(I32).reshape(per, N_EXPERT_GROUPS).T.reshape(N_EXPERTS)
    start = jnp.cumsum(counts) - counts
    m = sum(grp.n_rows for grp in groups) * TOP_K
    steps = _expert_steps(counts, m)
    dests = []
    expert_ids = jnp.arange(N_EXPERTS, dtype=I32)[:, None, None]
    for hn, shared, e_t, p_t, w_t in routed:
        start_sel = jnp.sum(jnp.where(e_t[None] == expert_ids, start[:, None, None], 0), axis=0)
        dests.append((start_sel + p_t).T.reshape(-1))
    row_pair = jnp.argsort(jnp.concatenate(dests)).astype(I32)
    slots = _experts(steps, [r[0] for r in routed], row_pair, w_gu, w_down)
    outs = []
    row0 = 0
    for cgrp, x, mod, (hn, shared, e_t, p_t, w_t) in zip(cgroups, x_new, mods, routed):
        outs.append(_combine(cgrp, row0, w_t.T.reshape(-1), x, shared, mod, fg, slots, final))
        row0 += cgrp.n_rows
    return outs


def kernel(x_prompt, x_sample, c_prompt, c_sample, state_conv, state_h, cache_k, cache_v, page_table, a_norm_g, a_mod_w, a_mod_b, a_w_in, a_conv_w, a_conv_b, a_w_gate_r, a_b_gate_r, a_w_gate_i, a_b_gate_i, a_lambda, a_w_out, kv_norm_g, kv_mod_w, kv_mod_b, w_kv, b_norm_g, b_mod_w, b_mod_b, b_w_q, b_sb_bias, b_w_o, m_norm_g, m_mod_w, m_mod_b, m_w_router, m_router_bias, m_w_gate_up, m_w_down, m_ws_gate_up, m_ws_down, final_norm_g):
    bp, t, d = x_prompt.shape
    bs, ts, _ = x_sample.shape
    depth = m_norm_g.shape[0]
    n_a = a_norm_g.shape[0]
    d_rnn = a_w_out.shape[1]
    heads = N_HEADS
    dh = b_w_q.shape[-1] // heads
    page = cache_k.shape[1]
    np_rows = bp * t
    ns_rows = bs * ts
    assert n_a == 1 and depth == 2, "layer pattern of this step: one self-decoder, one cross-decoder layer"

    n_c = bp + bs
    c_rows = -(-n_c // SUBLANES) * SUBLANES
    c_all = jnp.zeros((c_rows, d), F32).at[:bp].set(c_prompt).at[bp:n_c].set(c_sample)

    def mods(w, b):
        m = _modulation(c_all, w, b)
        mp = m[:bp].reshape(bp, 1, -1)
        ms = jnp.tile(m[bp:n_c], (ts, 1)).reshape(1, ns_rows, -1)
        return mp, ms

    a_mod = mods(a_mod_w[0], a_mod_b[0])
    kv_mod = mods(kv_mod_w, kv_mod_b)
    b_mod = mods(b_mod_w[0], b_mod_b[0])
    m_mod = [mods(m_mod_w[l], m_mod_b[l]) for l in range(depth)]

    bf = lambda w: w.astype(BF16)

    rec_w = (a_norm_g[0], bf(a_w_in[0]), a_conv_w[0], a_conv_b[0], bf(a_w_gate_r[0]), a_b_gate_r[0],
             bf(a_w_gate_i[0]), a_b_gate_i[0], a_lambda[0], bf(a_w_out[0]))
    hist_p = jnp.zeros((bp, CONV_W - 1, d_rnn), F32)
    h0_p = jnp.zeros((bp, 1, d_rnn), F32)
    x1_p, conv_p, h_p = _recurrent(x_prompt, a_mod[0], hist_p, h0_p, *rec_w,
                                   stride=1, steps=REC_ROWS, mod_rows=1)
    xs_tm = x_sample.swapaxes(0, 1).reshape(1, ns_rows, d)
    hist_s = state_conv[0].swapaxes(0, 1).reshape(1, (CONV_W - 1) * bs, d_rnn)
    h0_s = state_h[0].reshape(1, bs, d_rnn)
    x1_s, conv_s, h_s = _recurrent(xs_tm, a_mod[1], hist_s, h0_s, *rec_w,
                                   stride=bs, steps=ts, mod_rows=ns_rows)

    grp_p = _Group(np_rows, ROUTER_ROWS, t // ROUTER_ROWS, 1)
    grp_s = _Group(ns_rows, ns_rows, 1, ns_rows)
    groups = [grp_p, grp_s]
    cgroups = [_Group(np_rows, COMBINE_ROWS, t // COMBINE_ROWS, 1), grp_s]

    x2_p, x2_s = _moe_layer(groups, cgroups, [x1_p.reshape(np_rows, d), x1_s.reshape(ns_rows, d)],
                            [m_mod[0][0], m_mod[0][1]], [None, None], m_norm_g[0], m_w_router[0],
                            m_router_bias[0], m_w_gate_up[0], m_w_down[0], bf(m_ws_gate_up[0]),
                            bf(m_ws_down[0]), final_norm_g, False)

    q_scale = float(dh) ** -0.5 * LOG2E
    w_kv_b, w_q_b = bf(w_kv), bf(b_w_q[0])
    pgrp_p = _Group(np_rows, PROJ_ROWS, t // PROJ_ROWS, 1)
    kt_p, vt_p, kb_p, vb_p, q_p = _proj(pgrp_p, x2_p, kv_mod[0], kv_norm_g, w_kv_b, b_mod[0], b_norm_g[0],
                                        w_q_b, q_scale, n_seq=bp)
    k_s, v_s, q_s = _proj(grp_s, x2_s, kv_mod[1], kv_norm_g, w_kv_b, b_mod[1], b_norm_g[0],
                          w_q_b, q_scale)

    hd = heads * dh
    o_p = _attn_prompt(q_p.reshape(bp, t, hd), kb_p.reshape(bp, t, hd), vb_p.reshape(bp, t, hd),
                       b_sb_bias[0]).reshape(np_rows, hd)

    q_pad = SUBLANES
    q3 = jnp.zeros((bs, heads, q_pad, dh), BF16).at[:, :, :ts].set(
        q_s.reshape(ts, bs, heads, dh).transpose(1, 2, 0, 3))

    def new_keys(a):
        a = a.reshape(ts, bs, heads, dh).transpose(1, 2, 3, 0)
        return jnp.zeros((bs, heads, dh, page), F32).at[..., :ts].set(a)

    bias_rows = jnp.broadcast_to(jnp.repeat(b_sb_bias[0] * LOG2E, q_pad)[:, None], (heads * q_pad, page))
    o3 = _attn_sample(page_table, bias_rows, q3, new_keys(k_s), new_keys(v_s),
                      cache_k.transpose(0, 2, 3, 1), cache_v.transpose(0, 2, 3, 1), SAMPLE_PAGES_PER_STEP)
    o_s = o3[:, :, :ts].transpose(2, 0, 1, 3).reshape(ns_rows, hd).astype(BF16)

    w_o_b = bf(b_w_o[0])
    y_p, y_s = _moe_layer(groups, cgroups, [x2_p, x2_s], [m_mod[1][0], m_mod[1][1]],
                          [(o_p, w_o_b, b_mod[0]), (o_s, w_o_b, b_mod[1])], m_norm_g[1], m_w_router[1],
                          m_router_bias[1], m_w_gate_up[1], m_w_down[1], bf(m_ws_gate_up[1]),
                          bf(m_ws_down[1]), final_norm_g, True)

    def from_tm(a, *tail):
        return a.reshape(ts, bs, *tail).swapaxes(0, 1)

    return (y_p.reshape(bp, t, d),
            from_tm(y_s, d),
            conv_p[None],
            h_p.reshape(1, bp, d_rnn),
            kt_p.reshape(bp, heads, dh, t).transpose(0, 3, 1, 2),
            vt_p.reshape(bp, heads, dh, t).transpose(0, 3, 1, 2),
            conv_s.reshape(CONV_W - 1, bs, d_rnn).swapaxes(0, 1)[None],
            h_s.reshape(1, bs, d_rnn),
            from_tm(k_s, heads, dh),
            from_tm(v_s, heads, dh))
```

```python
import functools

import jax
import jax.numpy as jnp
from jax import lax
from jax.experimental import pallas as pl
from jax.experimental.pallas import tpu as pltpu

F32 = jnp.float32
BF16 = jnp.bfloat16
I32 = jnp.int32

CONV_W = 4
LRU_BLOCKS = 4
RGLRU_C = 8.0
N_HEADS = 16
N_EXPERTS = 64
TOP_K = 8
N_EXPERT_GROUPS = 8
TOPK_GROUPS = 4
ROUTED_SCALE = 2.5
EPS = 1e-6
NEG_INF = float("-inf")

LANES = 128
SUBLANES = 8
TILE_CHUNKS = SUBLANES
PACKED_CHUNKS = TILE_CHUNKS // 2
VMEM_LIMIT_BYTES = 56 * 1024 * 1024

REC_ROWS = 256
ROUTER_ROWS = 512
PROJ_ROWS = 512
DISPATCH_ROWS = 256
COMBINE_ROWS = 128
EXPERT_ROWS = 512
ATTN_Q_ROWS = 256
ATTN_K_ROWS = 128
SAMPLE_PAGES_PER_STEP = 4


def _params(*sem):
    return pltpu.CompilerParams(dimension_semantics=sem, vmem_limit_bytes=VMEM_LIMIT_BYTES)


def _dot(a, b):
    return jnp.dot(a, b, preferred_element_type=F32)


def _dot_nt(a, b, precision=None):
    return lax.dot_general(a, b, (((1,), (1,)), ((), ())), precision=precision,
                           preferred_element_type=F32)


def _sigmoid(x):
    return 1.0 / (1.0 + jnp.exp(-x))


def _silu(x):
    return x * _sigmoid(x)


def _gelu_tanh(x):
    return 0.5 * x * (1.0 + jnp.tanh(0.7978845608028654 * (x + 0.044715 * (x * x * x))))


def _rms(x, g):
    ms = jnp.mean(x * x, axis=-1, keepdims=True)
    return x * lax.rsqrt(ms + EPS) * g


def _ada(x, g, shift, scale):
    return _rms(x, g) * (1.0 + scale) + shift


def _swiglu(xb, w_gu, w_down):
    gu = _dot(xb, w_gu)
    half = gu.shape[-1] // 2
    act = _silu(gu[:, :half]) * gu[:, half:]
    return _dot(act.astype(BF16), w_down)


def _mod_kernel(c_ref, w_ref, b_ref, o_ref):
    c = c_ref[...]
    o_ref[...] = _dot(_silu(c).astype(BF16), w_ref[...].astype(BF16)) + b_ref[...]


def _modulation(c_all, w, b):
    rows, d = c_all.shape
    n_out = w.shape[1]
    tn = 1024
    return pl.pallas_call(
        _mod_kernel,
        grid=(n_out // tn,),
        in_specs=[pl.BlockSpec((rows, d), lambda j: (0, 0)),
                  pl.BlockSpec((d, tn), lambda j: (0, j)),
                  pl.BlockSpec((1, tn), lambda j: (0, j))],
        out_specs=pl.BlockSpec((rows, tn), lambda j: (0, j)),
        out_shape=jax.ShapeDtypeStruct((rows, n_out), F32),
        compiler_params=_params("arbitrary"),
        name="modulation",
    )(c_all, w, b.reshape(1, n_out))


class _Group:
    def __init__(self, n_rows, rows, blocks_per_seq, mod_rows):
        self.n_rows = n_rows
        self.rows = rows
        self.blocks_per_seq = blocks_per_seq
        self.mod_rows = mod_rows
        self.n_blocks = n_rows // rows

    def mod_spec(self, col, d):
        bps = self.blocks_per_seq
        return pl.BlockSpec((1, self.mod_rows, d), lambda i: (i // bps, 0, col))

    def row_spec(self, width):
        return pl.BlockSpec((self.rows, width), lambda i: (i, 0))


def _const_spec(shape):
    nd = len(shape)
    return pl.BlockSpec(shape, lambda *_: (0,) * nd)


def _scan_rows(a, u, stride, steps):
    row = lax.broadcasted_iota(I32, a.shape, 0)
    s = 1
    while s < steps:
        sh = s * stride
        a_p = pltpu.roll(a, sh, 0)
        u_p = pltpu.roll(u, sh, 0)
        m = row >= sh
        u = jnp.where(m, a * u_p + u, u)
        a = jnp.where(m, a * a_p, a)
        s *= 2
    return a, u


def _rec_kernel(x_ref, sh_ref, sc_ref, gt_ref, hist_ref, h0_ref, g_ref, w_in_ref, cw_ref, cb_ref,
                wr_ref, br_ref, wi_ref, bi_ref, lam_ref, w_out_ref,
                xo_ref, conv_ref, hl_ref, xpad_sc, h_sc, *, stride, steps, hist0):
    t = pl.program_id(1)
    rows = stride * steps
    hrows = (CONV_W - 1) * stride
    d_rnn = h_sc.shape[-1]

    @pl.when(t == 0)
    def _():
        xpad_sc[pl.ds(hist0 - hrows, hrows), :] = hist_ref[0]
        h_sc[...] = h0_ref[0]

    x = x_ref[0]
    hn = _ada(x, g_ref[...], sh_ref[0], sc_ref[0])
    proj = _dot(hn.astype(BF16), w_in_ref[...])
    gate_br = proj[:, :d_rnn]
    xb = proj[:, d_rnn:]
    xpad_sc[pl.ds(hist0, rows), :] = xb
    xc = cb_ref[...] + cw_ref[0:1, :] * xpad_sc[pl.ds(hist0 - hrows, rows), :]
    for j in range(1, CONV_W):
        xc = xc + cw_ref[j:j + 1, :] * xpad_sc[pl.ds(hist0 - hrows + j * stride, rows), :]
    new_hist = xpad_sc[pl.ds(hist0 + rows - hrows, hrows), :]
    xpad_sc[pl.ds(hist0 - hrows, hrows), :] = new_hist
    conv_ref[0] = new_hist

    bw = d_rnn // LRU_BLOCKS
    r_parts, i_parts = [], []
    for n in range(LRU_BLOCKS):
        xblk = xc[:, n * bw:(n + 1) * bw].astype(BF16)
        r_parts.append(_dot(xblk, wr_ref[n]))
        i_parts.append(_dot(xblk, wi_ref[n]))
    r = _sigmoid(jnp.concatenate(r_parts, axis=-1) + br_ref[...])
    ig = _sigmoid(jnp.concatenate(i_parts, axis=-1) + bi_ref[...])
    neg_lam = -lam_ref[...]
    softplus = jnp.maximum(neg_lam, 0.0) + jnp.log1p(jnp.exp(-jnp.abs(neg_lam)))
    log_a = (-RGLRU_C * r) * softplus
    a = jnp.exp(log_a)
    u = jnp.sqrt(-jnp.tanh(log_a) * (a * a + 1.0)) * (ig * xc)
    a_cum, u_cum = _scan_rows(a, u, stride, steps)
    h_prev = h_sc[...]
    if stride == 1:
        h = a_cum * h_prev + u_cum
    else:
        h = a_cum * jnp.concatenate([h_prev] * steps, axis=0) + u_cum
    h_last = h[rows - stride:, :]
    h_sc[...] = h_last
    hl_ref[0] = h_last
    y = _dot((_gelu_tanh(gate_br) * h).astype(BF16), w_out_ref[...])
    xo_ref[0] = x + gt_ref[0] * y


def _recurrent(x3, mod, hist, h0, g, w_in, conv_w, conv_b, w_r, b_r, w_i, b_i, lam, w_out,
               *, stride, steps, mod_rows):
    nb, t_rows, d = x3.shape
    d_rnn = w_out.shape[0]
    rows = stride * steps
    nt = t_rows // rows
    hrows = (CONV_W - 1) * stride
    hist0 = -(-hrows // SUBLANES) * SUBLANES
    assert rows >= hrows and rows % SUBLANES == 0
    x_spec = pl.BlockSpec((1, rows, d), lambda b, t: (b, t, 0))

    def mspec(col):
        if mod_rows == 1:
            return pl.BlockSpec((1, 1, d), lambda b, t: (b, 0, col))
        return pl.BlockSpec((1, rows, d), lambda b, t: (b, 0, col))

    def cspec(shape):
        nd = len(shape)
        return pl.BlockSpec(shape, lambda b, t: (0,) * nd)

    kern = functools.partial(_rec_kernel, stride=stride, steps=steps, hist0=hist0)
    return pl.pallas_call(
        kern,
        grid=(nb, nt),
        in_specs=[x_spec, mspec(0), mspec(1), mspec(2),
                  pl.BlockSpec((1, hrows, d_rnn), lambda b, t: (b, 0, 0)),
                  pl.BlockSpec((1, stride, d_rnn), lambda b, t: (b, 0, 0)),
                  cspec((1, d)), cspec(w_in.shape), cspec(conv_w.shape), cspec((1, d_rnn)),
                  cspec(w_r.shape), cspec((1, d_rnn)), cspec(w_i.shape), cspec((1, d_rnn)),
                  cspec((1, d_rnn)), cspec(w_out.shape)],
        out_specs=[x_spec,
                   pl.BlockSpec((1, hrows, d_rnn), lambda b, t: (b, 0, 0)),
                   pl.BlockSpec((1, stride, d_rnn), lambda b, t: (b, 0, 0))],
        out_shape=[jax.ShapeDtypeStruct(x3.shape, F32),
                   jax.ShapeDtypeStruct((nb, hrows, d_rnn), F32),
                   jax.ShapeDtypeStruct((nb, stride, d_rnn), F32)],
        scratch_shapes=[pltpu.VMEM((hist0 + rows, d_rnn), F32), pltpu.VMEM((stride, d_rnn), F32)],
        compiler_params=_params("arbitrary", "arbitrary"),
        name="recurrent_block",
    )(x3, mod, mod, mod, hist, h0, g.reshape(1, d), w_in, conv_w, conv_b.reshape(1, d_rnn),
      w_r, b_r.reshape(1, d_rnn), w_i, b_i.reshape(1, d_rnn), lam.reshape(1, d_rnn), w_out)


def _tree(op, xs):
    xs = list(xs)
    while len(xs) > 1:
        nxt = [op(xs[i], xs[i + 1]) for i in range(0, len(xs) - 1, 2)]
        if len(xs) % 2:
            nxt.append(xs[-1])
        xs = nxt
    return xs[0]


def _to_token_tiles(ref, x):
    rows = x.shape[0]
    for c in range(TILE_CHUNKS):
        ref[pl.ds(c, rows, stride=TILE_CHUNKS), :] = x[:, c * LANES:(c + 1) * LANES]


def _from_token_tiles(ref, rows):
    return jnp.concatenate([ref[pl.ds(c, rows, stride=TILE_CHUNKS), :] for c in range(TILE_CHUNKS)], axis=1)


HIGH_HALF = 0xFFFF0000


def _to_packed_tiles(ref, xb):
    rows, d = xb.shape
    bits = lax.bitcast_convert_type(xb.astype(F32), jnp.uint32)
    packed = bits[:, d // 2:] | (bits[:, :d // 2] >> 16)
    for c in range(PACKED_CHUNKS):
        ref[pl.ds(c, rows, stride=PACKED_CHUNKS), :] = packed[:, c * LANES:(c + 1) * LANES]


def _from_packed_tiles(ref, rows):
    words = [ref[pl.ds(c, rows, stride=PACKED_CHUNKS), :] for c in range(PACKED_CHUNKS)]
    low = [lax.bitcast_convert_type(w << 16, F32) for w in words]
    high = [lax.bitcast_convert_type(w & jnp.uint32(HIGH_HALF), F32) for w in words]
    return jnp.concatenate(low + high, axis=1).astype(BF16)


def _router_kernel(*refs, pre):
    if pre:
        (x_ref, o_ref, wo_ref, agt_ref, sh_ref, sc_ref, g_ref, wrt_ref, rb_ref, cnt_in_ref, wsgu_ref, wsd_ref,
         xo_ref, hn_ref, shared_ref, e_ref, p_ref, w_ref, cnt_ref, run_sc) = refs
    else:
        (x_ref, sh_ref, sc_ref, g_ref, wrt_ref, rb_ref, cnt_in_ref, wsgu_ref, wsd_ref,
         hn_ref, shared_ref, e_ref, p_ref, w_ref, cnt_ref, run_sc) = refs
    i = pl.program_id(0)
    per = N_EXPERTS // N_EXPERT_GROUPS

    @pl.when(i == 0)
    def _():
        run_sc[...] = cnt_in_ref[...]

    x = x_ref[...]
    if pre:
        x = x + agt_ref[0] * _dot(o_ref[...], wo_ref[...])
        xo_ref[...] = x
    hn = _ada(x, g_ref[...], sh_ref[0], sc_ref[0])
    hb = hn.astype(BF16)
    _to_packed_tiles(hn_ref, hb)
    shared_ref[...] = _swiglu(hb, wsgu_ref[...], wsd_ref[...])
    tb = hn.shape[0]
    logits = _dot_nt(wrt_ref[...], hn, precision=lax.Precision.HIGHEST)
    scores = _sigmoid(logits)
    sel = scores + rb_ref[...][:, 0:1]
    sel_j = [sel[j * SUBLANES:(j + 1) * SUBLANES, :] for j in range(per)]
    sc_j = [scores[j * SUBLANES:(j + 1) * SUBLANES, :] for j in range(per)]
    m1 = _tree(jnp.maximum, sel_j)
    j1 = _tree(jnp.minimum, [jnp.where(sel_j[j] == m1, j, per) for j in range(per)])
    m2 = _tree(jnp.maximum, [jnp.where(j1 == j, NEG_INF, sel_j[j]) for j in range(per)])
    gs = m1 + m2
    g_iota = lax.broadcasted_iota(I32, gs.shape, 0)
    rank = jnp.zeros(gs.shape, I32)
    for dlt in range(1, N_EXPERT_GROUPS):
        other = pltpu.roll(gs, dlt, 0)
        tie = jnp.where(g_iota >= dlt, 1, 0)
        rank = rank + jnp.where(other > gs, 1, jnp.where(other == gs, tie, 0))
    gmask = rank < TOPK_GROUPS
    masked = [jnp.where(gmask, s, NEG_INF) for s in sel_j]
    eid = [g_iota * per + j for j in range(per)]
    chosen = [jnp.zeros(gs.shape, F32) for _ in range(per)]
    e_rows, s_rows = [], []
    for k in range(TOP_K):
        m = jnp.max(_tree(jnp.maximum, masked), axis=0, keepdims=True)
        cand = _tree(jnp.minimum, [jnp.where(masked[j] == m, eid[j], N_EXPERTS) for j in range(per)])
        ek = jnp.min(cand, axis=0, keepdims=True)
        sk = jnp.zeros(gs.shape, F32)
        for j in range(per):
            oh = eid[j] == ek
            masked[j] = jnp.where(oh, NEG_INF, masked[j])
            chosen[j] = jnp.where(oh, 1.0, chosen[j])
            sk = sk + jnp.where(oh, sc_j[j], 0.0)
        e_rows.append(ek)
        s_rows.append(jnp.sum(sk, axis=0, keepdims=True))
    denom = _tree(lambda p, q: p + q, s_rows)
    sel_mask = jnp.concatenate(chosen, axis=0)
    upper = (lax.broadcasted_iota(I32, (tb, tb), 0) < lax.broadcasted_iota(I32, (tb, tb), 1))
    prefix = _dot(sel_mask.astype(BF16), jnp.where(upper, 1.0, 0.0).astype(BF16)) + run_sc[...][:, 0:1]
    pre_j = [prefix[j * SUBLANES:(j + 1) * SUBLANES, :] for j in range(per)]
    for k in range(TOP_K):
        pk = jnp.zeros(gs.shape, F32)
        for j in range(per):
            pk = pk + jnp.where(eid[j] == e_rows[k], pre_j[j], 0.0)
        e_ref[pl.ds(k, 1), :] = e_rows[k]
        p_ref[pl.ds(k, 1), :] = jnp.sum(pk, axis=0, keepdims=True).astype(I32)
        w_ref[pl.ds(k, 1), :] = s_rows[k] / denom * ROUTED_SCALE
    run = run_sc[...] + jnp.sum(sel_mask, axis=1, keepdims=True)
    run_sc[...] = run
    cnt_ref[...] = run


def _router(grp, x, mod, g, w_rt, rbias, cnt_in, ws_gu, ws_down, pre=None):
    d = x.shape[-1]
    assert d == TILE_CHUNKS * LANES
    n, tb = grp.n_rows, grp.rows
    ins = [x]
    in_specs = [grp.row_spec(d)]
    if pre is not None:
        o, w_o, amod = pre
        ins += [o, w_o, amod]
        in_specs += [grp.row_spec(o.shape[-1]), _const_spec(w_o.shape), grp.mod_spec(2, d)]
    ins += [mod, mod, g.reshape(1, d), w_rt, rbias, cnt_in, ws_gu, ws_down]
    in_specs += [grp.mod_spec(0, d), grp.mod_spec(1, d), _const_spec((1, d)), _const_spec(w_rt.shape),
                 _const_spec(rbias.shape), _const_spec(cnt_in.shape), _const_spec(ws_gu.shape),
                 _const_spec(ws_down.shape)]
    tok_spec = pl.BlockSpec((TOP_K, tb), lambda i: (0, i))
    out_specs = [pl.BlockSpec((tb * PACKED_CHUNKS, LANES), lambda i: (i, 0)), grp.row_spec(d),
                 tok_spec, tok_spec, tok_spec, _const_spec(cnt_in.shape)]
    out_shape = [jax.ShapeDtypeStruct((n * PACKED_CHUNKS, LANES), jnp.uint32), jax.ShapeDtypeStruct((n, d), F32),
                 jax.ShapeDtypeStruct((TOP_K, n), I32),
                 jax.ShapeDtypeStruct((TOP_K, n), I32), jax.ShapeDtypeStruct((TOP_K, n), F32),
                 jax.ShapeDtypeStruct(cnt_in.shape, F32)]
    if pre is not None:
        out_specs = [grp.row_spec(d)] + out_specs
        out_shape = [jax.ShapeDtypeStruct((n, d), F32)] + out_shape
    return pl.pallas_call(
        functools.partial(_router_kernel, pre=pre is not None),
        grid=(grp.n_blocks,),
        in_specs=in_specs, out_specs=out_specs, out_shape=out_shape,
        scratch_shapes=[pltpu.VMEM(cnt_in.shape, F32)],
        compiler_params=_params("arbitrary"),
        name="router",
    )(*ins)


def _experts_kernel(blk_ref, exp_ref, lo_ref, hi_ref, first_ref, last_ref, newe_ref, *refs, n_src):
    del exp_ref
    pair_ref = refs[0]
    src_refs = refs[1:1 + n_src]
    (wgu_ref, wd_ref, slots_ref, wgu_sc, wd_sc, x_sc, y_sc, out_sc, tok_sc, xg_sc, sems, load_sem
     ) = refs[1 + n_src:]
    s = pl.program_id(0)
    lo = lo_ref[s]
    hi = hi_ref[s]
    rb = x_sc.shape[0]
    blk = blk_ref[s]

    @pl.when(s == 0)
    def _():
        row = 0
        copies = []
        for src in src_refs:
            copies.append(pltpu.make_async_copy(src, tok_sc.at[pl.ds(row, src.shape[0])], load_sem))
            row += src.shape[0]
        for cp in copies:
            cp.start()
        for cp in copies:
            cp.wait()

    def packed_tile(ref, r):
        return ref.at[pl.ds(pl.multiple_of(r * PACKED_CHUNKS, PACKED_CHUNKS), PACKED_CHUNKS)]

    def tile(ref, r):
        return ref.at[pl.ds(pl.multiple_of(r * TILE_CHUNKS, TILE_CHUNKS), TILE_CHUNKS)]

    def row_copy(slot, r):
        return pltpu.make_async_copy(tile(out_sc.at[slot], r), tile(slots_ref, pair_ref[0, 0, r]),
                                     sems.at[slot])

    def drain(slot):
        def body(r, c):
            row_copy(slot, r).wait()
            return c
        lax.fori_loop(0, rb, body, 0, unroll=8)

    @pl.when(newe_ref[s] == 1)
    def _():
        wgu_sc[...] = wgu_ref[0].astype(BF16)
        wd_sc[...] = wd_ref[0].astype(BF16)

    @pl.when(first_ref[s] == 1)
    def _():
        def gather(r, c):
            tok = lax.shift_right_logical(pair_ref[0, 0, r], TOP_K.bit_length() - 1)
            packed_tile(xg_sc, r)[...] = packed_tile(tok_sc, tok)[...]
            return c
        lax.fori_loop(0, rb, gather, 0, unroll=8)
        x_sc[...] = _from_packed_tiles(xg_sc, rb)
        y_sc[...] = jnp.zeros(y_sc.shape, F32)

    @pl.when(hi > lo)
    def _():
        gu = _dot(x_sc[...], wgu_sc[...])
        half = gu.shape[-1] // 2
        act = _silu(gu[:, :half]) * gu[:, half:]
        row = lax.broadcasted_iota(I32, act.shape, 0)
        act = jnp.where((row >= lo) & (row < hi), act, 0.0)
        y_sc[...] += _dot(act.astype(BF16), wd_sc[...])

    @pl.when(last_ref[s] == 1)
    def _():
        for slot in range(2):
            @pl.when(blk % 2 == slot)
            def _():
                @pl.when(blk >= 2)
                def _():
                    drain(slot)

                _to_token_tiles(out_sc.at[slot], y_sc[...])

                def issue(r, c):
                    row_copy(slot, r).start()
                    return c
                lax.fori_loop(0, rb, issue, 0, unroll=8)

    @pl.when(s == pl.num_programs(0) - 1)
    def _():
        for slot in range(2):
            @pl.when((blk % 2 == slot) | (blk >= 1))
            def _():
                drain(slot)


def _experts(steps, packed_rows, row_pair, w_gu, w_down):
    n_tok = sum(p.shape[0] for p in packed_rows) // PACKED_CHUNKS
    m = n_tok * TOP_K
    d = TILE_CHUNKS * LANES
    rb = EXPERT_ROWS
    n_steps = steps[0].shape[0]
    de2 = w_gu.shape[-1]
    grid_spec = pltpu.PrefetchScalarGridSpec(
        num_scalar_prefetch=7,
        grid=(n_steps,),
        in_specs=[pl.BlockSpec((1, 1, rb), lambda s, blk, ex, *_: (blk[s], 0, 0), memory_space=pltpu.SMEM)]
                 + [pl.BlockSpec(memory_space=pl.ANY)] * len(packed_rows)
                 + [pl.BlockSpec((1, d, de2), lambda s, blk, ex, *_: (ex[s], 0, 0)),
                    pl.BlockSpec((1, de2 // 2, d), lambda s, blk, ex, *_: (ex[s], 0, 0))],
        out_specs=pl.BlockSpec(memory_space=pl.ANY),
        scratch_shapes=[pltpu.VMEM((d, de2), BF16), pltpu.VMEM((de2 // 2, d), BF16),
                        pltpu.VMEM((rb, d), BF16), pltpu.VMEM((rb, d), F32),
                        pltpu.VMEM((2, rb * TILE_CHUNKS, LANES), F32),
                        pltpu.VMEM((n_tok * PACKED_CHUNKS, LANES), jnp.uint32),
                        pltpu.VMEM((rb * PACKED_CHUNKS, LANES), jnp.uint32),
                        pltpu.SemaphoreType.DMA((2,)), pltpu.SemaphoreType.DMA(())],
    )
    return pl.pallas_call(
        functools.partial(_experts_kernel, n_src=len(packed_rows)),
        grid_spec=grid_spec,
        out_shape=jax.ShapeDtypeStruct((m * TILE_CHUNKS, LANES), F32),
        compiler_params=_params("arbitrary"),
        name="routed_experts",
    )(*steps, row_pair.reshape(m // rb, 1, rb), *packed_rows, w_gu, w_down)


def _expert_steps(counts, m):
    rb = EXPERT_ROWS
    nb = m // rb
    start = jnp.cumsum(counts) - counts
    end = start + counts
    cuts = jnp.sort(jnp.concatenate([jnp.arange(nb, dtype=I32) * rb, start.astype(I32)]))
    nxt = jnp.concatenate([cuts[1:], jnp.array([m], I32)])
    blk = jnp.minimum(cuts // rb, nb - 1)
    ex = jnp.minimum(jnp.sum((end[None, :] <= cuts[:, None]).astype(I32), axis=1), N_EXPERTS - 1)
    lo = cuts - blk * rb
    hi = nxt - blk * rb
    changes = (blk[1:] != blk[:-1]).astype(I32)
    first = jnp.concatenate([jnp.ones((1,), I32), changes])
    last = jnp.concatenate([changes, jnp.ones((1,), I32)])
    newe = jnp.concatenate([jnp.ones((1,), I32), (ex[1:] != ex[:-1]).astype(I32)])
    return blk, ex, lo, hi, first, last, newe


def _combine_kernel(ew_ref, x_ref, shared_ref, gt_ref, fg_ref, slots_ref, o_ref, routed_sc, *, rows, final):
    def weigh(r, c):
        acc = None
        for k in range(TOP_K):
            p0 = pl.multiple_of((r * TOP_K + k) * TILE_CHUNKS, TILE_CHUNKS)
            term = ew_ref[r * TOP_K + k] * slots_ref[pl.ds(p0, TILE_CHUNKS), :]
            acc = term if acc is None else acc + term
        routed_sc[pl.ds(pl.multiple_of(r * TILE_CHUNKS, TILE_CHUNKS), TILE_CHUNKS), :] = acc
        return c

    lax.fori_loop(0, rows, weigh, 0, unroll=2)
    routed = _from_token_tiles(routed_sc, rows)
    x = x_ref[...] + gt_ref[0] * (routed + shared_ref[...])
    if final:
        x = _rms(x, fg_ref[...])
    o_ref[...] = x


def _combine(grp, row0, ew, x, shared, mod, fg, slots, final):
    d = x.shape[-1]
    rows = grp.rows
    blk0 = row0 // rows
    assert row0 % rows == 0
    return pl.pallas_call(
        functools.partial(_combine_kernel, rows=rows, final=final),
        grid=(grp.n_blocks,),
        in_specs=[pl.BlockSpec((rows * TOP_K,), lambda i: (i,), memory_space=pltpu.SMEM),
                  grp.row_spec(d), grp.row_spec(d), grp.mod_spec(2, d), _const_spec((1, d)),
                  pl.BlockSpec((rows * TOP_K * TILE_CHUNKS, LANES), lambda i: (i + blk0, 0))],
        out_specs=grp.row_spec(d),
        out_shape=jax.ShapeDtypeStruct(x.shape, F32),
        scratch_shapes=[pltpu.VMEM((rows * TILE_CHUNKS, LANES), F32)],
        compiler_params=_params("arbitrary"),
        name="combine",
    )(ew, x, shared, mod, fg.reshape(1, d), slots)


def _proj_kernel(*refs, q_scale, transposed):
    if transposed:
        (x_ref, ksh_ref, ksc_ref, kg_ref, wkv_ref, wkvt_ref, bsh_ref, bsc_ref, bg_ref, wq_ref,
         k_ref, v_ref, kb_ref, vb_ref, q_ref) = refs
    else:
        (x_ref, ksh_ref, ksc_ref, kg_ref, wkv_ref, bsh_ref, bsc_ref, bg_ref, wq_ref,
         k_ref, v_ref, q_ref) = refs
    x = x_ref[...]
    kvn = _ada(x, kg_ref[...], ksh_ref[0], ksc_ref[0]).astype(BF16)
    kv = _dot(kvn, wkv_ref[...])
    half = kv.shape[-1] // 2
    if transposed:
        kvt = _dot_nt(wkvt_ref[...], kvn)
        k_ref[0] = kvt[:half]
        v_ref[0] = kvt[half:]
        kb_ref[...] = kv[:, :half].astype(BF16)
        vb_ref[...] = kv[:, half:].astype(BF16)
    else:
        k_ref[...] = kv[:, :half]
        v_ref[...] = kv[:, half:]
    hn = _ada(x, bg_ref[...], bsh_ref[0], bsc_ref[0])
    q_ref[...] = (_dot(hn.astype(BF16), wq_ref[...]) * q_scale).astype(BF16)


def _proj(grp, x, kv_mod, kv_g, w_kv, b_mod, b_g, w_q, q_scale, n_seq=None):
    d = x.shape[-1]
    n = grp.n_rows
    hd = w_q.shape[-1]
    transposed = n_seq is not None
    ins = [x, kv_mod, kv_mod, kv_g.reshape(1, d), w_kv]
    in_specs = [grp.row_spec(d), grp.mod_spec(0, d), grp.mod_spec(1, d), _const_spec((1, d)),
                _const_spec(w_kv.shape)]
    if transposed:
        ins.append(w_kv.T)
        in_specs.append(_const_spec((w_kv.shape[1], w_kv.shape[0])))
        bps = grp.blocks_per_seq
        t_spec = pl.BlockSpec((1, hd, grp.rows), lambda i: (i // bps, 0, i % bps))
        out_specs = [t_spec, t_spec, grp.row_spec(hd), grp.row_spec(hd), grp.row_spec(hd)]
        t_shape = jax.ShapeDtypeStruct((n_seq, hd, n // n_seq), F32)
        out_shape = [t_shape, t_shape] + [jax.ShapeDtypeStruct((n, hd), BF16)] * 3
    else:
        out_specs = [grp.row_spec(hd)] * 3
        out_shape = [jax.ShapeDtypeStruct((n, hd), F32), jax.ShapeDtypeStruct((n, hd), F32),
                     jax.ShapeDtypeStruct((n, hd), BF16)]
    ins += [b_mod, b_mod, b_g.reshape(1, d), w_q]
    in_specs += [grp.mod_spec(0, d), grp.mod_spec(1, d), _const_spec((1, d)), _const_spec(w_q.shape)]
    return pl.pallas_call(
        functools.partial(_proj_kernel, q_scale=q_scale, transposed=transposed),
        grid=(grp.n_blocks,),
        in_specs=in_specs, out_specs=out_specs, out_shape=out_shape,
        compiler_params=_params("arbitrary"),
        name="kv_q_projection",
    )(*ins)


def _later_key_matrix(tk, n_heads=1):
    size = n_heads * tk
    j = lax.broadcasted_iota(I32, (size, size), 0)
    s = lax.broadcasted_iota(I32, (size, size), 1)
    later = jnp.where(j % tk > s % tk, 1.0, 0.0)
    return jnp.where(j // tk == s // tk, later, 0.0).astype(BF16)


LOG2E = 1.4426950408889634
SIGN_BIT = 0x80000000
MASKED_LOGIT = -1e30


def _sb_logs(z2, tri, mask):
    neg_abs = lax.bitcast_convert_type(lax.bitcast_convert_type(z2, jnp.uint32) | jnp.uint32(SIGN_BIT), F32)
    neg_l1m = jnp.maximum(z2, 0.0) + jnp.log2(1.0 + jnp.exp2(neg_abs))
    log_b = z2 - neg_l1m
    if mask is not None:
        neg_l1m = jnp.where(mask, neg_l1m, 0.0)
    return log_b, _dot(neg_l1m.astype(BF16), tri), neg_l1m


def _sb_weights(log_b, tail, carry, mask):
    w = jnp.exp2(log_b - tail - carry)
    if mask is not None:
        w = jnp.where(mask, w, 0.0)
    return w


def _attn_prompt_kernel(bias_ref, q_ref, k_ref, v_ref, o_ref, acc_sc, z_sc, lb_sc, tail_sc, bias_sc,
                        *, tq, tk):
    hp = pl.program_id(1)
    qi = pl.program_id(2)
    q = q_ref[0]
    half = q.shape[-1] // 2
    tri = _later_key_matrix(tk, 2)
    n_diag = tq // tk
    assert n_diag == 2, "the block pipeline below alternates two slots per loop trip"
    first_head = lax.broadcasted_iota(I32, (tk, q.shape[-1]), 1) < half
    acc_sc[...] = jnp.zeros(acc_sc.shape, F32)

    @pl.when(qi == 0)
    def _():
        pair_lane = lax.broadcasted_iota(I32, (tq, 2 * tk), 1)
        row = lax.broadcasted_iota(I32, (tq, 2 * tk), 0)
        bias2 = jnp.where(pair_lane < tk, bias_ref[hp * 2] * LOG2E, bias_ref[hp * 2 + 1] * LOG2E)
        bias_sc[0] = jnp.where(tk + pair_lane % tk < row, bias2, MASKED_LOGIT)
        bias_sc[1] = jnp.where(pair_lane % tk < row, bias2, MASKED_LOGIT)
        bias_sc[2] = bias2

    def split_heads(a):
        zero = jnp.zeros_like(a)
        return jnp.concatenate([jnp.where(first_head, a, zero), jnp.where(first_head, zero, a)], axis=0)

    def keys(ref, j):
        return split_heads(ref[0, pl.ds(pl.multiple_of(j * tk, tk), tk), :])

    def logits(j):
        return _dot_nt(q, keys(k_ref, j))

    def logs(z_pair, bias):
        log_b, tail, neg_l1m = _sb_logs(z_pair + bias, tri, None)
        sums = tuple(jnp.sum(neg_l1m[:, hh * tk:(hh + 1) * tk], axis=1, keepdims=True) for hh in range(2))
        return log_b, tail, sums

    def accumulate(j, log_b, tail, carries):
        ws = [_sb_weights(log_b[:, hh * tk:(hh + 1) * tk], tail[:, hh * tk:(hh + 1) * tk], carries[hh],
                          None).astype(BF16) for hh in range(2)]
        acc_sc[...] += _dot(jnp.concatenate(ws, axis=1), keys(v_ref, j))

    last = (qi + 1) * n_diag - 1
    z_sc[1] = logits(last)
    lb_sc[0] = jnp.full(lb_sc.shape[1:], MASKED_LOGIT, F32)
    tail_sc[0] = jnp.zeros(tail_sc.shape[1:], F32)

    def body(it, c):
        cur, prev = c
        for u in range(2):
            j = last - 2 * it - u
            z_sc[u] = logits(jnp.maximum(j - 1, 0))
            log_b, tail, sums = logs(z_sc[1 - u], bias_sc[jnp.where(it == 0, u, 2)])
            accumulate(jnp.minimum(j + 1, last), lb_sc[u], tail_sc[u], prev)
            lb_sc[1 - u] = log_b
            tail_sc[1 - u] = tail
            prev = cur
            cur = tuple(a + s for a, s in zip(cur, sums))
        return cur, prev

    zero = jnp.zeros((tq, 1), F32)
    _, prev = lax.fori_loop(0, qi + 1, body, ((zero, zero), (zero, zero)))
    accumulate(0, lb_sc[0], tail_sc[0], prev)
    o_ref[0] = acc_sc[...].astype(o_ref.dtype)


def _attn_prompt(q, k, v, bias):
    b, t, hd = q.shape
    tq, tk = ATTN_Q_ROWS, ATTN_K_ROWS
    pair = 2 * (hd // N_HEADS)
    return pl.pallas_call(
        functools.partial(_attn_prompt_kernel, tq=tq, tk=tk),
        grid=(b, hd // pair, t // tq),
        in_specs=[pl.BlockSpec(memory_space=pltpu.SMEM),
                  pl.BlockSpec((1, tq, pair), lambda bi, hp, qi: (bi, qi, hp)),
                  pl.BlockSpec((1, t, pair), lambda bi, hp, qi: (bi, 0, hp)),
                  pl.BlockSpec((1, t, pair), lambda bi, hp, qi: (bi, 0, hp))],
        out_specs=pl.BlockSpec((1, tq, pair), lambda bi, hp, qi: (bi, qi, hp)),
        out_shape=jax.ShapeDtypeStruct((b, t, hd), BF16),
        scratch_shapes=[pltpu.VMEM((tq, pair), F32)] + [pltpu.VMEM((2, tq, 2 * tk), F32)] * 3
                       + [pltpu.VMEM((3, tq, 2 * tk), F32)],
        compiler_params=_params("arbitrary", "arbitrary", "arbitrary"),
        name="sb_attention_prompt",
    )(bias, q, k, v)


def _attn_sample_kernel(pt_ref, bias_ref, q_ref, kn_ref, vn_ref, *rest, page, q_pad, ppg):
    del pt_ref
    kc_refs, vc_refs = rest[:ppg], rest[ppg:2 * ppg]
    o_ref, acc_sc, carry_sc, tri_sc = rest[2 * ppg:]
    j = pl.program_id(1)
    heads = q_ref.shape[1]
    rows = heads * q_pad

    @pl.when((pl.program_id(0) == 0) & (j == 0))
    def _():
        n = tri_sc.shape[0]
        later = lax.broadcasted_iota(I32, (n, n), 0) > lax.broadcasted_iota(I32, (n, n), 1)
        tri_sc[...] = jnp.where(later, 1.0, 0.0).astype(BF16)

    @pl.when(j == 0)
    def _():
        acc_sc[...] = jnp.zeros(acc_sc.shape, F32)
        carry_sc[...] = jnp.zeros(carry_sc.shape, F32)

    def process(k_refs, v_refs, masked):
        k3 = jnp.concatenate([r[0] for r in k_refs], axis=2).astype(BF16)
        v3 = jnp.concatenate([r[0] for r in v_refs], axis=2).astype(BF16)
        keys = k3.shape[2]
        z3 = lax.dot_general(q_ref[0], k3, (((2,), (1,)), ((0,), (0,))), preferred_element_type=F32)
        z2 = z3.reshape(rows, keys) + bias_ref[...][:, 0:1]
        mask = None
        if masked:
            qidx = lax.broadcasted_iota(I32, (rows, keys), 0) % q_pad
            mask = lax.broadcasted_iota(I32, (rows, keys), 1) < qidx
        carry = carry_sc[...][:, 0:1]
        log_b, tail, neg_l1m = _sb_logs(z2, tri_sc[0:keys, 0:keys], mask)
        w = _sb_weights(log_b, tail, carry, mask)
        carry_sc[...] = jnp.broadcast_to(carry + jnp.sum(neg_l1m, axis=1, keepdims=True), carry_sc.shape)
        w3 = w.reshape(heads, q_pad, keys).astype(BF16)
        acc_sc[...] += lax.dot_general(w3, v3, (((2,), (2,)), ((0,), (0,))), preferred_element_type=F32)

    @pl.when(j == 0)
    def _():
        process([kn_ref], [vn_ref], True)

    @pl.when(j > 0)
    def _():
        process(kc_refs[::-1], vc_refs[::-1], False)

    @pl.when(j == pl.num_programs(1) - 1)
    def _():
        o_ref[0] = acc_sc[...]


def _attn_sample(page_table, bias_rows, q3, k_new, v_new, cache_kt, cache_vt, ppg):
    bs, heads, q_pad, dh = q3.shape
    page = cache_kt.shape[-1]
    n_pages = page_table.shape[1]
    rows = heads * q_pad
    assert n_pages % ppg == 0

    def cache_spec(p):
        def index(b, j, pt):
            logical = n_pages - 1 - ((jnp.maximum(j, 1) - 1) * ppg + p)
            return (pt[b, logical], 0, 0, 0)
        return pl.BlockSpec((1, heads, dh, page), index)

    new_spec = pl.BlockSpec((1, heads, dh, page), lambda b, j, pt: (b, 0, 0, 0))
    q_spec = pl.BlockSpec((1, heads, q_pad, dh), lambda b, j, pt: (b, 0, 0, 0))
    grid_spec = pltpu.PrefetchScalarGridSpec(
        num_scalar_prefetch=1,
        grid=(bs, n_pages // ppg + 1),
        in_specs=[pl.BlockSpec((rows, page), lambda b, j, pt: (0, 0)), q_spec, new_spec, new_spec]
                 + [cache_spec(p) for p in range(ppg)] * 2,
        out_specs=q_spec,
        scratch_shapes=[pltpu.VMEM((heads, q_pad, dh), F32), pltpu.VMEM((rows, LANES), F32),
                        pltpu.VMEM((ppg * page, ppg * page), BF16)],
    )
    return pl.pallas_call(
        functools.partial(_attn_sample_kernel, page=page, q_pad=q_pad, ppg=ppg),
        grid_spec=grid_spec,
        out_shape=jax.ShapeDtypeStruct((bs, heads, q_pad, dh), F32),
        compiler_params=_params("arbitrary", "arbitrary"),
        name="sb_attention_sample",
    )(page_table, bias_rows, q3, k_new, v_new, *([cache_kt] * ppg), *([cache_vt] * ppg))


def _permute_experts(a):
    per = N_EXPERTS // N_EXPERT_GROUPS
    return a.reshape(N_EXPERT_GROUPS, per, *a.shape[1:]).swapaxes(0, 1).reshape(a.shape)


def _moe_layer(groups, cgroups, xs_rows, mods, pres, g, w_router, r_bias, w_gu, w_down, ws_gu, ws_down,
               fg, final):
    d = xs_rows[0].shape[-1]
    w_rt = _permute_experts(w_router.T)
    rbias = jnp.broadcast_to(_permute_experts(r_bias.reshape(N_EXPERTS, 1)), (N_EXPERTS, LANES))
    cnt = jnp.zeros((N_EXPERTS, LANES), F32)
    routed = []
    x_new = []
    for grp, x, mod, pre in zip(groups, xs_rows, mods, pres):
        outs = _router(grp, x, mod, g, w_rt, rbias, cnt, ws_gu, ws_down, pre)
        if pre is not None:
            x, outs = outs[0], outs[1:]
        hn, shared, e_t, p_t, w_t, cnt = outs
        routed.append((hn, shared, e_t, p_t, w_t))
        x_new.append(x)
    per = N_EXPERTS // N_EXPERT_GROUPS
    counts = cnt[:, 0].astype(I32).reshape(per, N_EXPERT_GROUPS).T.reshape(N_EXPERTS)
    start = jnp.cumsum(counts) - counts
    m = sum(grp.n_rows for grp in groups) * TOP_K
    steps = _expert_steps(counts, m)
    dests = []
    expert_ids = jnp.arange(N_EXPERTS, dtype=I32)[:, None, None]
    for hn, shared, e_t, p_t, w_t in routed:
        start_sel = jnp.sum(jnp.where(e_t[None] == expert_ids, start[:, None, None], 0), axis=0)
        dests.append((start_sel + p_t).T.reshape(-1))
    row_pair = jnp.argsort(jnp.concatenate(dests)).astype(I32)
    slots = _experts(steps, [r[0] for r in routed], row_pair, w_gu, w_down)
    outs = []
    row0 = 0
    for cgrp, x, mod, (hn, shared, e_t, p_t, w_t) in zip(cgroups, x_new, mods, routed):
        outs.append(_combine(cgrp, row0, w_t.T.reshape(-1), x, shared, mod, fg, slots, final))
        row0 += cgrp.n_rows
    return outs


def kernel(x_prompt, x_sample, c_prompt, c_sample, state_conv, state_h, cache_k, cache_v, page_table, a_norm_g, a_mod_w, a_mod_b, a_w_in, a_conv_w, a_conv_b, a_w_gate_r, a_b_gate_r, a_w_gate_i, a_b_gate_i, a_lambda, a_w_out, kv_norm_g, kv_mod_w, kv_mod_b, w_kv, b_norm_g, b_mod_w, b_mod_b, b_w_q, b_sb_bias, b_w_o, m_norm_g, m_mod_w, m_mod_b, m_w_router, m_router_bias, m_w_gate_up, m_w_down, m_ws_gate_up, m_ws_down, final_norm_g):
    bp, t, d = x_prompt.shape
    bs, ts, _ = x_sample.shape
    depth = m_norm_g.shape[0]
    n_a = a_norm_g.shape[0]
    d_rnn = a_w_out.shape[1]
    heads = N_HEADS
    dh = b_w_q.shape[-1] // heads
    page = cache_k.shape[1]
    np_rows = bp * t
    ns_rows = bs * ts
    assert n_a == 1 and depth == 2, "layer pattern of this step: one self-decoder, one cross-decoder layer"

    n_c = bp + bs
    c_rows = -(-n_c // SUBLANES) * SUBLANES
    c_all = jnp.zeros((c_rows, d), F32).at[:bp].set(c_prompt).at[bp:n_c].set(c_sample)

    def mods(w, b):
        m = _modulation(c_all, w, b)
        mp = m[:bp].reshape(bp, 1, -1)
        ms = jnp.tile(m[bp:n_c], (ts, 1)).reshape(1, ns_rows, -1)
        return mp, ms

    a_mod = mods(a_mod_w[0], a_mod_b[0])
    kv_mod = mods(kv_mod_w, kv_mod_b)
    b_mod = mods(b_mod_w[0], b_mod_b[0])
    m_mod = [mods(m_mod_w[l], m_mod_b[l]) for l in range(depth)]

    bf = lambda w: w.astype(BF16)

    rec_w = (a_norm_g[0], bf(a_w_in[0]), a_conv_w[0], a_conv_b[0], bf(a_w_gate_r[0]), a_b_gate_r[0],
             bf(a_w_gate_i[0]), a_b_gate_i[0], a_lambda[0], bf(a_w_out[0]))
    hist_p = jnp.zeros((bp, CONV_W - 1, d_rnn), F32)
    h0_p = jnp.zeros((bp, 1, d_rnn), F32)
    x1_p, conv_p, h_p = _recurrent(x_prompt, a_mod[0], hist_p, h0_p, *rec_w,
                                   stride=1, steps=REC_ROWS, mod_rows=1)
    xs_tm = x_sample.swapaxes(0, 1).reshape(1, ns_rows, d)
    hist_s = state_conv[0].swapaxes(0, 1).reshape(1, (CONV_W - 1) * bs, d_rnn)
    h0_s = state_h[0].reshape(1, bs, d_rnn)
    x1_s, conv_s, h_s = _recurrent(xs_tm, a_mod[1], hist_s, h0_s, *rec_w,
                                   stride=bs, steps=ts, mod_rows=ns_rows)

    grp_p = _Group(np_rows, ROUTER_ROWS, t // ROUTER_ROWS, 1)
    grp_s = _Group(ns_rows, ns_rows, 1, ns_rows)
    groups = [grp_p, grp_s]
    cgroups = [_Group(np_rows, COMBINE_ROWS, t // COMBINE_ROWS, 1), grp_s]

    x2_p, x2_s = _moe_layer(groups, cgroups, [x1_p.reshape(np_rows, d), x1_s.reshape(ns_rows, d)],
                            [m_mod[0][0], m_mod[0][1]], [None, None], m_norm_g[0], m_w_router[0],
                            m_router_bias[0], m_w_gate_up[0], m_w_down[0], bf(m_ws_gate_up[0]),
                            bf(m_ws_down[0]), final_norm_g, False)

    q_scale = float(dh) ** -0.5 * LOG2E
    w_kv_b, w_q_b = bf(w_kv), bf(b_w_q[0])
    pgrp_p = _Group(np_rows, PROJ_ROWS, t // PROJ_ROWS, 1)
    kt_p, vt_p, kb_p, vb_p, q_p = _proj(pgrp_p, x2_p, kv_mod[0], kv_norm_g, w_kv_b, b_mod[0], b_norm_g[0],
                                        w_q_b, q_scale, n_seq=bp)
    k_s, v_s, q_s = _proj(grp_s, x2_s, kv_mod[1], kv_norm_g, w_kv_b, b_mod[1], b_norm_g[0],
                          w_q_b, q_scale)

    hd = heads * dh
    o_p = _attn_prompt(q_p.reshape(bp, t, hd), kb_p.reshape(bp, t, hd), vb_p.reshape(bp, t, hd),
                       b_sb_bias[0]).reshape(np_rows, hd)

    q_pad = SUBLANES
    q3 = jnp.zeros((bs, heads, q_pad, dh), BF16).at[:, :, :ts].set(
        q_s.reshape(ts, bs, heads, dh).transpose(1, 2, 0, 3))

    def new_keys(a):
        a = a.reshape(ts, bs, heads, dh).transpose(1, 2, 3, 0)
        return jnp.zeros((bs, heads, dh, page), F32).at[..., :ts].set(a)

    bias_rows = jnp.broadcast_to(jnp.repeat(b_sb_bias[0] * LOG2E, q_pad)[:, None], (heads * q_pad, page))
    o3 = _attn_sample(page_table, bias_rows, q3, new_keys(k_s), new_keys(v_s),
                      cache_k.transpose(0, 2, 3, 1), cache_v.transpose(0, 2, 3, 1), SAMPLE_PAGES_PER_STEP)
    o_s = o3[:, :, :ts].transpose(2, 0, 1, 3).reshape(ns_rows, hd).astype(BF16)

    w_o_b = bf(b_w_o[0])
    y_p, y_s = _moe_layer(groups, cgroups, [x2_p, x2_s], [m_mod[1][0], m_mod[1][1]],
                          [(o_p, w_o_b, b_mod[0]), (o_s, w_o_b, b_mod[1])], m_norm_g[1], m_w_router[1],
                          m_router_bias[1], m_w_gate_up[1], m_w_down[1], bf(m_ws_gate_up[1]),
                          bf(m_ws_down[1]), final_norm_g, True)

    def from_tm(a, *tail):
        return a.reshape(ts, bs, *tail).swapaxes(0, 1)

    return (y_p.reshape(bp, t, d),
            from_tm(y_s, d),
            conv_p[None],
            h_p.reshape(1, bp, d_rnn),
            kt_p.reshape(bp, heads, dh, t).transpose(0, 3, 1, 2),
            vt_p.reshape(bp, heads, dh, t).transpose(0, 3, 1, 2),
            conv_s.reshape(CONV_W - 1, bs, d_rnn).swapaxes(0, 1)[None],
            h_s.reshape(1, bs, d_rnn),
            from_tm(k_s, heads, dh),
            from_tm(v_s, heads, dh))
```

```python
import functools

import jax
import jax.numpy as jnp
from jax import lax
from jax.experimental import pallas as pl
from jax.experimental.pallas import tpu as pltpu

F32 = jnp.float32
BF16 = jnp.bfloat16
I32 = jnp.int32

CONV_W = 4
LRU_BLOCKS = 4
RGLRU_C = 8.0
N_HEADS = 16
N_EXPERTS = 64
TOP_K = 8
N_EXPERT_GROUPS = 8
TOPK_GROUPS = 4
ROUTED_SCALE = 2.5
EPS = 1e-6
NEG_INF = float("-inf")

LANES = 128
SUBLANES = 8
TILE_CHUNKS = SUBLANES
PACKED_CHUNKS = TILE_CHUNKS // 2
VMEM_LIMIT_BYTES = 56 * 1024 * 1024

REC_ROWS = 256
ROUTER_ROWS = 512
PROJ_ROWS = 512
DISPATCH_ROWS = 256
COMBINE_ROWS = 128
EXPERT_ROWS = 512
ATTN_Q_ROWS = 256
ATTN_K_ROWS = 128
SAMPLE_PAGES_PER_STEP = 4


def _params(*sem):
    return pltpu.CompilerParams(dimension_semantics=sem, vmem_limit_bytes=VMEM_LIMIT_BYTES)


def _dot(a, b):
    return jnp.dot(a, b, preferred_element_type=F32)


def _dot_nt(a, b, precision=None):
    return lax.dot_general(a, b, (((1,), (1,)), ((), ())), precision=precision,
                           preferred_element_type=F32)


def _sigmoid(x):
    return 1.0 / (1.0 + jnp.exp(-x))


def _silu(x):
    return x * _sigmoid(x)


def _gelu_tanh(x):
    return 0.5 * x * (1.0 + jnp.tanh(0.7978845608028654 * (x + 0.044715 * (x * x * x))))


def _rms(x, g):
    ms = jnp.mean(x * x, axis=-1, keepdims=True)
    return x * lax.rsqrt(ms + EPS) * g


def _ada(x, g, shift, scale):
    return _rms(x, g) * (1.0 + scale) + shift


def _swiglu(xb, w_gu, w_down):
    gu = _dot(xb, w_gu)
    half = gu.shape[-1] // 2
    act = _silu(gu[:, :half]) * gu[:, half:]
    return _dot(act.astype(BF16), w_down)


def _mod_kernel(c_ref, w_ref, b_ref, o_ref):
    c = c_ref[...]
    o_ref[...] = _dot(_silu(c).astype(BF16), w_ref[...].astype(BF16)) + b_ref[...]


def _modulation(c_all, w, b):
    rows, d = c_all.shape
    n_out = w.shape[1]
    tn = 1024
    return pl.pallas_call(
        _mod_kernel,
        grid=(n_out // tn,),
        in_specs=[pl.BlockSpec((rows, d), lambda j: (0, 0)),
                  pl.BlockSpec((d, tn), lambda j: (0, j)),
                  pl.BlockSpec((1, tn), lambda j: (0, j))],
        out_specs=pl.BlockSpec((rows, tn), lambda j: (0, j)),
        out_shape=jax.ShapeDtypeStruct((rows, n_out), F32),
        compiler_params=_params("arbitrary"),
        name="modulation",
    )(c_all, w, b.reshape(1, n_out))


class _Group:
    def __init__(self, n_rows, rows, blocks_per_seq, mod_rows):
        self.n_rows = n_rows
        self.rows = rows
        self.blocks_per_seq = blocks_per_seq
        self.mod_rows = mod_rows
        self.n_blocks = n_rows // rows

    def mod_spec(self, col, d):
        bps = self.blocks_per_seq
        return pl.BlockSpec((1, self.mod_rows, d), lambda i: (i // bps, 0, col))

    def row_spec(self, width):
        return pl.BlockSpec((self.rows, width), lambda i: (i, 0))


def _const_spec(shape):
    nd = len(shape)
    return pl.BlockSpec(shape, lambda *_: (0,) * nd)


def _scan_rows(a, u, stride, steps):
    row = lax.broadcasted_iota(I32, a.shape, 0)
    s = 1
    while s < steps:
        sh = s * stride
        a_p = pltpu.roll(a, sh, 0)
        u_p = pltpu.roll(u, sh, 0)
        m = row >= sh
        u = jnp.where(m, a * u_p + u, u)
        a = jnp.where(m, a * a_p, a)
        s *= 2
    return a, u


def _rec_kernel(x_ref, sh_ref, sc_ref, gt_ref, hist_ref, h0_ref, g_ref, w_in_ref, cw_ref, cb_ref,
                wr_ref, br_ref, wi_ref, bi_ref, lam_ref, w_out_ref,
                xo_ref, conv_ref, hl_ref, xpad_sc, h_sc, *, stride, steps, hist0):
    t = pl.program_id(1)
    rows = stride * steps
    hrows = (CONV_W - 1) * stride
    d_rnn = h_sc.shape[-1]

    @pl.when(t == 0)
    def _():
        xpad_sc[pl.ds(hist0 - hrows, hrows), :] = hist_ref[0]
        h_sc[...] = h0_ref[0]

    x = x_ref[0]
    hn = _ada(x, g_ref[...], sh_ref[0], sc_ref[0])
    proj = _dot(hn.astype(BF16), w_in_ref[...])
    gate_br = proj[:, :d_rnn]
    xb = proj[:, d_rnn:]
    xpad_sc[pl.ds(hist0, rows), :] = xb
    xc = cb_ref[...] + cw_ref[0:1, :] * xpad_sc[pl.ds(hist0 - hrows, rows), :]
    for j in range(1, CONV_W):
        xc = xc + cw_ref[j:j + 1, :] * xpad_sc[pl.ds(hist0 - hrows + j * stride, rows), :]
    new_hist = xpad_sc[pl.ds(hist0 + rows - hrows, hrows), :]
    xpad_sc[pl.ds(hist0 - hrows, hrows), :] = new_hist
    conv_ref[0] = new_hist

    bw = d_rnn // LRU_BLOCKS
    r_parts, i_parts = [], []
    for n in range(LRU_BLOCKS):
        xblk = xc[:, n * bw:(n + 1) * bw].astype(BF16)
        r_parts.append(_dot(xblk, wr_ref[n]))
        i_parts.append(_dot(xblk, wi_ref[n]))
    r = _sigmoid(jnp.concatenate(r_parts, axis=-1) + br_ref[...])
    ig = _sigmoid(jnp.concatenate(i_parts, axis=-1) + bi_ref[...])
    neg_lam = -lam_ref[...]
    softplus = jnp.maximum(neg_lam, 0.0) + jnp.log1p(jnp.exp(-jnp.abs(neg_lam)))
    log_a = (-RGLRU_C * r) * softplus
    a = jnp.exp(log_a)
    u = jnp.sqrt(-jnp.tanh(log_a) * (a * a + 1.0)) * (ig * xc)
    a_cum, u_cum = _scan_rows(a, u, stride, steps)
    h_prev = h_sc[...]
    if stride == 1:
        h = a_cum * h_prev + u_cum
    else:
        h = a_cum * jnp.concatenate([h_prev] * steps, axis=0) + u_cum
    h_last = h[rows - stride:, :]
    h_sc[...] = h_last
    hl_ref[0] = h_last
    y = _dot((_gelu_tanh(gate_br) * h).astype(BF16), w_out_ref[...])
    xo_ref[0] = x + gt_ref[0] * y


def _recurrent(x3, mod, hist, h0, g, w_in, conv_w, conv_b, w_r, b_r, w_i, b_i, lam, w_out,
               *, stride, steps, mod_rows):
    nb, t_rows, d = x3.shape
    d_rnn = w_out.shape[0]
    rows = stride * steps
    nt = t_rows // rows
    hrows = (CONV_W - 1) * stride
    hist0 = -(-hrows // SUBLANES) * SUBLANES
    assert rows >= hrows and rows % SUBLANES == 0
    x_spec = pl.BlockSpec((1, rows, d), lambda b, t: (b, t, 0))

    def mspec(col):
        if mod_rows == 1:
            return pl.BlockSpec((1, 1, d), lambda b, t: (b, 0, col))
        return pl.BlockSpec((1, rows, d), lambda b, t: (b, 0, col))

    def cspec(shape):
        nd = len(shape)
        return pl.BlockSpec(shape, lambda b, t: (0,) * nd)

    kern = functools.partial(_rec_kernel, stride=stride, steps=steps, hist0=hist0)
    return pl.pallas_call(
        kern,
        grid=(nb, nt),
        in_specs=[x_spec, mspec(0), mspec(1), mspec(2),
                  pl.BlockSpec((1, hrows, d_rnn), lambda b, t: (b, 0, 0)),
                  pl.BlockSpec((1, stride, d_rnn), lambda b, t: (b, 0, 0)),
                  cspec((1, d)), cspec(w_in.shape), cspec(conv_w.shape), cspec((1, d_rnn)),
                  cspec(w_r.shape), cspec((1, d_rnn)), cspec(w_i.shape), cspec((1, d_rnn)),
                  cspec((1, d_rnn)), cspec(w_out.shape)],
        out_specs=[x_spec,
                   pl.BlockSpec((1, hrows, d_rnn), lambda b, t: (b, 0, 0)),
                   pl.BlockSpec((1, stride, d_rnn), lambda b, t: (b, 0, 0))],
        out_shape=[jax.ShapeDtypeStruct(x3.shape, F32),
                   jax.ShapeDtypeStruct((nb, hrows, d_rnn), F32),
                   jax.ShapeDtypeStruct((nb, stride, d_rnn), F32)],
        scratch_shapes=[pltpu.VMEM((hist0 + rows, d_rnn), F32), pltpu.VMEM((stride, d_rnn), F32)],
        compiler_params=_params("arbitrary", "arbitrary"),
        name="recurrent_block",
    )(x3, mod, mod, mod, hist, h0, g.reshape(1, d), w_in, conv_w, conv_b.reshape(1, d_rnn),
      w_r, b_r.reshape(1, d_rnn), w_i, b_i.reshape(1, d_rnn), lam.reshape(1, d_rnn), w_out)


def _tree(op, xs):
    xs = list(xs)
    while len(xs) > 1:
        nxt = [op(xs[i], xs[i + 1]) for i in range(0, len(xs) - 1, 2)]
        if len(xs) % 2:
            nxt.append(xs[-1])
        xs = nxt
    return xs[0]


def _to_token_tiles(ref, x):
    rows = x.shape[0]
    for c in range(TILE_CHUNKS):
        ref[pl.ds(c, rows, stride=TILE_CHUNKS), :] = x[:, c * LANES:(c + 1) * LANES]


def _from_token_tiles(ref, rows):
    return jnp.concatenate([ref[pl.ds(c, rows, stride=TILE_CHUNKS), :] for c in range(TILE_CHUNKS)], axis=1)


HIGH_HALF = 0xFFFF0000


def _to_packed_tiles(ref, xb):
    rows, d = xb.shape
    bits = lax.bitcast_convert_type(xb.astype(F32), jnp.uint32)
    packed = bits[:, d // 2:] | (bits[:, :d // 2] >> 16)
    for c in range(PACKED_CHUNKS):
        ref[pl.ds(c, rows, stride=PACKED_CHUNKS), :] = packed[:, c * LANES:(c + 1) * LANES]


def _from_packed_tiles(ref, rows):
    words = [ref[pl.ds(c, rows, stride=PACKED_CHUNKS), :] for c in range(PACKED_CHUNKS)]
    low = [lax.bitcast_convert_type(w << 16, F32) for w in words]
    high = [lax.bitcast_convert_type(w & jnp.uint32(HIGH_HALF), F32) for w in words]
    return jnp.concatenate(low + high, axis=1).astype(BF16)


def _router_kernel(*refs, pre):
    if pre:
        (x_ref, o_ref, wo_ref, agt_ref, sh_ref, sc_ref, g_ref, wrt_ref, rb_ref, cnt_in_ref, wsgu_ref, wsd_ref,
         xo_ref, hn_ref, shared_ref, e_ref, p_ref, w_ref, cnt_ref, run_sc) = refs
    else:
        (x_ref, sh_ref, sc_ref, g_ref, wrt_ref, rb_ref, cnt_in_ref, wsgu_ref, wsd_ref,
         hn_ref, shared_ref, e_ref, p_ref, w_ref, cnt_ref, run_sc) = refs
    i = pl.program_id(0)
    per = N_EXPERTS // N_EXPERT_GROUPS

    @pl.when(i == 0)
    def _():
        run_sc[...] = cnt_in_ref[...]

    x = x_ref[...]
    if pre:
        x = x + agt_ref[0] * _dot(o_ref[...], wo_ref[...])
        xo_ref[...] = x
    hn = _ada(x, g_ref[...], sh_ref[0], sc_ref[0])
    hb = hn.astype(BF16)
    _to_packed_tiles(hn_ref, hb)
    shared_ref[...] = _swiglu(hb, wsgu_ref[...], wsd_ref[...])
    tb = hn.shape[0]
    logits = _dot_nt(wrt_ref[...], hn, precision=lax.Precision.HIGHEST)
    scores = _sigmoid(logits)
    sel = scores + rb_ref[...][:, 0:1]
    sel_j = [sel[j * SUBLANES:(j + 1) * SUBLANES, :] for j in range(per)]
    sc_j = [scores[j * SUBLANES:(j + 1) * SUBLANES, :] for j in range(per)]
    m1 = _tree(jnp.maximum, sel_j)
    j1 = _tree(jnp.minimum, [jnp.where(sel_j[j] == m1, j, per) for j in range(per)])
    m2 = _tree(jnp.maximum, [jnp.where(j1 == j, NEG_INF, sel_j[j]) for j in range(per)])
    gs = m1 + m2
    g_iota = lax.broadcasted_iota(I32, gs.shape, 0)
    rank = jnp.zeros(gs.shape, I32)
    for dlt in range(1, N_EXPERT_GROUPS):
        other = pltpu.roll(gs, dlt, 0)
        tie = jnp.where(g_iota >= dlt, 1, 0)
        rank = rank + jnp.where(other > gs, 1, jnp.where(other == gs, tie, 0))
    gmask = rank < TOPK_GROUPS
    masked = [jnp.where(gmask, s, NEG_INF) for s in sel_j]
    eid = [g_iota * per + j for j in range(per)]
    chosen = [jnp.zeros(gs.shape, F32) for _ in range(per)]
    e_rows, s_rows = [], []
    for k in range(TOP_K):
        m = jnp.max(_tree(jnp.maximum, masked), axis=0, keepdims=True)
        cand = _tree(jnp.minimum, [jnp.where(masked[j] == m, eid[j], N_EXPERTS) for j in range(per)])
        ek = jnp.min(cand, axis=0, keepdims=True)
        sk = jnp.zeros(gs.shape, F32)
        for j in range(per):
            oh = eid[j] == ek
            masked[j] = jnp.where(oh, NEG_INF, masked[j])
            chosen[j] = jnp.where(oh, 1.0, chosen[j])
            sk = sk + jnp.where(oh, sc_j[j], 0.0)
        e_rows.append(ek)
        s_rows.append(jnp.sum(sk, axis=0, keepdims=True))
    denom = _tree(lambda p, q: p + q, s_rows)
    sel_mask = jnp.concatenate(chosen, axis=0)
    upper = (lax.broadcasted_iota(I32, (tb, tb), 0) < lax.broadcasted_iota(I32, (tb, tb), 1))
    prefix = _dot(sel_mask.astype(BF16), jnp.where(upper, 1.0, 0.0).astype(BF16)) + run_sc[...][:, 0:1]
    pre_j = [prefix[j * SUBLANES:(j + 1) * SUBLANES, :] for j in range(per)]
    for k in range(TOP_K):
        pk = jnp.zeros(gs.shape, F32)
        for j in range(per):
            pk = pk + jnp.where(eid[j] == e_rows[k], pre_j[j], 0.0)
        e_ref[pl.ds(k, 1), :] = e_rows[k]
        p_ref[pl.ds(k, 1), :] = jnp.sum(pk, axis=0, keepdims=True).astype(I32)
        w_ref[pl.ds(k, 1), :] = s_rows[k] / denom * ROUTED_SCALE
    run = run_sc[...] + jnp.sum(sel_mask, axis=1, keepdims=True)
    run_sc[...] = run
    cnt_ref[...] = run


def _router(grp, x, mod, g, w_rt, rbias, cnt_in, ws_gu, ws_down, pre=None):
    d = x.shape[-1]
    assert d == TILE_CHUNKS * LANES
    n, tb = grp.n_rows, grp.rows
    ins = [x]
    in_specs = [grp.row_spec(d)]
    if pre is not None:
        o, w_o, amod = pre
        ins += [o, w_o, amod]
        in_specs += [grp.row_spec(o.shape[-1]), _const_spec(w_o.shape), grp.mod_spec(2, d)]
    ins += [mod, mod, g.reshape(1, d), w_rt, rbias, cnt_in, ws_gu, ws_down]
    in_specs += [grp.mod_spec(0, d), grp.mod_spec(1, d), _const_spec((1, d)), _const_spec(w_rt.shape),
                 _const_spec(rbias.shape), _const_spec(cnt_in.shape), _const_spec(ws_gu.shape),
                 _const_spec(ws_down.shape)]
    tok_spec = pl.BlockSpec((TOP_K, tb), lambda i: (0, i))
    out_specs = [pl.BlockSpec((tb * PACKED_CHUNKS, LANES), lambda i: (i, 0)), grp.row_spec(d),
                 tok_spec, tok_spec, tok_spec, _const_spec(cnt_in.shape)]
    out_shape = [jax.ShapeDtypeStruct((n * PACKED_CHUNKS, LANES), jnp.uint32), jax.ShapeDtypeStruct((n, d), F32),
                 jax.ShapeDtypeStruct((TOP_K, n), I32),
                 jax.ShapeDtypeStruct((TOP_K, n), I32), jax.ShapeDtypeStruct((TOP_K, n), F32),
                 jax.ShapeDtypeStruct(cnt_in.shape, F32)]
    if pre is not None:
        out_specs = [grp.row_spec(d)] + out_specs
        out_shape = [jax.ShapeDtypeStruct((n, d), F32)] + out_shape
    return pl.pallas_call(
        functools.partial(_router_kernel, pre=pre is not None),
        grid=(grp.n_blocks,),
        in_specs=in_specs, out_specs=out_specs, out_shape=out_shape,
        scratch_shapes=[pltpu.VMEM(cnt_in.shape, F32)],
        compiler_params=_params("arbitrary"),
        name="router",
    )(*ins)


def _experts_kernel(blk_ref, exp_ref, lo_ref, hi_ref, first_ref, last_ref, newe_ref, *refs, n_src):
    del exp_ref
    pair_ref, tokrow_ref = refs[:2]
    src_refs = refs[2:2 + n_src]
    (wgu_ref, wd_ref, slots_ref, wgu_sc, wd_sc, x_sc, y_sc, out_sc, tok_sc, xg_sc, sems, load_sem
     ) = refs[2 + n_src:]
    s = pl.program_id(0)
    lo = lo_ref[s]
    hi = hi_ref[s]
    rb = x_sc.shape[0]
    blk = blk_ref[s]

    @pl.when(s == 0)
    def _():
        row = 0
        copies = []
        for src in src_refs:
            copies.append(pltpu.make_async_copy(src, tok_sc.at[pl.ds(row, src.shape[0])], load_sem))
            row += src.shape[0]
        for cp in copies:
            cp.start()
        for cp in copies:
            cp.wait()

    def packed_tile(ref, r):
        return ref.at[pl.ds(pl.multiple_of(r * PACKED_CHUNKS, PACKED_CHUNKS), PACKED_CHUNKS)]

    def tile(ref, r):
        return ref.at[pl.ds(pl.multiple_of(r * TILE_CHUNKS, TILE_CHUNKS), TILE_CHUNKS)]

    def row_copy(slot, r):
        return pltpu.make_async_copy(tile(out_sc.at[slot], r), tile(slots_ref, pair_ref[0, 0, r]),
                                     sems.at[slot])

    def drain(slot):
        def body(r, c):
            row_copy(slot, r).wait()
            return c
        lax.fori_loop(0, rb, body, 0, unroll=8)

    @pl.when(newe_ref[s] == 1)
    def _():
        wgu_sc[...] = wgu_ref[0, 0].astype(BF16)
        wd_sc[...] = wd_ref[0, 0].astype(BF16)

    @pl.when(first_ref[s] == 1)
    def _():
        def gather(r, c):
            src = pl.multiple_of(tokrow_ref[0, 0, r], PACKED_CHUNKS)
            packed_tile(xg_sc, r)[...] = tok_sc[pl.ds(src, PACKED_CHUNKS), :]
            return c
        lax.fori_loop(0, rb, gather, 0, unroll=8)
        x_sc[...] = _from_packed_tiles(xg_sc, rb)
        y_sc[...] = jnp.zeros(y_sc.shape, F32)

    @pl.when(hi > lo)
    def _():
        gu = _dot(x_sc[...], wgu_sc[...])
        half = gu.shape[-1] // 2
        act = _silu(gu[:, :half]) * gu[:, half:]
        row = lax.broadcasted_iota(I32, act.shape, 0)
        act = jnp.where((row >= lo) & (row < hi), act, 0.0)
        y_sc[...] += _dot(act.astype(BF16), wd_sc[...])

    @pl.when(last_ref[s] == 1)
    def _():
        for slot in range(2):
            @pl.when(blk % 2 == slot)
            def _():
                @pl.when(blk >= 2)
                def _():
                    drain(slot)

                _to_token_tiles(out_sc.at[slot], y_sc[...])

                def issue(r, c):
                    row_copy(slot, r).start()
                    return c
                lax.fori_loop(0, rb, issue, 0, unroll=8)

    @pl.when(s == pl.num_programs(0) - 1)
    def _():
        for slot in range(2):
            @pl.when((blk % 2 == slot) | (blk >= 1))
            def _():
                drain(slot)


def _experts(steps, packed_rows, row_pair, w_gu, w_down, layer):
    n_tok = sum(p.shape[0] for p in packed_rows) // PACKED_CHUNKS
    m = n_tok * TOP_K
    d = TILE_CHUNKS * LANES
    rb = EXPERT_ROWS
    n_steps = steps[0].shape[0]
    de2 = w_gu.shape[-1]
    tok_rows = (row_pair // TOP_K) * PACKED_CHUNKS
    grid_spec = pltpu.PrefetchScalarGridSpec(
        num_scalar_prefetch=7,
        grid=(n_steps,),
        in_specs=[pl.BlockSpec((1, 1, rb), lambda s, blk, ex, *_: (blk[s], 0, 0), memory_space=pltpu.SMEM)] * 2
                 + [pl.BlockSpec(memory_space=pl.ANY)] * len(packed_rows)
                 + [pl.BlockSpec((1, 1, d, de2), lambda s, blk, ex, *_: (layer, ex[s], 0, 0)),
                    pl.BlockSpec((1, 1, de2 // 2, d), lambda s, blk, ex, *_: (layer, ex[s], 0, 0))],
        out_specs=pl.BlockSpec(memory_space=pl.ANY),
        scratch_shapes=[pltpu.VMEM((d, de2), BF16), pltpu.VMEM((de2 // 2, d), BF16),
                        pltpu.VMEM((rb, d), BF16), pltpu.VMEM((rb, d), F32),
                        pltpu.VMEM((2, rb * TILE_CHUNKS, LANES), F32),
                        pltpu.VMEM((n_tok * PACKED_CHUNKS, LANES), jnp.uint32),
                        pltpu.VMEM((rb * PACKED_CHUNKS, LANES), jnp.uint32),
                        pltpu.SemaphoreType.DMA((2,)), pltpu.SemaphoreType.DMA(())],
    )
    return pl.pallas_call(
        functools.partial(_experts_kernel, n_src=len(packed_rows)),
        grid_spec=grid_spec,
        out_shape=jax.ShapeDtypeStruct((m * TILE_CHUNKS, LANES), F32),
        compiler_params=_params("arbitrary"),
        name="routed_experts",
    )(*steps, row_pair.reshape(m // rb, 1, rb), tok_rows.reshape(m // rb, 1, rb), *packed_rows, w_gu, w_down)


def _expert_steps(counts, m):
    rb = EXPERT_ROWS
    nb = m // rb
    start = jnp.cumsum(counts) - counts
    end = start + counts
    cuts = jnp.sort(jnp.concatenate([jnp.arange(nb, dtype=I32) * rb, start.astype(I32)]))
    nxt = jnp.concatenate([cuts[1:], jnp.array([m], I32)])
    blk = jnp.minimum(cuts // rb, nb - 1)
    ex = jnp.minimum(jnp.sum((end[None, :] <= cuts[:, None]).astype(I32), axis=1), N_EXPERTS - 1)
    lo = cuts - blk * rb
    hi = nxt - blk * rb
    changes = (blk[1:] != blk[:-1]).astype(I32)
    first = jnp.concatenate([jnp.ones((1,), I32), changes])
    last = jnp.concatenate([changes, jnp.ones((1,), I32)])
    newe = jnp.concatenate([jnp.ones((1,), I32), (ex[1:] != ex[:-1]).astype(I32)])
    return blk, ex, lo, hi, first, last, newe


def _combine_kernel(ew_ref, x_ref, shared_ref, gt_ref, fg_ref, slots_ref, o_ref, routed_sc, *, rows, final):
    def weigh(r, c):
        acc = None
        for k in range(TOP_K):
            p0 = pl.multiple_of((r * TOP_K + k) * TILE_CHUNKS, TILE_CHUNKS)
            term = ew_ref[r * TOP_K + k] * slots_ref[pl.ds(p0, TILE_CHUNKS), :]
            acc = term if acc is None else acc + term
        routed_sc[pl.ds(pl.multiple_of(r * TILE_CHUNKS, TILE_CHUNKS), TILE_CHUNKS), :] = acc
        return c

    lax.fori_loop(0, rows, weigh, 0, unroll=2)
    routed = _from_token_tiles(routed_sc, rows)
    x = x_ref[...] + gt_ref[0] * (routed + shared_ref[...])
    if final:
        x = _rms(x, fg_ref[...])
    o_ref[...] = x


def _combine(grp, row0, ew, x, shared, mod, fg, slots, final):
    d = x.shape[-1]
    rows = grp.rows
    blk0 = row0 // rows
    assert row0 % rows == 0
    return pl.pallas_call(
        functools.partial(_combine_kernel, rows=rows, final=final),
        grid=(grp.n_blocks,),
        in_specs=[pl.BlockSpec((rows * TOP_K,), lambda i: (i,), memory_space=pltpu.SMEM),
                  grp.row_spec(d), grp.row_spec(d), grp.mod_spec(2, d), _const_spec((1, d)),
                  pl.BlockSpec((rows * TOP_K * TILE_CHUNKS, LANES), lambda i: (i + blk0, 0))],
        out_specs=grp.row_spec(d),
        out_shape=jax.ShapeDtypeStruct(x.shape, F32),
        scratch_shapes=[pltpu.VMEM((rows * TILE_CHUNKS, LANES), F32)],
        compiler_params=_params("arbitrary"),
        name="combine",
    )(ew, x, shared, mod, fg.reshape(1, d), slots)


def _proj_kernel(*refs, q_scale, transposed):
    if transposed:
        (x_ref, ksh_ref, ksc_ref, kg_ref, wkv_ref, wkvt_ref, bsh_ref, bsc_ref, bg_ref, wq_ref,
         k_ref, v_ref, kb_ref, vb_ref, q_ref) = refs
    else:
        (x_ref, ksh_ref, ksc_ref, kg_ref, wkv_ref, bsh_ref, bsc_ref, bg_ref, wq_ref,
         k_ref, v_ref, q_ref) = refs
    x = x_ref[...]
    kvn = _ada(x, kg_ref[...], ksh_ref[0], ksc_ref[0]).astype(BF16)
    kv = _dot(kvn, wkv_ref[...])
    half = kv.shape[-1] // 2
    if transposed:
        kvt = _dot_nt(wkvt_ref[...], kvn)
        k_ref[0] = kvt[:half]
        v_ref[0] = kvt[half:]
        kb_ref[...] = kv[:, :half].astype(BF16)
        vb_ref[...] = kv[:, half:].astype(BF16)
    else:
        k_ref[...] = kv[:, :half]
        v_ref[...] = kv[:, half:]
    hn = _ada(x, bg_ref[...], bsh_ref[0], bsc_ref[0])
    q_ref[...] = (_dot(hn.astype(BF16), wq_ref[...]) * q_scale).astype(BF16)


def _proj(grp, x, kv_mod, kv_g, w_kv, b_mod, b_g, w_q, q_scale, n_seq=None):
    d = x.shape[-1]
    n = grp.n_rows
    hd = w_q.shape[-1]
    transposed = n_seq is not None
    ins = [x, kv_mod, kv_mod, kv_g.reshape(1, d), w_kv]
    in_specs = [grp.row_spec(d), grp.mod_spec(0, d), grp.mod_spec(1, d), _const_spec((1, d)),
                _const_spec(w_kv.shape)]
    if transposed:
        ins.append(w_kv.T)
        in_specs.append(_const_spec((w_kv.shape[1], w_kv.shape[0])))
        bps = grp.blocks_per_seq
        t_spec = pl.BlockSpec((1, hd, grp.rows), lambda i: (i // bps, 0, i % bps))
        out_specs = [t_spec, t_spec, grp.row_spec(hd), grp.row_spec(hd), grp.row_spec(hd)]
        t_shape = jax.ShapeDtypeStruct((n_seq, hd, n // n_seq), F32)
        out_shape = [t_shape, t_shape] + [jax.ShapeDtypeStruct((n, hd), BF16)] * 3
    else:
        out_specs = [grp.row_spec(hd)] * 3
        out_shape = [jax.ShapeDtypeStruct((n, hd), F32), jax.ShapeDtypeStruct((n, hd), F32),
                     jax.ShapeDtypeStruct((n, hd), BF16)]
    ins += [b_mod, b_mod, b_g.reshape(1, d), w_q]
    in_specs += [grp.mod_spec(0, d), grp.mod_spec(1, d), _const_spec((1, d)), _const_spec(w_q.shape)]
    return pl.pallas_call(
        functools.partial(_proj_kernel, q_scale=q_scale, transposed=transposed),
        grid=(grp.n_blocks,),
        in_specs=in_specs, out_specs=out_specs, out_shape=out_shape,
        compiler_params=_params("arbitrary"),
        name="kv_q_projection",
    )(*ins)


def _later_key_matrix(tk, n_heads=1):
    size = n_heads * tk
    j = lax.broadcasted_iota(I32, (size, size), 0)
    s = lax.broadcasted_iota(I32, (size, size), 1)
    later = jnp.where(j % tk > s % tk, 1.0, 0.0)
    return jnp.where(j // tk == s // tk, later, 0.0).astype(BF16)


LOG2E = 1.4426950408889634
SIGN_BIT = 0x80000000
MASKED_LOGIT = -1e30


def _sb_logs(z2, tri, mask):
    neg_abs = lax.bitcast_convert_type(lax.bitcast_convert_type(z2, jnp.uint32) | jnp.uint32(SIGN_BIT), F32)
    neg_l1m = jnp.maximum(z2, 0.0) + jnp.log2(1.0 + jnp.exp2(neg_abs))
    log_b = z2 - neg_l1m
    if mask is not None:
        neg_l1m = jnp.where(mask, neg_l1m, 0.0)
    return log_b, _dot(neg_l1m.astype(BF16), tri), neg_l1m


def _sb_weights(log_b, tail, carry, mask):
    w = jnp.exp2(log_b - tail - carry)
    if mask is not None:
        w = jnp.where(mask, w, 0.0)
    return w


def _attn_prompt_kernel(bias_ref, q_ref, k_ref, v_ref, o_ref, acc_sc, z_sc, lb_sc, tail_sc, bias_sc,
                        *, tq, tk):
    hp = pl.program_id(1)
    qi = pl.program_id(2)
    q = q_ref[0]
    half = q.shape[-1] // 2
    tri = _later_key_matrix(tk, 2)
    n_diag = tq // tk
    assert n_diag == 2, "the block pipeline below alternates two slots per loop trip"
    first_head = lax.broadcasted_iota(I32, (tk, q.shape[-1]), 1) < half
    acc_sc[...] = jnp.zeros(acc_sc.shape, F32)

    @pl.when(qi == 0)
    def _():
        pair_lane = lax.broadcasted_iota(I32, (tq, 2 * tk), 1)
        row = lax.broadcasted_iota(I32, (tq, 2 * tk), 0)
        bias2 = jnp.where(pair_lane < tk, bias_ref[hp * 2] * LOG2E, bias_ref[hp * 2 + 1] * LOG2E)
        bias_sc[0] = jnp.where(tk + pair_lane % tk < row, bias2, MASKED_LOGIT)
        bias_sc[1] = jnp.where(pair_lane % tk < row, bias2, MASKED_LOGIT)
        bias_sc[2] = bias2

    def split_heads(a):
        zero = jnp.zeros_like(a)
        return jnp.concatenate([jnp.where(first_head, a, zero), jnp.where(first_head, zero, a)], axis=0)

    def keys(ref, j):
        return split_heads(ref[0, pl.ds(pl.multiple_of(j * tk, tk), tk), :])

    def logits(j):
        return _dot_nt(q, keys(k_ref, j))

    def logs(z_pair, bias):
        log_b, tail, neg_l1m = _sb_logs(z_pair + bias, tri, None)
        sums = tuple(jnp.sum(neg_l1m[:, hh * tk:(hh + 1) * tk], axis=1, keepdims=True) for hh in range(2))
        return log_b, tail, sums

    def accumulate(j, log_b, tail, carries):
        ws = [_sb_weights(log_b[:, hh * tk:(hh + 1) * tk], tail[:, hh * tk:(hh + 1) * tk], carries[hh],
                          None).astype(BF16) for hh in range(2)]
        acc_sc[...] += _dot(jnp.concatenate(ws, axis=1), keys(v_ref, j))

    last = (qi + 1) * n_diag - 1
    z_sc[1] = logits(last)
    lb_sc[0] = jnp.full(lb_sc.shape[1:], MASKED_LOGIT, F32)
    tail_sc[0] = jnp.zeros(tail_sc.shape[1:], F32)

    def body(it, c):
        cur, prev = c
        for u in range(2):
            j = last - 2 * it - u
            z_sc[u] = logits(jnp.maximum(j - 1, 0))
            log_b, tail, sums = logs(z_sc[1 - u], bias_sc[jnp.where(it == 0, u, 2)])
            accumulate(jnp.minimum(j + 1, last), lb_sc[u], tail_sc[u], prev)
            lb_sc[1 - u] = log_b
            tail_sc[1 - u] = tail
            prev = cur
            cur = tuple(a + s for a, s in zip(cur, sums))
        return cur, prev

    zero = jnp.zeros((tq, 1), F32)
    _, prev = lax.fori_loop(0, qi + 1, body, ((zero, zero), (zero, zero)))
    accumulate(0, lb_sc[0], tail_sc[0], prev)
    o_ref[0] = acc_sc[...].astype(o_ref.dtype)


def _attn_prompt(q, k, v, bias):
    b, t, hd = q.shape
    tq, tk = ATTN_Q_ROWS, ATTN_K_ROWS
    pair = 2 * (hd // N_HEADS)
    return pl.pallas_call(
        functools.partial(_attn_prompt_kernel, tq=tq, tk=tk),
        grid=(b, hd // pair, t // tq),
        in_specs=[pl.BlockSpec(memory_space=pltpu.SMEM),
                  pl.BlockSpec((1, tq, pair), lambda bi, hp, qi: (bi, qi, hp)),
                  pl.BlockSpec((1, t, pair), lambda bi, hp, qi: (bi, 0, hp)),
                  pl.BlockSpec((1, t, pair), lambda bi, hp, qi: (bi, 0, hp))],
        out_specs=pl.BlockSpec((1, tq, pair), lambda bi, hp, qi: (bi, qi, hp)),
        out_shape=jax.ShapeDtypeStruct((b, t, hd), BF16),
        scratch_shapes=[pltpu.VMEM((tq, pair), F32)] + [pltpu.VMEM((2, tq, 2 * tk), F32)] * 3
                       + [pltpu.VMEM((3, tq, 2 * tk), F32)],
        compiler_params=_params("arbitrary", "arbitrary", "arbitrary"),
        name="sb_attention_prompt",
    )(bias, q, k, v)


def _attn_sample_kernel(pt_ref, bias_ref, q_ref, kn_ref, vn_ref, *rest, page, q_pad, ppg):
    del pt_ref
    kc_refs, vc_refs = rest[:ppg], rest[ppg:2 * ppg]
    o_ref, acc_sc, carry_sc, tri_sc = rest[2 * ppg:]
    j = pl.program_id(1)
    heads = q_ref.shape[1]
    rows = heads * q_pad

    @pl.when((pl.program_id(0) == 0) & (j == 0))
    def _():
        n = tri_sc.shape[0]
        later = lax.broadcasted_iota(I32, (n, n), 0) > lax.broadcasted_iota(I32, (n, n), 1)
        tri_sc[...] = jnp.where(later, 1.0, 0.0).astype(BF16)

    @pl.when(j == 0)
    def _():
        acc_sc[...] = jnp.zeros(acc_sc.shape, F32)
        carry_sc[...] = jnp.zeros(carry_sc.shape, F32)

    def process(k_refs, v_refs, masked):
        k3 = jnp.concatenate([r[0] for r in k_refs], axis=2).astype(BF16)
        v3 = jnp.concatenate([r[0] for r in v_refs], axis=2).astype(BF16)
        keys = k3.shape[2]
        z3 = lax.dot_general(q_ref[0], k3, (((2,), (1,)), ((0,), (0,))), preferred_element_type=F32)
        z2 = z3.reshape(rows, keys) + bias_ref[...][:, 0:1]
        mask = None
        if masked:
            qidx = lax.broadcasted_iota(I32, (rows, keys), 0) % q_pad
            mask = lax.broadcasted_iota(I32, (rows, keys), 1) < qidx
        carry = carry_sc[...][:, 0:1]
        log_b, tail, neg_l1m = _sb_logs(z2, tri_sc[0:keys, 0:keys], mask)
        w = _sb_weights(log_b, tail, carry, mask)
        carry_sc[...] = jnp.broadcast_to(carry + jnp.sum(neg_l1m, axis=1, keepdims=True), carry_sc.shape)
        w3 = w.reshape(heads, q_pad, keys).astype(BF16)
        acc_sc[...] += lax.dot_general(w3, v3, (((2,), (2,)), ((0,), (0,))), preferred_element_type=F32)

    @pl.when(j == 0)
    def _():
        process([kn_ref], [vn_ref], True)

    @pl.when(j > 0)
    def _():
        process(kc_refs[::-1], vc_refs[::-1], False)

    @pl.when(j == pl.num_programs(1) - 1)
    def _():
        o_ref[0] = acc_sc[...]


def _attn_sample(page_table, bias_rows, q3, k_new, v_new, cache_kt, cache_vt, ppg):
    bs, heads, q_pad, dh = q3.shape
    page = cache_kt.shape[-1]
    n_pages = page_table.shape[1]
    rows = heads * q_pad
    assert n_pages % ppg == 0

    def cache_spec(p):
        def index(b, j, pt):
            logical = n_pages - 1 - ((jnp.maximum(j, 1) - 1) * ppg + p)
            return (pt[b, logical], 0, 0, 0)
        return pl.BlockSpec((1, heads, dh, page), index)

    new_spec = pl.BlockSpec((1, heads, dh, page), lambda b, j, pt: (b, 0, 0, 0))
    q_spec = pl.BlockSpec((1, heads, q_pad, dh), lambda b, j, pt: (b, 0, 0, 0))
    grid_spec = pltpu.PrefetchScalarGridSpec(
        num_scalar_prefetch=1,
        grid=(bs, n_pages // ppg + 1),
        in_specs=[pl.BlockSpec((rows, page), lambda b, j, pt: (0, 0)), q_spec, new_spec, new_spec]
                 + [cache_spec(p) for p in range(ppg)] * 2,
        out_specs=q_spec,
        scratch_shapes=[pltpu.VMEM((heads, q_pad, dh), F32), pltpu.VMEM((rows, LANES), F32),
                        pltpu.VMEM((ppg * page, ppg * page), BF16)],
    )
    return pl.pallas_call(
        functools.partial(_attn_sample_kernel, page=page, q_pad=q_pad, ppg=ppg),
        grid_spec=grid_spec,
        out_shape=jax.ShapeDtypeStruct((bs, heads, q_pad, dh), F32),
        compiler_params=_params("arbitrary", "arbitrary"),
        name="sb_attention_sample",
    )(page_table, bias_rows, q3, k_new, v_new, *([cache_kt] * ppg), *([cache_vt] * ppg))


def _permute_experts(a):
    per = N_EXPERTS // N_EXPERT_GROUPS
    return a.reshape(N_EXPERT_GROUPS, per, *a.shape[1:]).swapaxes(0, 1).reshape(a.shape)


def _moe_layer(groups, cgroups, xs_rows, mods, pres, g, w_router, r_bias, expert_w, ws_gu, ws_down,
               fg, final):
    d = xs_rows[0].shape[-1]
    w_rt = _permute_experts(w_router.T)
    rbias = jnp.broadcast_to(_permute_experts(r_bias.reshape(N_EXPERTS, 1)), (N_EXPERTS, LANES))
    cnt = jnp.zeros((N_EXPERTS, LANES), F32)
    routed = []
    x_new = []
    for grp, x, mod, pre in zip(groups, xs_rows, mods, pres):
        outs = _router(grp, x, mod, g, w_rt, rbias, cnt, ws_gu, ws_down, pre)
        if pre is not None:
            x, outs = outs[0], outs[1:]
        hn, shared, e_t, p_t, w_t, cnt = outs
        routed.append((hn, shared, e_t, p_t, w_t))
        x_new.append(x)
    per = N_EXPERTS // N_EXPERT_GROUPS
    counts = cnt[:, 0].astype(I32).reshape(per, N_EXPERT_GROUPS).T.reshape(N_EXPERTS)
    start = jnp.cumsum(counts) - counts
    m = sum(grp.n_rows for grp in groups) * TOP_K
    steps = _expert_steps(counts, m)
    dests = []
    expert_ids = jnp.arange(N_EXPERTS, dtype=I32)[:, None, None]
    for hn, shared, e_t, p_t, w_t in routed:
        start_sel = jnp.sum(jnp.where(e_t[None] == expert_ids, start[:, None, None], 0), axis=0)
        dests.append((start_sel + p_t).T.reshape(-1))
    row_pair = jnp.argsort(jnp.concatenate(dests)).astype(I32)
    slots = _experts(steps, [r[0] for r in routed], row_pair, *expert_w)
    outs = []
    row0 = 0
    for cgrp, x, mod, (hn, shared, e_t, p_t, w_t) in zip(cgroups, x_new, mods, routed):
        outs.append(_combine(cgrp, row0, w_t.T.reshape(-1), x, shared, mod, fg, slots, final))
        row0 += cgrp.n_rows
    return outs


def kernel(x_prompt, x_sample, c_prompt, c_sample, state_conv, state_h, cache_k, cache_v, page_table, a_norm_g, a_mod_w, a_mod_b, a_w_in, a_conv_w, a_conv_b, a_w_gate_r, a_b_gate_r, a_w_gate_i, a_b_gate_i, a_lambda, a_w_out, kv_norm_g, kv_mod_w, kv_mod_b, w_kv, b_norm_g, b_mod_w, b_mod_b, b_w_q, b_sb_bias, b_w_o, m_norm_g, m_mod_w, m_mod_b, m_w_router, m_router_bias, m_w_gate_up, m_w_down, m_ws_gate_up, m_ws_down, final_norm_g):
    bp, t, d = x_prompt.shape
    bs, ts, _ = x_sample.shape
    depth = m_norm_g.shape[0]
    n_a = a_norm_g.shape[0]
    d_rnn = a_w_out.shape[1]
    heads = N_HEADS
    dh = b_w_q.shape[-1] // heads
    page = cache_k.shape[1]
    np_rows = bp * t
    ns_rows = bs * ts
    assert n_a == 1 and depth == 2, "layer pattern of this step: one self-decoder, one cross-decoder layer"

    n_c = bp + bs
    c_rows = -(-n_c // SUBLANES) * SUBLANES
    c_all = jnp.zeros((c_rows, d), F32).at[:bp].set(c_prompt).at[bp:n_c].set(c_sample)

    def mods(w, b):
        m = _modulation(c_all, w, b)
        mp = m[:bp].reshape(bp, 1, -1)
        ms = jnp.tile(m[bp:n_c], (ts, 1)).reshape(1, ns_rows, -1)
        return mp, ms

    a_mod = mods(a_mod_w[0], a_mod_b[0])
    kv_mod = mods(kv_mod_w, kv_mod_b)
    b_mod = mods(b_mod_w[0], b_mod_b[0])
    m_mod = [mods(m_mod_w[l], m_mod_b[l]) for l in range(depth)]

    bf = lambda w: w.astype(BF16)

    rec_w = (a_norm_g[0], bf(a_w_in[0]), a_conv_w[0], a_conv_b[0], bf(a_w_gate_r[0]), a_b_gate_r[0],
             bf(a_w_gate_i[0]), a_b_gate_i[0], a_lambda[0], bf(a_w_out[0]))
    hist_p = jnp.zeros((bp, CONV_W - 1, d_rnn), F32)
    h0_p = jnp.zeros((bp, 1, d_rnn), F32)
    x1_p, conv_p, h_p = _recurrent(x_prompt, a_mod[0], hist_p, h0_p, *rec_w,
                                   stride=1, steps=REC_ROWS, mod_rows=1)
    xs_tm = x_sample.swapaxes(0, 1).reshape(1, ns_rows, d)
    hist_s = state_conv[0].swapaxes(0, 1).reshape(1, (CONV_W - 1) * bs, d_rnn)
    h0_s = state_h[0].reshape(1, bs, d_rnn)
    x1_s, conv_s, h_s = _recurrent(xs_tm, a_mod[1], hist_s, h0_s, *rec_w,
                                   stride=bs, steps=ts, mod_rows=ns_rows)

    grp_p = _Group(np_rows, ROUTER_ROWS, t // ROUTER_ROWS, 1)
    grp_s = _Group(ns_rows, ns_rows, 1, ns_rows)
    groups = [grp_p, grp_s]
    cgroups = [_Group(np_rows, COMBINE_ROWS, t // COMBINE_ROWS, 1), grp_s]

    x2_p, x2_s = _moe_layer(groups, cgroups, [x1_p.reshape(np_rows, d), x1_s.reshape(ns_rows, d)],
                            [m_mod[0][0], m_mod[0][1]], [None, None], m_norm_g[0], m_w_router[0],
                            m_router_bias[0], (m_w_gate_up, m_w_down, 0), bf(m_ws_gate_up[0]),
                            bf(m_ws_down[0]), final_norm_g, False)

    q_scale = float(dh) ** -0.5 * LOG2E
    w_kv_b, w_q_b = bf(w_kv), bf(b_w_q[0])
    pgrp_p = _Group(np_rows, PROJ_ROWS, t // PROJ_ROWS, 1)
    kt_p, vt_p, kb_p, vb_p, q_p = _proj(pgrp_p, x2_p, kv_mod[0], kv_norm_g, w_kv_b, b_mod[0], b_norm_g[0],
                                        w_q_b, q_scale, n_seq=bp)
    k_s, v_s, q_s = _proj(grp_s, x2_s, kv_mod[1], kv_norm_g, w_kv_b, b_mod[1], b_norm_g[0],
                          w_q_b, q_scale)

    hd = heads * dh
    o_p = _attn_prompt(q_p.reshape(bp, t, hd), kb_p.reshape(bp, t, hd), vb_p.reshape(bp, t, hd),
                       b_sb_bias[0]).reshape(np_rows, hd)

    q_pad = SUBLANES
    q3 = jnp.zeros((bs, heads, q_pad, dh), BF16).at[:, :, :ts].set(
        q_s.reshape(ts, bs, heads, dh).transpose(1, 2, 0, 3))

    def new_keys(a):
        a = a.reshape(ts, bs, heads, dh).transpose(1, 2, 3, 0)
        return jnp.zeros((bs, heads, dh, page), F32).at[..., :ts].set(a)

    bias_rows = jnp.broadcast_to(jnp.repeat(b_sb_bias[0] * LOG2E, q_pad)[:, None], (heads * q_pad, page))
    o3 = _attn_sample(page_table, bias_rows, q3, new_keys(k_s), new_keys(v_s),
                      cache_k.transpose(0, 2, 3, 1), cache_v.transpose(0, 2, 3, 1), SAMPLE_PAGES_PER_STEP)
    o_s = o3[:, :, :ts].transpose(2, 0, 1, 3).reshape(ns_rows, hd).astype(BF16)

    w_o_b = bf(b_w_o[0])
    y_p, y_s = _moe_layer(groups, cgroups, [x2_p, x2_s], [m_mod[1][0], m_mod[1][1]],
                          [(o_p, w_o_b, b_mod[0]), (o_s, w_o_b, b_mod[1])], m_norm_g[1], m_w_router[1],
                          m_router_bias[1], (m_w_gate_up, m_w_down, 1), bf(m_ws_gate_up[1]),
                          bf(m_ws_down[1]), final_norm_g, True)

    def from_tm(a, *tail):
        return a.reshape(ts, bs, *tail).swapaxes(0, 1)

    return (y_p.reshape(bp, t, d),
            from_tm(y_s, d),
            conv_p[None],
            h_p.reshape(1, bp, d_rnn),
            kt_p.reshape(bp, heads, dh, t).transpose(0, 3, 1, 2),
            vt_p.reshape(bp, heads, dh, t).transpose(0, 3, 1, 2),
            conv_s.reshape(CONV_W - 1, bs, d_rnn).swapaxes(0, 1)[None],
            h_s.reshape(1, bs, d_rnn),
            from_tm(k_s, heads, dh),
            from_tm(v_s, heads, dh))
```

```python
import functools

import jax
import jax.numpy as jnp
from jax import lax
from jax.experimental import pallas as pl
from jax.experimental.pallas import tpu as pltpu

F32 = jnp.float32
BF16 = jnp.bfloat16
I32 = jnp.int32

CONV_W = 4
LRU_BLOCKS = 4
RGLRU_C = 8.0
N_HEADS = 16
N_EXPERTS = 64
TOP_K = 8
N_EXPERT_GROUPS = 8
TOPK_GROUPS = 4
ROUTED_SCALE = 2.5
EPS = 1e-6
NEG_INF = float("-inf")

LANES = 128
SUBLANES = 8
TILE_CHUNKS = SUBLANES
PACKED_CHUNKS = TILE_CHUNKS // 2
VMEM_LIMIT_BYTES = 56 * 1024 * 1024

REC_ROWS = 256
ROUTER_ROWS = 512
PROJ_ROWS = 512
COMBINE_ROWS = 256
EXPERT_ROWS = 512
ATTN_Q_ROWS = 256
ATTN_K_ROWS = 128
SAMPLE_PAGES_PER_STEP = 4


def _params(*sem):
    return pltpu.CompilerParams(dimension_semantics=sem, vmem_limit_bytes=VMEM_LIMIT_BYTES)


def _dot(a, b):
    return jnp.dot(a, b, preferred_element_type=F32)


def _dot_nt(a, b, precision=None):
    return lax.dot_general(a, b, (((1,), (1,)), ((), ())), precision=precision,
                           preferred_element_type=F32)


def _sigmoid(x):
    return 1.0 / (1.0 + jnp.exp(-x))


def _silu(x):
    return x * _sigmoid(x)


def _gelu_tanh(x):
    return 0.5 * x * (1.0 + jnp.tanh(0.7978845608028654 * (x + 0.044715 * (x * x * x))))


def _rms(x, g):
    ms = jnp.mean(x * x, axis=-1, keepdims=True)
    return x * lax.rsqrt(ms + EPS) * g


def _ada(x, g, shift, scale):
    return _rms(x, g) * (1.0 + scale) + shift


def _swiglu(xb, w_gu, w_down):
    gu = _dot(xb, w_gu)
    half = gu.shape[-1] // 2
    act = _silu(gu[:, :half]) * gu[:, half:]
    return _dot(act.astype(BF16), w_down)


def _mod_kernel(c_ref, w_ref, b_ref, o_ref):
    c = c_ref[...]
    o_ref[...] = _dot(_silu(c).astype(BF16), w_ref[...].astype(BF16)) + b_ref[...]


def _modulation(c_all, w, b):
    rows, d = c_all.shape
    n_out = w.shape[1]
    tn = 1024
    return pl.pallas_call(
        _mod_kernel,
        grid=(n_out // tn,),
        in_specs=[pl.BlockSpec((rows, d), lambda j: (0, 0)),
                  pl.BlockSpec((d, tn), lambda j: (0, j)),
                  pl.BlockSpec((1, tn), lambda j: (0, j))],
        out_specs=pl.BlockSpec((rows, tn), lambda j: (0, j)),
        out_shape=jax.ShapeDtypeStruct((rows, n_out), F32),
        compiler_params=_params("arbitrary"),
        name="modulation",
    )(c_all, w, b.reshape(1, n_out))


class _Group:
    def __init__(self, n_rows, rows, blocks_per_seq, mod_rows):
        self.n_rows = n_rows
        self.rows = rows
        self.blocks_per_seq = blocks_per_seq
        self.mod_rows = mod_rows
        self.n_blocks = n_rows // rows

    def mod_spec(self, col, d):
        bps = self.blocks_per_seq
        return pl.BlockSpec((1, self.mod_rows, d), lambda i: (i // bps, 0, col))

    def row_spec(self, width):
        return pl.BlockSpec((self.rows, width), lambda i: (i, 0))


def _const_spec(shape):
    nd = len(shape)
    return pl.BlockSpec(shape, lambda *_: (0,) * nd)


def _scan_rows(a, u, stride, steps):
    row = lax.broadcasted_iota(I32, a.shape, 0)
    s = 1
    while s < steps:
        sh = s * stride
        a_p = pltpu.roll(a, sh, 0)
        u_p = pltpu.roll(u, sh, 0)
        m = row >= sh
        u = jnp.where(m, a * u_p + u, u)
        a = jnp.where(m, a * a_p, a)
        s *= 2
    return a, u


def _rec_kernel(x_ref, sh_ref, sc_ref, gt_ref, hist_ref, h0_ref, g_ref, w_in_ref, cw_ref, cb_ref,
                wr_ref, br_ref, wi_ref, bi_ref, lam_ref, w_out_ref,
                xo_ref, conv_ref, hl_ref, xpad_sc, h_sc, *, stride, steps, hist0):
    t = pl.program_id(1)
    rows = stride * steps
    hrows = (CONV_W - 1) * stride
    d_rnn = h_sc.shape[-1]

    @pl.when(t == 0)
    def _():
        xpad_sc[pl.ds(hist0 - hrows, hrows), :] = hist_ref[0]
        h_sc[...] = h0_ref[0]

    x = x_ref[0]
    hn = _ada(x, g_ref[...], sh_ref[0], sc_ref[0])
    proj = _dot(hn.astype(BF16), w_in_ref[...])
    gate_br = proj[:, :d_rnn]
    xb = proj[:, d_rnn:]
    xpad_sc[pl.ds(hist0, rows), :] = xb
    xc = cb_ref[...] + cw_ref[0:1, :] * xpad_sc[pl.ds(hist0 - hrows, rows), :]
    for j in range(1, CONV_W):
        xc = xc + cw_ref[j:j + 1, :] * xpad_sc[pl.ds(hist0 - hrows + j * stride, rows), :]
    new_hist = xpad_sc[pl.ds(hist0 + rows - hrows, hrows), :]
    xpad_sc[pl.ds(hist0 - hrows, hrows), :] = new_hist
    conv_ref[0] = new_hist

    bw = d_rnn // LRU_BLOCKS
    r_parts, i_parts = [], []
    for n in range(LRU_BLOCKS):
        xblk = xc[:, n * bw:(n + 1) * bw].astype(BF16)
        r_parts.append(_dot(xblk, wr_ref[n]))
        i_parts.append(_dot(xblk, wi_ref[n]))
    r = _sigmoid(jnp.concatenate(r_parts, axis=-1) + br_ref[...])
    ig = _sigmoid(jnp.concatenate(i_parts, axis=-1) + bi_ref[...])
    neg_lam = -lam_ref[...]
    softplus = jnp.maximum(neg_lam, 0.0) + jnp.log1p(jnp.exp(-jnp.abs(neg_lam)))
    log_a = (-RGLRU_C * r) * softplus
    a = jnp.exp(log_a)
    u = jnp.sqrt(-jnp.tanh(log_a) * (a * a + 1.0)) * (ig * xc)
    a_cum, u_cum = _scan_rows(a, u, stride, steps)
    h_prev = h_sc[...]
    if stride == 1:
        h = a_cum * h_prev + u_cum
    else:
        h = a_cum * jnp.concatenate([h_prev] * steps, axis=0) + u_cum
    h_last = h[rows - stride:, :]
    h_sc[...] = h_last
    hl_ref[0] = h_last
    y = _dot((_gelu_tanh(gate_br) * h).astype(BF16), w_out_ref[...])
    xo_ref[0] = x + gt_ref[0] * y


def _recurrent(x3, mod, hist, h0, g, w_in, conv_w, conv_b, w_r, b_r, w_i, b_i, lam, w_out,
               *, stride, steps, mod_rows):
    nb, t_rows, d = x3.shape
    d_rnn = w_out.shape[0]
    rows = stride * steps
    nt = t_rows // rows
    hrows = (CONV_W - 1) * stride
    hist0 = -(-hrows // SUBLANES) * SUBLANES
    assert rows >= hrows and rows % SUBLANES == 0
    x_spec = pl.BlockSpec((1, rows, d), lambda b, t: (b, t, 0))

    def mspec(col):
        if mod_rows == 1:
            return pl.BlockSpec((1, 1, d), lambda b, t: (b, 0, col))
        return pl.BlockSpec((1, rows, d), lambda b, t: (b, 0, col))

    def cspec(shape):
        nd = len(shape)
        return pl.BlockSpec(shape, lambda b, t: (0,) * nd)

    kern = functools.partial(_rec_kernel, stride=stride, steps=steps, hist0=hist0)
    return pl.pallas_call(
        kern,
        grid=(nb, nt),
        in_specs=[x_spec, mspec(0), mspec(1), mspec(2),
                  pl.BlockSpec((1, hrows, d_rnn), lambda b, t: (b, 0, 0)),
                  pl.BlockSpec((1, stride, d_rnn), lambda b, t: (b, 0, 0)),
                  cspec((1, d)), cspec(w_in.shape), cspec(conv_w.shape), cspec((1, d_rnn)),
                  cspec(w_r.shape), cspec((1, d_rnn)), cspec(w_i.shape), cspec((1, d_rnn)),
                  cspec((1, d_rnn)), cspec(w_out.shape)],
        out_specs=[x_spec,
                   pl.BlockSpec((1, hrows, d_rnn), lambda b, t: (b, 0, 0)),
                   pl.BlockSpec((1, stride, d_rnn), lambda b, t: (b, 0, 0))],
        out_shape=[jax.ShapeDtypeStruct(x3.shape, F32),
                   jax.ShapeDtypeStruct((nb, hrows, d_rnn), F32),
                   jax.ShapeDtypeStruct((nb, stride, d_rnn), F32)],
        scratch_shapes=[pltpu.VMEM((hist0 + rows, d_rnn), F32), pltpu.VMEM((stride, d_rnn), F32)],
        compiler_params=_params("arbitrary", "arbitrary"),
        name="recurrent_block",
    )(x3, mod, mod, mod, hist, h0, g.reshape(1, d), w_in, conv_w, conv_b.reshape(1, d_rnn),
      w_r, b_r.reshape(1, d_rnn), w_i, b_i.reshape(1, d_rnn), lam.reshape(1, d_rnn), w_out)


def _tree(op, xs):
    xs = list(xs)
    while len(xs) > 1:
        nxt = [op(xs[i], xs[i + 1]) for i in range(0, len(xs) - 1, 2)]
        if len(xs) % 2:
            nxt.append(xs[-1])
        xs = nxt
    return xs[0]


def _to_token_tiles(ref, x):
    rows = x.shape[0]
    for c in range(TILE_CHUNKS):
        ref[pl.ds(c, rows, stride=TILE_CHUNKS), :] = x[:, c * LANES:(c + 1) * LANES]


def _from_token_tiles(ref, rows):
    return jnp.concatenate([ref[pl.ds(c, rows, stride=TILE_CHUNKS), :] for c in range(TILE_CHUNKS)], axis=1)


HIGH_HALF = 0xFFFF0000


def _to_packed_tiles(ref, xb):
    rows, d = xb.shape
    bits = lax.bitcast_convert_type(xb.astype(F32), jnp.uint32)
    packed = bits[:, d // 2:] | (bits[:, :d // 2] >> 16)
    for c in range(PACKED_CHUNKS):
        ref[pl.ds(c, rows, stride=PACKED_CHUNKS), :] = packed[:, c * LANES:(c + 1) * LANES]


def _from_packed_tiles(ref, rows):
    words = [ref[pl.ds(c, rows, stride=PACKED_CHUNKS), :] for c in range(PACKED_CHUNKS)]
    low = [lax.bitcast_convert_type(w << 16, F32) for w in words]
    high = [lax.bitcast_convert_type(w & jnp.uint32(HIGH_HALF), F32) for w in words]
    return jnp.concatenate(low + high, axis=1).astype(BF16)


def _router_kernel(*refs, pre):
    if pre:
        (x_ref, o_ref, wo_ref, agt_ref, sh_ref, sc_ref, g_ref, wrt_ref, rb_ref, cnt_in_ref, wsgu_ref, wsd_ref,
         xo_ref, hn_ref, shared_ref, e_ref, p_ref, w_ref, cnt_ref, run_sc) = refs
    else:
        (x_ref, sh_ref, sc_ref, g_ref, wrt_ref, rb_ref, cnt_in_ref, wsgu_ref, wsd_ref,
         hn_ref, shared_ref, e_ref, p_ref, w_ref, cnt_ref, run_sc) = refs
    i = pl.program_id(0)
    per = N_EXPERTS // N_EXPERT_GROUPS

    @pl.when(i == 0)
    def _():
        run_sc[...] = cnt_in_ref[...]

    x = x_ref[...]
    if pre:
        x = x + agt_ref[0] * _dot(o_ref[...], wo_ref[...])
        xo_ref[...] = x
    hn = _ada(x, g_ref[...], sh_ref[0], sc_ref[0])
    hb = hn.astype(BF16)
    _to_packed_tiles(hn_ref, hb)
    shared_ref[...] = _swiglu(hb, wsgu_ref[...], wsd_ref[...])
    tb = hn.shape[0]
    logits = _dot_nt(wrt_ref[...], hn, precision=lax.Precision.HIGHEST)
    scores = _sigmoid(logits)
    sel = scores + rb_ref[...][:, 0:1]
    sel_j = [sel[j * SUBLANES:(j + 1) * SUBLANES, :] for j in range(per)]
    sc_j = [scores[j * SUBLANES:(j + 1) * SUBLANES, :] for j in range(per)]
    m1 = _tree(jnp.maximum, sel_j)
    j1 = _tree(jnp.minimum, [jnp.where(sel_j[j] == m1, j, per) for j in range(per)])
    m2 = _tree(jnp.maximum, [jnp.where(j1 == j, NEG_INF, sel_j[j]) for j in range(per)])
    gs = m1 + m2
    g_iota = lax.broadcasted_iota(I32, gs.shape, 0)
    rank = jnp.zeros(gs.shape, I32)
    for dlt in range(1, N_EXPERT_GROUPS):
        other = pltpu.roll(gs, dlt, 0)
        tie = jnp.where(g_iota >= dlt, 1, 0)
        rank = rank + jnp.where(other > gs, 1, jnp.where(other == gs, tie, 0))
    gmask = rank < TOPK_GROUPS
    masked = [jnp.where(gmask, s, NEG_INF) for s in sel_j]
    eid = [g_iota * per + j for j in range(per)]
    chosen = [jnp.zeros(gs.shape, F32) for _ in range(per)]
    e_rows, s_rows = [], []
    for k in range(TOP_K):
        m = jnp.max(_tree(jnp.maximum, masked), axis=0, keepdims=True)
        cand = _tree(jnp.minimum, [jnp.where(masked[j] == m, eid[j], N_EXPERTS) for j in range(per)])
        ek = jnp.min(cand, axis=0, keepdims=True)
        sk = jnp.zeros(gs.shape, F32)
        for j in range(per):
            oh = eid[j] == ek
            masked[j] = jnp.where(oh, NEG_INF, masked[j])
            chosen[j] = jnp.where(oh, 1.0, chosen[j])
            sk = sk + jnp.where(oh, sc_j[j], 0.0)
        e_rows.append(ek)
        s_rows.append(jnp.sum(sk, axis=0, keepdims=True))
    denom = _tree(lambda p, q: p + q, s_rows)
    sel_mask = jnp.concatenate(chosen, axis=0)
    upper = (lax.broadcasted_iota(I32, (tb, tb), 0) < lax.broadcasted_iota(I32, (tb, tb), 1))
    prefix = _dot(sel_mask.astype(BF16), jnp.where(upper, 1.0, 0.0).astype(BF16)) + run_sc[...][:, 0:1]
    pre_j = [prefix[j * SUBLANES:(j + 1) * SUBLANES, :] for j in range(per)]
    for k in range(TOP_K):
        pk = jnp.zeros(gs.shape, F32)
        for j in range(per):
            pk = pk + jnp.where(eid[j] == e_rows[k], pre_j[j], 0.0)
        e_ref[pl.ds(k, 1), :] = e_rows[k]
        p_ref[pl.ds(k, 1), :] = jnp.sum(pk, axis=0, keepdims=True).astype(I32)
        w_ref[pl.ds(k, 1), :] = s_rows[k] / denom * ROUTED_SCALE
    run = run_sc[...] + jnp.sum(sel_mask, axis=1, keepdims=True)
    run_sc[...] = run
    cnt_ref[...] = run


def _router(grp, x, mod, g, w_rt, rbias, cnt_in, ws_gu, ws_down, pre=None):
    d = x.shape[-1]
    assert d == TILE_CHUNKS * LANES
    n, tb = grp.n_rows, grp.rows
    ins = [x]
    in_specs = [grp.row_spec(d)]
    if pre is not None:
        o, w_o, amod = pre
        ins += [o, w_o, amod]
        in_specs += [grp.row_spec(o.shape[-1]), _const_spec(w_o.shape), grp.mod_spec(2, d)]
    ins += [mod, mod, g.reshape(1, d), w_rt, rbias, cnt_in, ws_gu, ws_down]
    in_specs += [grp.mod_spec(0, d), grp.mod_spec(1, d), _const_spec((1, d)), _const_spec(w_rt.shape),
                 _const_spec(rbias.shape), _const_spec(cnt_in.shape), _const_spec(ws_gu.shape),
                 _const_spec(ws_down.shape)]
    tok_spec = pl.BlockSpec((TOP_K, tb), lambda i: (0, i))
    out_specs = [pl.BlockSpec((tb * PACKED_CHUNKS, LANES), lambda i: (i, 0)), grp.row_spec(d),
                 tok_spec, tok_spec, tok_spec, _const_spec(cnt_in.shape)]
    out_shape = [jax.ShapeDtypeStruct((n * PACKED_CHUNKS, LANES), jnp.uint32), jax.ShapeDtypeStruct((n, d), F32),
                 jax.ShapeDtypeStruct((TOP_K, n), I32),
                 jax.ShapeDtypeStruct((TOP_K, n), I32), jax.ShapeDtypeStruct((TOP_K, n), F32),
                 jax.ShapeDtypeStruct(cnt_in.shape, F32)]
    if pre is not None:
        out_specs = [grp.row_spec(d)] + out_specs
        out_shape = [jax.ShapeDtypeStruct((n, d), F32)] + out_shape
    return pl.pallas_call(
        functools.partial(_router_kernel, pre=pre is not None),
        grid=(grp.n_blocks,),
        in_specs=in_specs, out_specs=out_specs, out_shape=out_shape,
        scratch_shapes=[pltpu.VMEM(cnt_in.shape, F32)],
        compiler_params=_params("arbitrary"),
        name="router",
    )(*ins)


def _experts_kernel(blk_ref, exp_ref, lo_ref, hi_ref, first_ref, last_ref, newe_ref, *refs, n_src):
    del exp_ref
    pair_ref, tokrow_ref = refs[:2]
    src_refs = refs[2:2 + n_src]
    (wgu_ref, wd_ref, slots_ref, wgu_sc, wd_sc, x_sc, y_sc, out_sc, tok_sc, xg_sc, sems, load_sem
     ) = refs[2 + n_src:]
    s = pl.program_id(0)
    lo = lo_ref[s]
    hi = hi_ref[s]
    rb = x_sc.shape[0]
    blk = blk_ref[s]

    @pl.when(s == 0)
    def _():
        row = 0
        copies = []
        for src in src_refs:
            copies.append(pltpu.make_async_copy(src, tok_sc.at[pl.ds(row, src.shape[0])], load_sem))
            row += src.shape[0]
        for cp in copies:
            cp.start()
        for cp in copies:
            cp.wait()

    def packed_tile(ref, r):
        return ref.at[pl.ds(pl.multiple_of(r * PACKED_CHUNKS, PACKED_CHUNKS), PACKED_CHUNKS)]

    def tile(ref, r):
        return ref.at[pl.ds(pl.multiple_of(r * TILE_CHUNKS, TILE_CHUNKS), TILE_CHUNKS)]

    def row_copy(slot, r):
        return pltpu.make_async_copy(tile(out_sc.at[slot], r), tile(slots_ref, pair_ref[0, 0, r]),
                                     sems.at[slot])

    def drain(slot):
        def body(r, c):
            row_copy(slot, r).wait()
            return c
        lax.fori_loop(0, rb, body, 0, unroll=8)

    @pl.when(newe_ref[s] == 1)
    def _():
        wgu_sc[...] = wgu_ref[0, 0].astype(BF16)
        wd_sc[...] = wd_ref[0, 0].astype(BF16)

    @pl.when(first_ref[s] == 1)
    def _():
        def gather(r, c):
            src = pl.multiple_of(tokrow_ref[0, 0, r], PACKED_CHUNKS)
            packed_tile(xg_sc, r)[...] = tok_sc[pl.ds(src, PACKED_CHUNKS), :]
            return c
        lax.fori_loop(0, rb, gather, 0, unroll=8)
        x_sc[...] = _from_packed_tiles(xg_sc, rb)
        y_sc[...] = jnp.zeros(y_sc.shape, F32)

    @pl.when(hi > lo)
    def _():
        gu = _dot(x_sc[...], wgu_sc[...])
        half = gu.shape[-1] // 2
        act = _silu(gu[:, :half]) * gu[:, half:]
        row = lax.broadcasted_iota(I32, act.shape, 0)
        act = jnp.where((row >= lo) & (row < hi), act, 0.0)
        y_sc[...] += _dot(act.astype(BF16), wd_sc[...])

    @pl.when(last_ref[s] == 1)
    def _():
        for slot in range(2):
            @pl.when(blk % 2 == slot)
            def _():
                @pl.when(blk >= 2)
                def _():
                    drain(slot)

                _to_token_tiles(out_sc.at[slot], y_sc[...])

                def issue(r, c):
                    row_copy(slot, r).start()
                    return c
                lax.fori_loop(0, rb, issue, 0, unroll=8)

    @pl.when(s == pl.num_programs(0) - 1)
    def _():
        for slot in range(2):
            @pl.when((blk % 2 == slot) | (blk >= 1))
            def _():
                drain(slot)


def _experts(steps, packed_rows, row_pair, w_gu, w_down, layer):
    n_tok = sum(p.shape[0] for p in packed_rows) // PACKED_CHUNKS
    m = n_tok * TOP_K
    d = TILE_CHUNKS * LANES
    rb = EXPERT_ROWS
    n_steps = steps[0].shape[0]
    de2 = w_gu.shape[-1]
    tok_rows = (row_pair // TOP_K) * PACKED_CHUNKS
    grid_spec = pltpu.PrefetchScalarGridSpec(
        num_scalar_prefetch=7,
        grid=(n_steps,),
        in_specs=[pl.BlockSpec((1, 1, rb), lambda s, blk, ex, *_: (blk[s], 0, 0), memory_space=pltpu.SMEM)] * 2
                 + [pl.BlockSpec(memory_space=pl.ANY)] * len(packed_rows)
                 + [pl.BlockSpec((1, 1, d, de2), lambda s, blk, ex, *_: (layer, ex[s], 0, 0)),
                    pl.BlockSpec((1, 1, de2 // 2, d), lambda s, blk, ex, *_: (layer, ex[s], 0, 0))],
        out_specs=pl.BlockSpec(memory_space=pl.ANY),
        scratch_shapes=[pltpu.VMEM((d, de2), BF16), pltpu.VMEM((de2 // 2, d), BF16),
                        pltpu.VMEM((rb, d), BF16), pltpu.VMEM((rb, d), F32),
                        pltpu.VMEM((2, rb * TILE_CHUNKS, LANES), F32),
                        pltpu.VMEM((n_tok * PACKED_CHUNKS, LANES), jnp.uint32),
                        pltpu.VMEM((rb * PACKED_CHUNKS, LANES), jnp.uint32),
                        pltpu.SemaphoreType.DMA((2,)), pltpu.SemaphoreType.DMA(())],
    )
    return pl.pallas_call(
        functools.partial(_experts_kernel, n_src=len(packed_rows)),
        grid_spec=grid_spec,
        out_shape=jax.ShapeDtypeStruct((m * TILE_CHUNKS, LANES), F32),
        compiler_params=_params("arbitrary"),
        name="routed_experts",
    )(*steps, row_pair.reshape(m // rb, 1, rb), tok_rows.reshape(m // rb, 1, rb), *packed_rows, w_gu, w_down)


def _expert_steps(counts, m):
    rb = EXPERT_ROWS
    nb = m // rb
    start = jnp.cumsum(counts) - counts
    end = start + counts
    cuts = jnp.sort(jnp.concatenate([jnp.arange(nb, dtype=I32) * rb, start.astype(I32)]))
    nxt = jnp.concatenate([cuts[1:], jnp.array([m], I32)])
    blk = jnp.minimum(cuts // rb, nb - 1)
    ex = jnp.minimum(jnp.sum((end[None, :] <= cuts[:, None]).astype(I32), axis=1), N_EXPERTS - 1)
    lo = cuts - blk * rb
    hi = nxt - blk * rb
    changes = (blk[1:] != blk[:-1]).astype(I32)
    first = jnp.concatenate([jnp.ones((1,), I32), changes])
    last = jnp.concatenate([changes, jnp.ones((1,), I32)])
    newe = jnp.concatenate([jnp.ones((1,), I32), (ex[1:] != ex[:-1]).astype(I32)])
    return blk, ex, lo, hi, first, last, newe


def _combine_kernel(ew_ref, x_ref, shared_ref, gt_ref, fg_ref, slots_ref, o_ref, routed_sc, *, rows, final):
    def weigh(r, c):
        acc = None
        for k in range(TOP_K):
            p0 = pl.multiple_of((r * TOP_K + k) * TILE_CHUNKS, TILE_CHUNKS)
            term = ew_ref[r * TOP_K + k] * slots_ref[pl.ds(p0, TILE_CHUNKS), :]
            acc = term if acc is None else acc + term
        routed_sc[pl.ds(pl.multiple_of(r * TILE_CHUNKS, TILE_CHUNKS), TILE_CHUNKS), :] = acc
        return c

    lax.fori_loop(0, rows, weigh, 0, unroll=2)
    routed = _from_token_tiles(routed_sc, rows)
    x = x_ref[...] + gt_ref[0] * (routed + shared_ref[...])
    if final:
        x = _rms(x, fg_ref[...])
    o_ref[...] = x


def _combine(grp, row0, ew, x, shared, mod, fg, slots, final):
    d = x.shape[-1]
    rows = grp.rows
    blk0 = row0 // rows
    assert row0 % rows == 0
    return pl.pallas_call(
        functools.partial(_combine_kernel, rows=rows, final=final),
        grid=(grp.n_blocks,),
        in_specs=[pl.BlockSpec((rows * TOP_K,), lambda i: (i,), memory_space=pltpu.SMEM),
                  grp.row_spec(d), grp.row_spec(d), grp.mod_spec(2, d), _const_spec((1, d)),
                  pl.BlockSpec((rows * TOP_K * TILE_CHUNKS, LANES), lambda i: (i + blk0, 0))],
        out_specs=grp.row_spec(d),
        out_shape=jax.ShapeDtypeStruct(x.shape, F32),
        scratch_shapes=[pltpu.VMEM((rows * TILE_CHUNKS, LANES), F32)],
        compiler_params=_params("arbitrary"),
        name="combine",
    )(ew, x, shared, mod, fg.reshape(1, d), slots)


def _proj_kernel(*refs, q_scale, transposed):
    if transposed:
        (x_ref, ksh_ref, ksc_ref, kg_ref, wkv_ref, wkvt_ref, bsh_ref, bsc_ref, bg_ref, wq_ref,
         k_ref, v_ref, kb_ref, vb_ref, q_ref) = refs
    else:
        (x_ref, ksh_ref, ksc_ref, kg_ref, wkv_ref, bsh_ref, bsc_ref, bg_ref, wq_ref,
         k_ref, v_ref, q_ref) = refs
    x = x_ref[...]
    kvn = _ada(x, kg_ref[...], ksh_ref[0], ksc_ref[0]).astype(BF16)
    kv = _dot(kvn, wkv_ref[...])
    half = kv.shape[-1] // 2
    if transposed:
        kvt = _dot_nt(wkvt_ref[...], kvn)
        k_ref[0] = kvt[:half]
        v_ref[0] = kvt[half:]
        kb_ref[...] = kv[:, :half].astype(BF16)
        vb_ref[...] = kv[:, half:].astype(BF16)
    else:
        k_ref[...] = kv[:, :half]
        v_ref[...] = kv[:, half:]
    hn = _ada(x, bg_ref[...], bsh_ref[0], bsc_ref[0])
    q_ref[...] = (_dot(hn.astype(BF16), wq_ref[...]) * q_scale).astype(BF16)


def _proj(grp, x, kv_mod, kv_g, w_kv, b_mod, b_g, w_q, q_scale, n_seq=None):
    d = x.shape[-1]
    n = grp.n_rows
    hd = w_q.shape[-1]
    transposed = n_seq is not None
    ins = [x, kv_mod, kv_mod, kv_g.reshape(1, d), w_kv]
    in_specs = [grp.row_spec(d), grp.mod_spec(0, d), grp.mod_spec(1, d), _const_spec((1, d)),
                _const_spec(w_kv.shape)]
    if transposed:
        ins.append(w_kv.T)
        in_specs.append(_const_spec((w_kv.shape[1], w_kv.shape[0])))
        bps = grp.blocks_per_seq
        t_spec = pl.BlockSpec((1, hd, grp.rows), lambda i: (i // bps, 0, i % bps))
        out_specs = [t_spec, t_spec, grp.row_spec(hd), grp.row_spec(hd), grp.row_spec(hd)]
        t_shape = jax.ShapeDtypeStruct((n_seq, hd, n // n_seq), F32)
        out_shape = [t_shape, t_shape] + [jax.ShapeDtypeStruct((n, hd), BF16)] * 3
    else:
        out_specs = [grp.row_spec(hd)] * 3
        out_shape = [jax.ShapeDtypeStruct((n, hd), F32), jax.ShapeDtypeStruct((n, hd), F32),
                     jax.ShapeDtypeStruct((n, hd), BF16)]
    ins += [b_mod, b_mod, b_g.reshape(1, d), w_q]
    in_specs += [grp.mod_spec(0, d), grp.mod_spec(1, d), _const_spec((1, d)), _const_spec(w_q.shape)]
    return pl.pallas_call(
        functools.partial(_proj_kernel, q_scale=q_scale, transposed=transposed),
        grid=(grp.n_blocks,),
        in_specs=in_specs, out_specs=out_specs, out_shape=out_shape,
        compiler_params=_params("arbitrary"),
        name="kv_q_projection",
    )(*ins)


def _later_key_matrix(tk, n_heads=1):
    size = n_heads * tk
    j = lax.broadcasted_iota(I32, (size, size), 0)
    s = lax.broadcasted_iota(I32, (size, size), 1)
    later = jnp.where(j % tk > s % tk, 1.0, 0.0)
    return jnp.where(j // tk == s // tk, later, 0.0).astype(BF16)


LOG2E = 1.4426950408889634
SIGN_BIT = 0x80000000
MASKED_LOGIT = -1e30


def _sb_logs(z2, tri, mask):
    neg_abs = lax.bitcast_convert_type(lax.bitcast_convert_type(z2, jnp.uint32) | jnp.uint32(SIGN_BIT), F32)
    neg_l1m = jnp.maximum(z2, 0.0) + jnp.log2(1.0 + jnp.exp2(neg_abs))
    log_b = z2 - neg_l1m
    if mask is not None:
        neg_l1m = jnp.where(mask, neg_l1m, 0.0)
    return log_b, _dot(neg_l1m.astype(BF16), tri), neg_l1m


def _sb_weights(log_b, tail, carry, mask):
    w = jnp.exp2(log_b - tail - carry)
    if mask is not None:
        w = jnp.where(mask, w, 0.0)
    return w


def _attn_prompt_kernel(bias_ref, q_ref, k_ref, v_ref, o_ref, acc_sc, z_sc, lb_sc, tail_sc, bias_sc,
                        *, tq, tk):
    hp = pl.program_id(1)
    qi = pl.program_id(2)
    q = q_ref[0]
    half = q.shape[-1] // 2
    tri = _later_key_matrix(tk, 2)
    n_diag = tq // tk
    assert n_diag == 2, "the block pipeline below alternates two slots per loop trip"
    first_head = lax.broadcasted_iota(I32, (tk, q.shape[-1]), 1) < half
    acc_sc[...] = jnp.zeros(acc_sc.shape, F32)

    @pl.when(qi == 0)
    def _():
        pair_lane = lax.broadcasted_iota(I32, (tq, 2 * tk), 1)
        row = lax.broadcasted_iota(I32, (tq, 2 * tk), 0)
        bias2 = jnp.where(pair_lane < tk, bias_ref[hp * 2] * LOG2E, bias_ref[hp * 2 + 1] * LOG2E)
        bias_sc[0] = jnp.where(tk + pair_lane % tk < row, bias2, MASKED_LOGIT)
        bias_sc[1] = jnp.where(pair_lane % tk < row, bias2, MASKED_LOGIT)
        bias_sc[2] = bias2

    def split_heads(a):
        zero = jnp.zeros_like(a)
        return jnp.concatenate([jnp.where(first_head, a, zero), jnp.where(first_head, zero, a)], axis=0)

    def keys(ref, j):
        return split_heads(ref[0, pl.ds(pl.multiple_of(j * tk, tk), tk), :])

    def logits(j):
        return _dot_nt(q, keys(k_ref, j))

    def logs(z_pair, bias):
        log_b, tail, neg_l1m = _sb_logs(z_pair + bias, tri, None)
        sums = tuple(jnp.sum(neg_l1m[:, hh * tk:(hh + 1) * tk], axis=1, keepdims=True) for hh in range(2))
        return log_b, tail, sums

    def accumulate(j, log_b, tail, carries):
        ws = [_sb_weights(log_b[:, hh * tk:(hh + 1) * tk], tail[:, hh * tk:(hh + 1) * tk], carries[hh],
                          None).astype(BF16) for hh in range(2)]
        acc_sc[...] += _dot(jnp.concatenate(ws, axis=1), keys(v_ref, j))

    last = (qi + 1) * n_diag - 1
    z_sc[1] = logits(last)
    lb_sc[0] = jnp.full(lb_sc.shape[1:], MASKED_LOGIT, F32)
    tail_sc[0] = jnp.zeros(tail_sc.shape[1:], F32)

    def body(it, c):
        cur, prev = c
        for u in range(2):
            j = last - 2 * it - u
            z_sc[u] = logits(jnp.maximum(j - 1, 0))
            log_b, tail, sums = logs(z_sc[1 - u], bias_sc[jnp.where(it == 0, u, 2)])
            accumulate(jnp.minimum(j + 1, last), lb_sc[u], tail_sc[u], prev)
            lb_sc[1 - u] = log_b
            tail_sc[1 - u] = tail
            prev = cur
            cur = tuple(a + s for a, s in zip(cur, sums))
        return cur, prev

    zero = jnp.zeros((tq, 1), F32)
    _, prev = lax.fori_loop(0, qi + 1, body, ((zero, zero), (zero, zero)))
    accumulate(0, lb_sc[0], tail_sc[0], prev)
    o_ref[0] = acc_sc[...].astype(o_ref.dtype)


def _attn_prompt(q, k, v, bias):
    b, t, hd = q.shape
    tq, tk = ATTN_Q_ROWS, ATTN_K_ROWS
    pair = 2 * (hd // N_HEADS)
    return pl.pallas_call(
        functools.partial(_attn_prompt_kernel, tq=tq, tk=tk),
        grid=(b, hd // pair, t // tq),
        in_specs=[pl.BlockSpec(memory_space=pltpu.SMEM),
                  pl.BlockSpec((1, tq, pair), lambda bi, hp, qi: (bi, qi, hp)),
                  pl.BlockSpec((1, t, pair), lambda bi, hp, qi: (bi, 0, hp)),
                  pl.BlockSpec((1, t, pair), lambda bi, hp, qi: (bi, 0, hp))],
        out_specs=pl.BlockSpec((1, tq, pair), lambda bi, hp, qi: (bi, qi, hp)),
        out_shape=jax.ShapeDtypeStruct((b, t, hd), BF16),
        scratch_shapes=[pltpu.VMEM((tq, pair), F32)] + [pltpu.VMEM((2, tq, 2 * tk), F32)] * 3
                       + [pltpu.VMEM((3, tq, 2 * tk), F32)],
        compiler_params=_params("arbitrary", "arbitrary", "arbitrary"),
        name="sb_attention_prompt",
    )(bias, q, k, v)


def _attn_sample_kernel(pt_ref, bias_ref, q_ref, kn_ref, vn_ref, *rest, page, q_pad, ppg):
    del pt_ref
    kc_refs, vc_refs = rest[:ppg], rest[ppg:2 * ppg]
    o_ref, acc_sc, carry_sc, tri_sc = rest[2 * ppg:]
    j = pl.program_id(1)
    heads = q_ref.shape[1]
    rows = heads * q_pad

    @pl.when((pl.program_id(0) == 0) & (j == 0))
    def _():
        n = tri_sc.shape[0]
        later = lax.broadcasted_iota(I32, (n, n), 0) > lax.broadcasted_iota(I32, (n, n), 1)
        tri_sc[...] = jnp.where(later, 1.0, 0.0).astype(BF16)

    @pl.when(j == 0)
    def _():
        acc_sc[...] = jnp.zeros(acc_sc.shape, F32)
        carry_sc[...] = jnp.zeros(carry_sc.shape, F32)

    def process(k_refs, v_refs, masked):
        k3 = jnp.concatenate([r[0] for r in k_refs], axis=2).astype(BF16)
        v3 = jnp.concatenate([r[0] for r in v_refs], axis=2).astype(BF16)
        keys = k3.shape[2]
        z3 = lax.dot_general(q_ref[0], k3, (((2,), (1,)), ((0,), (0,))), preferred_element_type=F32)
        z2 = z3.reshape(rows, keys) + bias_ref[...][:, 0:1]
        mask = None
        if masked:
            qidx = lax.broadcasted_iota(I32, (rows, keys), 0) % q_pad
            mask = lax.broadcasted_iota(I32, (rows, keys), 1) < qidx
        carry = carry_sc[...][:, 0:1]
        log_b, tail, neg_l1m = _sb_logs(z2, tri_sc[0:keys, 0:keys], mask)
        w = _sb_weights(log_b, tail, carry, mask)
        carry_sc[...] = jnp.broadcast_to(carry + jnp.sum(neg_l1m, axis=1, keepdims=True), carry_sc.shape)
        w3 = w.reshape(heads, q_pad, keys).astype(BF16)
        acc_sc[...] += lax.dot_general(w3, v3, (((2,), (2,)), ((0,), (0,))), preferred_element_type=F32)

    @pl.when(j == 0)
    def _():
        process([kn_ref], [vn_ref], True)

    @pl.when(j > 0)
    def _():
        process(kc_refs[::-1], vc_refs[::-1], False)

    @pl.when(j == pl.num_programs(1) - 1)
    def _():
        o_ref[0] = acc_sc[...]


def _attn_sample(page_table, bias_rows, q3, k_new, v_new, cache_kt, cache_vt, ppg):
    bs, heads, q_pad, dh = q3.shape
    page = cache_kt.shape[-1]
    n_pages = page_table.shape[1]
    rows = heads * q_pad
    assert n_pages % ppg == 0

    def cache_spec(p):
        def index(b, j, pt):
            logical = n_pages - 1 - ((jnp.maximum(j, 1) - 1) * ppg + p)
            return (pt[b, logical], 0, 0, 0)
        return pl.BlockSpec((1, heads, dh, page), index)

    new_spec = pl.BlockSpec((1, heads, dh, page), lambda b, j, pt: (b, 0, 0, 0))
    q_spec = pl.BlockSpec((1, heads, q_pad, dh), lambda b, j, pt: (b, 0, 0, 0))
    grid_spec = pltpu.PrefetchScalarGridSpec(
        num_scalar_prefetch=1,
        grid=(bs, n_pages // ppg + 1),
        in_specs=[pl.BlockSpec((rows, page), lambda b, j, pt: (0, 0)), q_spec, new_spec, new_spec]
                 + [cache_spec(p) for p in range(ppg)] * 2,
        out_specs=q_spec,
        scratch_shapes=[pltpu.VMEM((heads, q_pad, dh), F32), pltpu.VMEM((rows, LANES), F32),
                        pltpu.VMEM((ppg * page, ppg * page), BF16)],
    )
    return pl.pallas_call(
        functools.partial(_attn_sample_kernel, page=page, q_pad=q_pad, ppg=ppg),
        grid_spec=grid_spec,
        out_shape=jax.ShapeDtypeStruct((bs, heads, q_pad, dh), F32),
        compiler_params=_params("arbitrary", "arbitrary"),
        name="sb_attention_sample",
    )(page_table, bias_rows, q3, k_new, v_new, *([cache_kt] * ppg), *([cache_vt] * ppg))


def _permute_experts(a):
    per = N_EXPERTS // N_EXPERT_GROUPS
    return a.reshape(N_EXPERT_GROUPS, per, *a.shape[1:]).swapaxes(0, 1).reshape(a.shape)


def _moe_layer(groups, cgroups, xs_rows, mods, pres, g, w_router, r_bias, expert_w, ws_gu, ws_down,
               fg, final):
    d = xs_rows[0].shape[-1]
    w_rt = _permute_experts(w_router.T)
    rbias = jnp.broadcast_to(_permute_experts(r_bias.reshape(N_EXPERTS, 1)), (N_EXPERTS, LANES))
    cnt = jnp.zeros((N_EXPERTS, LANES), F32)
    routed = []
    x_new = []
    for grp, x, mod, pre in zip(groups, xs_rows, mods, pres):
        outs = _router(grp, x, mod, g, w_rt, rbias, cnt, ws_gu, ws_down, pre)
        if pre is not None:
            x, outs = outs[0], outs[1:]
        hn, shared, e_t, p_t, w_t, cnt = outs
        routed.append((hn, shared, e_t, p_t, w_t))
        x_new.append(x)
    per = N_EXPERTS // N_EXPERT_GROUPS
    counts = cnt[:, 0].astype(I32).reshape(per, N_EXPERT_GROUPS).T.reshape(N_EXPERTS)
    start = jnp.cumsum(counts) - counts
    m = sum(grp.n_rows for grp in groups) * TOP_K
    steps = _expert_steps(counts, m)
    dests = []
    expert_ids = jnp.arange(N_EXPERTS, dtype=I32)[:, None, None]
    for hn, shared, e_t, p_t, w_t in routed:
        start_sel = jnp.sum(jnp.where(e_t[None] == expert_ids, start[:, None, None], 0), axis=0)
        dests.append((start_sel + p_t).T.reshape(-1))
    row_pair = jnp.argsort(jnp.concatenate(dests)).astype(I32)
    slots = _experts(steps, [r[0] for r in routed], row_pair, *expert_w)
    outs = []
    row0 = 0
    for cgrp, x, mod, (hn, shared, e_t, p_t, w_t) in zip(cgroups, x_new, mods, routed):
        outs.append(_combine(cgrp, row0, w_t.T.reshape(-1), x, shared, mod, fg, slots, final))
        row0 += cgrp.n_rows
    return outs


def kernel(x_prompt, x_sample, c_prompt, c_sample, state_conv, state_h, cache_k, cache_v, page_table, a_norm_g, a_mod_w, a_mod_b, a_w_in, a_conv_w, a_conv_b, a_w_gate_r, a_b_gate_r, a_w_gate_i, a_b_gate_i, a_lambda, a_w_out, kv_norm_g, kv_mod_w, kv_mod_b, w_kv, b_norm_g, b_mod_w, b_mod_b, b_w_q, b_sb_bias, b_w_o, m_norm_g, m_mod_w, m_mod_b, m_w_router, m_router_bias, m_w_gate_up, m_w_down, m_ws_gate_up, m_ws_down, final_norm_g):
    bp, t, d = x_prompt.shape
    bs, ts, _ = x_sample.shape
    depth = m_norm_g.shape[0]
    n_a = a_norm_g.shape[0]
    d_rnn = a_w_out.shape[1]
    heads = N_HEADS
    dh = b_w_q.shape[-1] // heads
    page = cache_k.shape[1]
    np_rows = bp * t
    ns_rows = bs * ts
    assert n_a == 1 and depth == 2, "layer pattern of this step: one self-decoder, one cross-decoder layer"

    n_c = bp + bs
    c_rows = -(-n_c // SUBLANES) * SUBLANES
    c_all = jnp.zeros((c_rows, d), F32).at[:bp].set(c_prompt).at[bp:n_c].set(c_sample)

    def mods(w, b):
        m = _modulation(c_all, w, b)
        mp = m[:bp].reshape(bp, 1, -1)
        ms = jnp.tile(m[bp:n_c], (ts, 1)).reshape(1, ns_rows, -1)
        return mp, ms

    a_mod = mods(a_mod_w[0], a_mod_b[0])
    kv_mod = mods(kv_mod_w, kv_mod_b)
    b_mod = mods(b_mod_w[0], b_mod_b[0])
    m_mod = [mods(m_mod_w[l], m_mod_b[l]) for l in range(depth)]

    bf = lambda w: w.astype(BF16)

    rec_w = (a_norm_g[0], bf(a_w_in[0]), a_conv_w[0], a_conv_b[0], bf(a_w_gate_r[0]), a_b_gate_r[0],
             bf(a_w_gate_i[0]), a_b_gate_i[0], a_lambda[0], bf(a_w_out[0]))
    hist_p = jnp.zeros((bp, CONV_W - 1, d_rnn), F32)
    h0_p = jnp.zeros((bp, 1, d_rnn), F32)
    x1_p, conv_p, h_p = _recurrent(x_prompt, a_mod[0], hist_p, h0_p, *rec_w,
                                   stride=1, steps=REC_ROWS, mod_rows=1)
    xs_tm = x_sample.swapaxes(0, 1).reshape(1, ns_rows, d)
    hist_s = state_conv[0].swapaxes(0, 1).reshape(1, (CONV_W - 1) * bs, d_rnn)
    h0_s = state_h[0].reshape(1, bs, d_rnn)
    x1_s, conv_s, h_s = _recurrent(xs_tm, a_mod[1], hist_s, h0_s, *rec_w,
                                   stride=bs, steps=ts, mod_rows=ns_rows)

    grp_p = _Group(np_rows, ROUTER_ROWS, t // ROUTER_ROWS, 1)
    grp_s = _Group(ns_rows, ns_rows, 1, ns_rows)
    groups = [grp_p, grp_s]
    cgroups = [_Group(np_rows, COMBINE_ROWS, t // COMBINE_ROWS, 1), grp_s]

    x2_p, x2_s = _moe_layer(groups, cgroups, [x1_p.reshape(np_rows, d), x1_s.reshape(ns_rows, d)],
                            [m_mod[0][0], m_mod[0][1]], [None, None], m_norm_g[0], m_w_router[0],
                            m_router_bias[0], (m_w_gate_up, m_w_down, 0), bf(m_ws_gate_up[0]),
                            bf(m_ws_down[0]), final_norm_g, False)

    q_scale = float(dh) ** -0.5 * LOG2E
    w_kv_b, w_q_b = bf(w_kv), bf(b_w_q[0])
    pgrp_p = _Group(np_rows, PROJ_ROWS, t // PROJ_ROWS, 1)
    kt_p, vt_p, kb_p, vb_p, q_p = _proj(pgrp_p, x2_p, kv_mod[0], kv_norm_g, w_kv_b, b_mod[0], b_norm_g[0],
                                        w_q_b, q_scale, n_seq=bp)
    k_s, v_s, q_s = _proj(grp_s, x2_s, kv_mod[1], kv_norm_g, w_kv_b, b_mod[1], b_norm_g[0],
                          w_q_b, q_scale)

    hd = heads * dh
    o_p = _attn_prompt(q_p.reshape(bp, t, hd), kb_p.reshape(bp, t, hd), vb_p.reshape(bp, t, hd),
                       b_sb_bias[0]).reshape(np_rows, hd)

    q_pad = SUBLANES
    q3 = jnp.zeros((bs, heads, q_pad, dh), BF16).at[:, :, :ts].set(
        q_s.reshape(ts, bs, heads, dh).transpose(1, 2, 0, 3))

    def new_keys(a):
        a = a.reshape(ts, bs, heads, dh).transpose(1, 2, 3, 0)
        return jnp.zeros((bs, heads, dh, page), F32).at[..., :ts].set(a)

    bias_rows = jnp.broadcast_to(jnp.repeat(b_sb_bias[0] * LOG2E, q_pad)[:, None], (heads * q_pad, page))
    o3 = _attn_sample(page_table, bias_rows, q3, new_keys(k_s), new_keys(v_s),
                      cache_k.transpose(0, 2, 3, 1), cache_v.transpose(0, 2, 3, 1), SAMPLE_PAGES_PER_STEP)
    o_s = o3[:, :, :ts].transpose(2, 0, 1, 3).reshape(ns_rows, hd).astype(BF16)

    w_o_b = bf(b_w_o[0])
    y_p, y_s = _moe_layer(groups, cgroups, [x2_p, x2_s], [m_mod[1][0], m_mod[1][1]],
                          [(o_p, w_o_b, b_mod[0]), (o_s, w_o_b, b_mod[1])], m_norm_g[1], m_w_router[1],
                          m_router_bias[1], (m_w_gate_up, m_w_down, 1), bf(m_ws_gate_up[1]),
                          bf(m_ws_down[1]), final_norm_g, True)

    def from_tm(a, *tail):
        return a.reshape(ts, bs, *tail).swapaxes(0, 1)

    return (y_p.reshape(bp, t, d),
            from_tm(y_s, d),
            conv_p[None],
            h_p.reshape(1, bp, d_rnn),
            kt_p.reshape(bp, heads, dh, t).transpose(0, 3, 1, 2),
            vt_p.reshape(bp, heads, dh, t).transpose(0, 3, 1, 2),
            conv_s.reshape(CONV_W - 1, bs, d_rnn).swapaxes(0, 1)[None],
            h_s.reshape(1, bs, d_rnn),
            from_tm(k_s, heads, dh),
            from_tm(v_s, heads, dh))
```
